```python
import math
import jax
import jax.numpy as jnp
from jax import lax
import numpy as np

D_MODEL = 1024
BATCH = 8
SEQ = 2048
DEPTH = 2

D_MIX = D_MODEL
HEAD_DIM = 64
RWKV_WIDTH = D_MIX // 2
RWKV_HEADS = RWKV_WIDTH // HEAD_DIM
MOBA_WIDTH = D_MIX - RWKV_WIDTH
MOBA_HEADS = MOBA_WIDTH // HEAD_DIM
DECAY_LORA = 64
ICLR_LORA = 64
GATE_LORA = 128
RWKV_PROJ = 3 * RWKV_WIDTH + DECAY_LORA + ICLR_LORA + GATE_LORA
MOBA_PROJ = 3 * MOBA_WIDTH
D_PROJ = RWKV_PROJ + MOBA_PROJ
LNX_EPS = 64e-5
MOBA_BLOCK = 256
MOBA_TOPK = 3
MOBA_Q_CHUNK = 16
REL_BUCKETS = 32
REL_MAX_DISTANCE = 1024
MEM_LEN = 256
XATTN_HEADS = 4
XATTN_HEAD_DIM = D_MODEL // XATTN_HEADS
D_FF = ((8 * D_MODEL // 3 + 127) // 128) * 128
FFN_RES_WEIGHT = 0.5
NORM_EPS = 1e-6

kernel_name = "hybrid_rwkv7_moba_macaron"


def rms_norm(x, g):
    xf = x.astype(jnp.float32)
    y = xf * lax.rsqrt(jnp.mean(xf * xf, axis=-1, keepdims=True) + NORM_EPS)
    return (y * g.astype(jnp.float32)).astype(x.dtype)


def swiglu_ffn(h, w_in, w_out):
    gate, up = jnp.split(h @ w_in, 2, axis=-1)
    return (jax.nn.silu(gate) * up) @ w_out


def token_shift(z):
    return jnp.pad(z, ((0, 0), (1, 0), (0, 0)))[:, :-1]


def rel_bucket(dist):
    n = jnp.maximum(dist, 0)
    max_exact = REL_BUCKETS // 2
    nf = jnp.maximum(n, 1).astype(jnp.float32)
    large = max_exact + (jnp.log(nf / max_exact) / math.log(REL_MAX_DISTANCE / max_exact)
                         * (REL_BUCKETS - max_exact)).astype(jnp.int32)
    large = jnp.minimum(large, REL_BUCKETS - 1)
    return jnp.where(n < max_exact, n, large)


def rwkv7_time_mix(p, mu, w0, w_up, a0, a_up, g_up, k_k, k_a, r_k, ln_g, ln_b):
    B, S, _ = p.shape
    H, Dh, W = RWKV_HEADS, HEAD_DIM, RWKV_WIDTH
    dt = p.dtype
    f32 = jnp.float32
    p = p + (token_shift(p) - p) * mu
    r, k, v, w_lo, a_lo, g_lo = jnp.split(
        p, [W, 2 * W, 3 * W, 3 * W + DECAY_LORA, 3 * W + DECAY_LORA + ICLR_LORA], axis=-1)
    w_log = -jax.nn.softplus(-(w0 + jnp.tanh(w_lo) @ w_up).astype(f32)) - 0.5
    decay = jnp.exp(-jnp.exp(w_log))
    a = jax.nn.sigmoid(a0 + a_lo @ a_up)
    g = jax.nn.sigmoid(g_lo) @ g_up
    kk = k * k_k
    k = k * (1 + (a - 1) * k_a)
    heads = lambda z: z.reshape(B, S, H, Dh).astype(f32)
    r, k, v, a, decay, kk = (heads(z) for z in (r, k, v, a, decay, kk))
    kk = kk / jnp.maximum(jnp.sqrt(jnp.sum(kk * kk, axis=-1, keepdims=True)), 1e-12)
    b = kk * a

    def step(state, inp):
        r_t, w_t, k_t, v_t, kk_t, b_t = inp
        sa = jnp.einsum('bhij,bhj->bhi', state, -kk_t)
        state = (state * w_t[:, :, None, :] + sa[..., None] * b_t[:, :, None, :]
                 + v_t[..., None] * k_t[:, :, None, :])
        return state, jnp.einsum('bhij,bhj->bhi', state, r_t)

    tm = lambda z: jnp.swapaxes(z, 0, 1)
    state0 = jnp.zeros((B, H, Dh, Dh), f32)
    _, y = lax.scan(step, state0, tuple(tm(z) for z in (r, decay, k, v, kk, b)))
    y = tm(y)
    mean = jnp.mean(y, axis=-1, keepdims=True)
    var = jnp.mean(jnp.square(y - mean), axis=-1, keepdims=True)
    y = ((y - mean) * lax.rsqrt(var + LNX_EPS)).reshape(B, S, W) * ln_g + ln_b
    bonus = jnp.sum(r * k * r_k, axis=-1, keepdims=True) * v
    y = y + bonus.reshape(B, S, W)
    return (y * g).astype(dt)


def moba_attention(q, k, v, rel_bias):
    B, S, _ = q.shape
    H, Dh, BLK, QC = MOBA_HEADS, HEAD_DIM, MOBA_BLOCK, MOBA_Q_CHUNK
    f32 = jnp.float32
    nb = -(-S // BLK)
    pad = nb * BLK - S
    to_heads = lambda z: z.reshape(B, S, H, Dh).transpose(0, 2, 1, 3)
    q = to_heads(q) * (Dh ** -0.5)
    k = jnp.pad(to_heads(k), ((0, 0), (0, 0), (0, pad), (0, 0)))
    v = jnp.pad(to_heads(v), ((0, 0), (0, 0), (0, pad), (0, 0)))
    kb = k.reshape(B, H, nb, BLK, Dh)
    vb = v.reshape(B, H, nb, BLK, Dh)
    k_mean = jnp.mean(kb.astype(f32), axis=3).astype(q.dtype)
    topk = min(MOBA_TOPK, nb)
    n_chunks = S // QC
    qc = q.reshape(B, H, n_chunks, QC, Dh).transpose(2, 0, 1, 3, 4)
    b_ix = jnp.arange(B)[:, None, None, None]
    h_ix = jnp.arange(H)[None, :, None, None]
    blk_ar = jnp.arange(BLK)

    def chunk_fn(args):
        q_c, c = args
        q_start = c * QC
        q_pos = q_start + jnp.arange(QC)
        q_blk = q_start // BLK
        gate = jnp.einsum('bhqd,bhnd->bhqn', q_c, k_mean)
        gate = jnp.where(jnp.arange(nb) < q_blk, gate, -jnp.inf)
        _, idx = lax.top_k(gate, topk)
        valid = idx < q_blk
        k_sel = kb[b_ix, h_ix, idx]
        v_sel = vb[b_ix, h_ix, idx]
        k_pos_sel = idx[..., None] * BLK + blk_ar
        bias_sel = rel_bias[h_ix[..., None], rel_bucket(q_pos[None, None, :, None, None] - k_pos_sel)]
        s_sel = jnp.einsum('bhqd,bhqtkd->bhqtk', q_c, k_sel).astype(f32) + bias_sel.astype(f32)
        s_sel = jnp.where(valid[..., None], s_sel, -jnp.inf)
        k_own = lax.dynamic_index_in_dim(kb, q_blk, axis=2, keepdims=False)
        v_own = lax.dynamic_index_in_dim(vb, q_blk, axis=2, keepdims=False)
        own_pos = q_blk * BLK + blk_ar
        bias_own = rel_bias[:, rel_bucket(q_pos[:, None] - own_pos[None, :])]
        s_own = jnp.einsum('bhqd,bhkd->bhqk', q_c, k_own).astype(f32) + bias_own.astype(f32)[None]
        s_own = jnp.where(own_pos[None, :] <= q_pos[:, None], s_own, -jnp.inf)
        logits = jnp.concatenate([s_sel.reshape(B, H, QC, topk * BLK), s_own], axis=-1)
        probs = jax.nn.softmax(logits, axis=-1).astype(v.dtype)
        p_sel = probs[..., :topk * BLK].reshape(B, H, QC, topk, BLK)
        p_own = probs[..., topk * BLK:]
        return (jnp.einsum('bhqtk,bhqtkd->bhqd', p_sel, v_sel)
                + jnp.einsum('bhqk,bhkd->bhqd', p_own, v_own))

    out = lax.map(chunk_fn, (qc, jnp.arange(n_chunks)))
    return out.transpose(1, 0, 3, 2, 4).reshape(B, S, H * Dh)


def memory_cross_attention(h, mem_n, w_q, w_kv, w_o):
    B, S, D = h.shape
    M = mem_n.shape[1]
    q = (h @ w_q).reshape(B, S, XATTN_HEADS, XATTN_HEAD_DIM)
    k, v = jnp.split(mem_n @ w_kv, 2, axis=-1)
    k = k.reshape(B, M, XATTN_HEADS, XATTN_HEAD_DIM)
    v = v.reshape(B, M, XATTN_HEADS, XATTN_HEAD_DIM)
    logits = jnp.einsum('bshd,bmhd->bhsm', q, k).astype(jnp.float32) * (XATTN_HEAD_DIM ** -0.5)
    probs = jax.nn.softmax(logits, axis=-1).astype(v.dtype)
    o = jnp.einsum('bhsm,bmhd->bshd', probs, v).reshape(B, S, D)
    return o @ w_o


def setup_inputs(seed: int = 0) -> dict:
    key = jax.random.key(seed)
    ks = iter(jax.random.split(key, 40))
    f32 = jnp.float32

    def nrm(shape, scale):
        return scale * jax.random.normal(next(ks), shape, f32)

    def dense(fan_in, fan_out):
        return nrm((DEPTH, fan_in, fan_out), fan_in ** -0.5)

    def gain(n):
        return 1.0 + nrm((DEPTH, n), 0.02)

    W = RWKV_WIDTH
    return {
        "x": nrm((BATCH, SEQ, D_MODEL), 1.0),
        "mem": nrm((BATCH, MEM_LEN, D_MODEL), 1.0),
        "rel_bias": nrm((MOBA_HEADS, REL_BUCKETS), 0.5),
        "final_norm_g": 1.0 + nrm((D_MODEL,), 0.02),
        "ffn1_norm_g": gain(D_MODEL),
        "ffn1_w_in": dense(D_MODEL, 2 * D_FF),
        "ffn1_w_out": dense(D_FF, D_MODEL),
        "mix_norm_g": gain(D_MODEL),
        "w_mix_in": dense(D_MODEL, D_PROJ),
        "w_mix_out": dense(D_MIX, D_MODEL),
        "rwkv_mu": jax.random.uniform(next(ks), (DEPTH, RWKV_PROJ), f32),
        "rwkv_w0": jnp.linspace(-6.0, -1.0, W, dtype=f32)[None, :] + nrm((DEPTH, W), 0.1),
        "rwkv_w_up": nrm((DEPTH, DECAY_LORA, W), 0.5 * DECAY_LORA ** -0.5),
        "rwkv_a0": nrm((DEPTH, W), 0.1),
        "rwkv_a_up": dense(ICLR_LORA, W),
        "rwkv_g_up": dense(GATE_LORA, W),
        "rwkv_k_k": 0.85 + nrm((DEPTH, W), 0.02),
        "rwkv_k_a": 1.0 + nrm((DEPTH, W), 0.02),
        "rwkv_r_k": nrm((DEPTH, RWKV_HEADS, HEAD_DIM), 0.1),
        "rwkv_ln_g": gain(W),
        "rwkv_ln_b": nrm((DEPTH, W), 0.02),
        "xattn_norm_g": gain(D_MODEL),
        "mem_norm_g": gain(D_MODEL),
        "xattn_w_q": dense(D_MODEL, D_MODEL),
        "xattn_w_kv": dense(D_MODEL, 2 * D_MODEL),
        "xattn_w_o": dense(D_MODEL, D_MODEL),
        "ffn2_norm_g": gain(D_MODEL),
        "ffn2_w_in": dense(D_MODEL, 2 * D_FF),
        "ffn2_w_out": dense(D_FF, D_MODEL),
    }


def reference(x, mem, rel_bias, final_norm_g, ffn1_norm_g, ffn1_w_in, ffn1_w_out, mix_norm_g,
              w_mix_in, w_mix_out, rwkv_mu, rwkv_w0, rwkv_w_up, rwkv_a0, rwkv_a_up, rwkv_g_up,
              rwkv_k_k, rwkv_k_a, rwkv_r_k, rwkv_ln_g, rwkv_ln_b, xattn_norm_g, mem_norm_g,
              xattn_w_q, xattn_w_kv, xattn_w_o, ffn2_norm_g, ffn2_w_in, ffn2_w_out):
    for l in range(DEPTH):
        x = x + FFN_RES_WEIGHT * swiglu_ffn(rms_norm(x, ffn1_norm_g[l]), ffn1_w_in[l], ffn1_w_out[l])
        proj = rms_norm(x, mix_norm_g[l]) @ w_mix_in[l]
        p_rwkv = proj[..., :RWKV_PROJ]
        q_m, k_m, v_m = jnp.split(proj[..., RWKV_PROJ:], 3, axis=-1)
        y_rwkv = rwkv7_time_mix(p_rwkv, rwkv_mu[l], rwkv_w0[l], rwkv_w_up[l], rwkv_a0[l],
                                rwkv_a_up[l], rwkv_g_up[l], rwkv_k_k[l], rwkv_k_a[l], rwkv_r_k[l],
                                rwkv_ln_g[l], rwkv_ln_b[l])
        y_moba = moba_attention(q_m, k_m, v_m, rel_bias)
        x = x + jnp.concatenate([y_rwkv, y_moba], axis=-1) @ w_mix_out[l]
        x = x + memory_cross_attention(rms_norm(x, xattn_norm_g[l]), rms_norm(mem, mem_norm_g[l]),
                                       xattn_w_q[l], xattn_w_kv[l], xattn_w_o[l])
        x = x + FFN_RES_WEIGHT * swiglu_ffn(rms_norm(x, ffn2_norm_g[l]), ffn2_w_in[l], ffn2_w_out[l])
    return rms_norm(x, final_norm_g)
```

```python
import functools
import math

import jax
import jax.numpy as jnp
from jax import lax
from jax.experimental import pallas as pl
from jax.experimental.pallas import tpu as pltpu

F32 = jnp.float32
BF16 = jnp.bfloat16
HIGHEST = lax.Precision.HIGHEST

HEAD_DIM = 64
RWKV_WIDTH = 512
MOBA_WIDTH = 512
MOBA_HEADS = MOBA_WIDTH // HEAD_DIM
DECAY_LORA = 64
ICLR_LORA = 64
GATE_LORA = 128
RWKV_PROJ = 3 * RWKV_WIDTH + DECAY_LORA + ICLR_LORA + GATE_LORA
MOBA_PROJ = 3 * MOBA_WIDTH
LNX_EPS = 64e-5
MOBA_BLOCK = 256
MOBA_TOPK = 3
REL_BUCKETS = 32
REL_MAX_DISTANCE = 1024
XATTN_HEADS = 4
FFN_RES_WEIGHT = 0.5
NORM_EPS = 1e-6

LANES = 128
HEADS_PER_TILE = LANES // HEAD_DIM
HEAD_SHIFT = HEAD_DIM.bit_length() - 1
RWKV_CHUNK = 64
VMEM_LIMIT = 56 * 1024 * 1024


def _rms(x, g):
    ms = jnp.mean(x * x, axis=-1, keepdims=True)
    return x * lax.rsqrt(ms + NORM_EPS) * g


def _dot(a, b, precision=None):
    return jnp.dot(a, b, precision=precision, preferred_element_type=F32)


def _dot_nt(a, b, precision=None):
    return lax.dot_general(a, b, (((1,), (1,)), ((), ())), precision=precision,
                           preferred_element_type=F32)


def _dot_tn(a, b, precision=None):
    return lax.dot_general(a, b, (((0,), (0,)), ((), ())), precision=precision,
                           preferred_element_type=F32)


def _params(*semantics):
    return pltpu.CompilerParams(dimension_semantics=semantics, vmem_limit_bytes=VMEM_LIMIT)


def _ffn_kernel(x_ref, g_ref, wg_ref, wu_ref, wo_ref, *rest, final_norm):
    if final_norm:
        fg_ref, o_ref, h_ref, acc_ref = rest
    else:
        o_ref, h_ref, acc_ref = rest
    j = pl.program_id(1)

    @pl.when(j == 0)
    def _():
        h_ref[...] = _rms(x_ref[...], g_ref[...]).astype(BF16)
        acc_ref[...] = jnp.zeros_like(acc_ref)

    h = h_ref[...]
    gate = _dot(h, wg_ref[...])
    up = _dot(h, wu_ref[...])
    act = gate * jax.nn.sigmoid(gate) * up
    acc_ref[...] += _dot(act.astype(BF16), wo_ref[...])

    @pl.when(j == pl.num_programs(1) - 1)
    def _():
        y = x_ref[...] + FFN_RES_WEIGHT * acc_ref[...]
        if final_norm:
            y = _rms(y, fg_ref[...])
        o_ref[...] = y


def _ffn(x, g, w_in, w_out, final_g=None, *, tm=512, tf=1408):
    T, D = x.shape
    F = w_out.shape[0]
    tm = min(tm, T)
    nf = F // tf
    in_specs = [
        pl.BlockSpec((tm, D), lambda i, j: (i, 0)),
        pl.BlockSpec((1, D), lambda i, j: (0, 0)),
        pl.BlockSpec((D, tf), lambda i, j: (0, j)),
        pl.BlockSpec((D, tf), lambda i, j: (0, j + nf)),
        pl.BlockSpec((tf, D), lambda i, j: (j, 0)),
    ]
    args = [x, g.reshape(1, D), w_in, w_in, w_out]
    if final_g is not None:
        in_specs.append(pl.BlockSpec((1, D), lambda i, j: (0, 0)))
        args.append(final_g.reshape(1, D))
    return pl.pallas_call(
        functools.partial(_ffn_kernel, final_norm=final_g is not None),
        grid=(T // tm, nf),
        in_specs=in_specs,
        out_specs=pl.BlockSpec((tm, D), lambda i, j: (i, 0)),
        out_shape=jax.ShapeDtypeStruct((T, D), F32),
        scratch_shapes=[pltpu.VMEM((tm, D), BF16), pltpu.VMEM((tm, D), F32)],
        compiler_params=_params("parallel", "arbitrary"),
        name="ffn",
    )(*args)


def _norm_proj_kernel(x_ref, g_ref, w_ref, *o_refs, splits):
    h = _rms(x_ref[...], g_ref[...]).astype(BF16)
    off = 0
    for o_ref, n in zip(o_refs, splits):
        o_ref[...] = _dot(h, w_ref[:, off:off + n])
        off += n


def _norm_proj(x, g, w, splits, *, tm=512):
    T, D = x.shape
    N = w.shape[1]
    tm = min(tm, T)
    assert sum(splits) == N
    return pl.pallas_call(
        functools.partial(_norm_proj_kernel, splits=splits),
        grid=(T // tm,),
        in_specs=[
            pl.BlockSpec((tm, D), lambda i: (i, 0)),
            pl.BlockSpec((1, D), lambda i: (0, 0)),
            pl.BlockSpec((D, N), lambda i: (0, 0)),
        ],
        out_specs=[pl.BlockSpec((tm, n), lambda i: (i, 0)) for n in splits],
        out_shape=[jax.ShapeDtypeStruct((T, n), F32) for n in splits],
        compiler_params=_params("parallel"),
        name="norm_proj",
    )(x, g.reshape(1, D), w)


def _rwkv_kernel(p_ref, mu_ref, w0_ref, a0_ref, wwa_ref, gup_ref, kk_ref, ka_ref, rk_ref,
                 lng_ref, lnb_ref, o_ref, prev_ref, state_ref):
    C = RWKV_CHUNK
    W = RWKV_WIDTH
    s_idx = pl.program_id(1)

    @pl.when(s_idx == 0)
    def _():
        prev_ref[...] = jnp.zeros_like(prev_ref)
        state_ref[...] = jnp.zeros_like(state_ref)

    p = p_ref[0]
    row = lax.broadcasted_iota(jnp.int32, (C, 1), 0)
    shifted = jnp.where(row == 0, prev_ref[...], pltpu.roll(p, 1, axis=0))
    prev_ref[...] = p[C - 1:C, :]
    p = p + (shifted - p) * mu_ref[...]

    r = p[:, 0:W]
    k = p[:, W:2 * W]
    v = p[:, 2 * W:3 * W]
    lora = p[:, 3 * W:3 * W + LANES]
    g_lo = p[:, 3 * W + LANES:3 * W + 2 * LANES]

    lane = lax.broadcasted_iota(jnp.int32, (1, LANES), 1)
    first_head = lane < HEAD_DIM
    z = jnp.where(first_head, jnp.tanh(lora), lora)
    wa = _dot(z, wwa_ref[...], HIGHEST)
    logw = -math.exp(-0.5) * jax.nn.sigmoid(w0_ref[...] + wa[:, :W])
    a = jax.nn.sigmoid(a0_ref[...] + wa[:, W:])
    g = _dot(jax.nn.sigmoid(g_lo), gup_ref[...], HIGHEST)

    kk = k * kk_ref[...]
    k2 = k * (1.0 + (a - 1.0) * ka_ref[...])

    ri = lax.broadcasted_iota(jnp.int32, (LANES, LANES), 0)
    ci = lax.broadcasted_iota(jnp.int32, (LANES, LANES), 1)
    same_head = (ri >> HEAD_SHIFT) == (ci >> HEAD_SHIFT)
    head_ones = same_head.astype(F32)

    def head_sum(t):
        return jnp.concatenate(
            [_dot(t[:, i * LANES:(i + 1) * LANES], head_ones, HIGHEST) for i in range(W // LANES)],
            axis=1)

    kkn = kk / jnp.maximum(jnp.sqrt(head_sum(kk * kk)), 1e-12)
    b = kkn * a
    bonus = head_sum(r * k2 * rk_ref[...]) * v

    tr = lax.broadcasted_iota(jnp.int32, (C, C), 0)
    tc = lax.broadcasted_iota(jnp.int32, (C, C), 1)
    logp = _dot((tc <= tr).astype(F32), logw, HIGHEST)
    logp_end = logp[C - 1:C, :]
    r_t = r * jnp.exp(logp)
    a_t = -kkn * jnp.exp(logp - logw)
    inv = jnp.exp(-logp)
    b_t = b * inv
    k_t = k2 * inv
    to_end = jnp.exp(logp_end - logp)
    b_e = b * to_end
    k_e = k2 * to_end
    decay_end = jnp.exp(logp_end)

    strict = same_head & (ci < ri)
    incl = same_head & (ci <= ri)
    eye = (ri == ci).astype(F32)

    def stack(t):
        return jnp.concatenate([jnp.where(first_head, t, 0.0), jnp.where(first_head, 0.0, t)], axis=0)

    ys = []
    for i in range(W // LANES):
        sl = slice(i * LANES, (i + 1) * LANES)
        a_s, r_s = stack(a_t[:, sl]), stack(r_t[:, sl])
        b_s, k_s, v_s = stack(b_t[:, sl]), stack(k_t[:, sl]), stack(v[:, sl])
        state = state_ref[i]

        gram = _dot_nt(jnp.concatenate([a_s, r_s], axis=0),
                       jnp.concatenate([b_s, k_s], axis=0), HIGHEST)
        n = 2 * C
        l_ab = jnp.where(strict, gram[:n, :n], 0.0)
        a_ak = jnp.where(strict, gram[:n, n:], 0.0)
        a_rb = jnp.where(incl, gram[n:, :n], 0.0)
        a_rk = jnp.where(incl, gram[n:, n:], 0.0)

        t_inv = eye + l_ab
        l_pow = l_ab
        for _ in range(int(math.log2(C)) - 1):
            l_pow = _dot(l_pow, l_pow, HIGHEST)
            t_inv = t_inv + _dot(l_pow, t_inv, HIGHEST)

        from_state = _dot_nt(jnp.concatenate([a_s, r_s], axis=0), state, HIGHEST)
        rhs = from_state[:n] + _dot(a_ak, v_s, HIGHEST)
        u_s = _dot(t_inv, rhs, HIGHEST)
        uv = jnp.concatenate([u_s, v_s], axis=0)
        y_s = from_state[n:] + _dot(jnp.concatenate([a_rb, a_rk], axis=1), uv, HIGHEST)
        ys.append(y_s[:C] + y_s[C:])

        bk_e = jnp.concatenate([stack(b_e[:, sl]), stack(k_e[:, sl])], axis=0)
        state_ref[i] = state * decay_end[:, sl] + _dot_tn(uv, bk_e, HIGHEST)

    y = jnp.concatenate(ys, axis=1)
    mean = head_sum(y) * (1.0 / HEAD_DIM)
    yc = y - mean
    var = head_sum(yc * yc) * (1.0 / HEAD_DIM)
    y = yc * lax.rsqrt(var + LNX_EPS) * lng_ref[...] + lnb_ref[...]
    o_ref[0] = (y + bonus) * g


def _rwkv(p, mu, w0, w_up, a0, a_up, g_up, k_k, k_a, r_k, ln_g, ln_b):
    B, S, P = p.shape
    W = RWKV_WIDTH
    C = RWKV_CHUNK
    wwa = jnp.zeros((LANES, 2 * W), F32)
    wwa = wwa.at[:DECAY_LORA, :W].set(w_up).at[DECAY_LORA:, W:].set(a_up)
    vec = lambda t: t.reshape(1, -1)
    row_spec = lambda n: pl.BlockSpec((1, n), lambda b, s: (0, 0))
    return pl.pallas_call(
        _rwkv_kernel,
        grid=(B, S // C),
        in_specs=[
            pl.BlockSpec((1, C, P), lambda b, s: (b, s, 0)),
            row_spec(P), row_spec(W), row_spec(W),
            pl.BlockSpec((LANES, 2 * W), lambda b, s: (0, 0)),
            pl.BlockSpec((GATE_LORA, W), lambda b, s: (0, 0)),
            row_spec(W), row_spec(W), row_spec(W), row_spec(W), row_spec(W),
        ],
        out_specs=pl.BlockSpec((1, C, W), lambda b, s: (b, s, 0)),
        out_shape=jax.ShapeDtypeStruct((B, S, W), F32),
        scratch_shapes=[pltpu.VMEM((1, P), F32),
                        pltpu.VMEM((W // LANES, LANES, LANES), F32)],
        compiler_params=_params("parallel", "arbitrary"),
        name="rwkv7",
    )(p, vec(mu), vec(w0), vec(a0), wwa, g_up, vec(k_k), vec(k_a), vec(r_k), vec(ln_g), vec(ln_b))


def _rel_bias_kernel(tab_ref, o_ref):
    d = pl.program_id(0)
    n_blk = MOBA_BLOCK
    qi = lax.broadcasted_iota(jnp.int32, (n_blk, n_blk), 0)
    kj = lax.broadcasted_iota(jnp.int32, (n_blk, n_blk), 1)
    n = jnp.maximum(qi - kj + d * n_blk, 0)
    max_exact = REL_BUCKETS // 2
    nf = jnp.maximum(n, 1).astype(F32)
    large = max_exact + (jnp.log(nf / max_exact) / math.log(REL_MAX_DISTANCE / max_exact)
                         * (REL_BUCKETS - max_exact)).astype(jnp.int32)
    large = jnp.minimum(large, REL_BUCKETS - 1)
    bucket = jnp.where(n < max_exact, n, large)
    for h in range(MOBA_HEADS):
        tile = jnp.zeros((n_blk, n_blk), F32)
        for c in range(REL_BUCKETS):
            tile = jnp.where(bucket == c, tab_ref[h, c], tile)
        o_ref[h, 0] = tile


def _rel_bias_tiles(rel_bias, n_blocks):
    return pl.pallas_call(
        _rel_bias_kernel,
        grid=(n_blocks,),
        in_specs=[pl.BlockSpec(memory_space=pltpu.SMEM)],
        out_specs=pl.BlockSpec((MOBA_HEADS, 1, MOBA_BLOCK, MOBA_BLOCK), lambda d: (0, d, 0, 0)),
        out_shape=jax.ShapeDtypeStruct((MOBA_HEADS, n_blocks, MOBA_BLOCK, MOBA_BLOCK), F32),
        compiler_params=_params("parallel"),
        name="rel_bias_tiles",
    )(rel_bias)


def _moba_kernel(q_ref, k_ref, v_ref, bias_ref, o_ref, kmean_ref, sel_ref, *, n_blocks):
    blk = MOBA_BLOCK
    qb = pl.program_id(2)

    @pl.when(qb == 0)
    def _():
        kmean_ref[...] = jnp.mean(k_ref[0].reshape(n_blocks, blk, LANES), axis=1)

    lane = lax.broadcasted_iota(jnp.int32, (1, LANES), 1)
    q = q_ref[0] * (HEAD_DIM ** -0.5)
    ri = lax.broadcasted_iota(jnp.int32, (blk, blk), 0)
    ci = lax.broadcasted_iota(jnp.int32, (blk, blk), 1)
    causal = ci <= ri
    own_start = pl.multiple_of(qb * blk, blk)

    outs = []
    for e in range(HEADS_PER_TILE):
        head_lanes = (lane >> HEAD_SHIFT) == e
        qe = jnp.where(head_lanes, q, 0.0)
        gate = _dot_nt(qe, kmean_ref[...], HIGHEST)
        cols = [gate[:, j:j + 1] for j in range(n_blocks)]
        for j in range(n_blocks - 1):
            rank = jnp.zeros((blk, 1), jnp.int32)
            for jj in range(n_blocks - 1):
                if jj == j:
                    continue
                ahead = (cols[jj] >= cols[j]) if jj < j else (cols[jj] > cols[j])
                rank = rank + jnp.where(ahead, 1, 0) * (jj < qb).astype(jnp.int32)
            sel_ref[e, j] = jnp.where(rank < MOBA_TOPK, 1.0, 0.0)

        qe16 = qe.astype(BF16)

        def scores(start, bias):
            kb = k_ref[0, pl.ds(start, blk), :].astype(BF16)
            return _dot_nt(qe16, kb) + bias

        s = jnp.where(causal, scores(own_start, bias_ref[e, 0]), -jnp.inf)
        m = jnp.max(s, axis=-1, keepdims=True)
        pr = jnp.exp(s - m)
        l = jnp.sum(pr, axis=-1, keepdims=True)
        acc = _dot(pr.astype(BF16), v_ref[0, pl.ds(own_start, blk), :].astype(BF16))

        def past_block(j, carry):
            m, l, acc = carry
            start = pl.multiple_of(j * blk, blk)
            s = jnp.where(sel_ref[e, j] > 0.5, scores(start, bias_ref[e, qb - j]), -jnp.inf)
            m_new = jnp.maximum(m, jnp.max(s, axis=-1, keepdims=True))
            alpha = jnp.exp(m - m_new)
            pr = jnp.exp(s - m_new)
            l = alpha * l + jnp.sum(pr, axis=-1, keepdims=True)
            acc = alpha * acc + _dot(pr.astype(BF16), v_ref[0, pl.ds(start, blk), :].astype(BF16))
            return m_new, l, acc

        m, l, acc = lax.fori_loop(0, qb, past_block, (m, l, acc))
        outs.append(acc / l)

    o_ref[0] = jnp.where(lane < HEAD_DIM, outs[0], outs[1])


def _moba(qkv, bias_tiles):
    B, S, _ = qkv.shape
    blk = MOBA_BLOCK
    n_blocks = S // blk
    n_pairs = MOBA_WIDTH // LANES
    return pl.pallas_call(
        functools.partial(_moba_kernel, n_blocks=n_blocks),
        grid=(n_pairs, B, n_blocks),
        in_specs=[
            pl.BlockSpec((1, blk, LANES), lambda p, b, i: (b, i, p)),
            pl.BlockSpec((1, S, LANES), lambda p, b, i: (b, 0, n_pairs + p)),
            pl.BlockSpec((1, S, LANES), lambda p, b, i: (b, 0, 2 * n_pairs + p)),
            pl.BlockSpec((HEADS_PER_TILE, n_blocks, blk, blk), lambda p, b, i: (p, 0, 0, 0)),
        ],
        out_specs=pl.BlockSpec((1, blk, LANES), lambda p, b, i: (b, i, p)),
        out_shape=jax.ShapeDtypeStruct((B, S, MOBA_WIDTH), F32),
        scratch_shapes=[pltpu.VMEM((n_blocks, LANES), F32),
                        pltpu.VMEM((HEADS_PER_TILE, n_blocks, blk, 1), F32)],
        compiler_params=_params("parallel", "parallel", "arbitrary"),
        name="moba",
    )(qkv, qkv, qkv, bias_tiles)


def _mix_out_kernel(x_ref, a_ref, b_ref, wa_ref, wb_ref, o_ref):
    o_ref[...] = (x_ref[...] + _dot(a_ref[...].astype(BF16), wa_ref[...])
                  + _dot(b_ref[...].astype(BF16), wb_ref[...]))


def _mix_out(x, ya, yb, w, *, tm=512):
    T, D = x.shape
    na, nb = ya.shape[1], yb.shape[1]
    tm = min(tm, T)
    return pl.pallas_call(
        _mix_out_kernel,
        grid=(T // tm,),
        in_specs=[
            pl.BlockSpec((tm, D), lambda i: (i, 0)),
            pl.BlockSpec((tm, na), lambda i: (i, 0)),
            pl.BlockSpec((tm, nb), lambda i: (i, 0)),
            pl.BlockSpec((na, D), lambda i: (0, 0)),
            pl.BlockSpec((nb, D), lambda i: (0, 0)),
        ],
        out_specs=pl.BlockSpec((tm, D), lambda i: (i, 0)),
        out_shape=jax.ShapeDtypeStruct((T, D), F32),
        compiler_params=_params("parallel"),
        name="mix_out",
    )(x, ya, yb, w[:na], w[na:])


def _xattn_kernel(x_ref, g_ref, wq_ref, kv_ref, wo_ref, o_ref):
    x = x_ref[0]
    D = x.shape[-1]
    dh = D // XATTN_HEADS
    h = _rms(x, g_ref[...]).astype(BF16)
    q = _dot(h, wq_ref[...])
    heads = []
    for i in range(XATTN_HEADS):
        qh = q[:, i * dh:(i + 1) * dh].astype(BF16)
        kh = kv_ref[0, :, i * dh:(i + 1) * dh].astype(BF16)
        vh = kv_ref[0, :, D + i * dh:D + (i + 1) * dh].astype(BF16)
        s = _dot_nt(qh, kh) * (dh ** -0.5)
        s = s - jnp.max(s, axis=-1, keepdims=True)
        pr = jnp.exp(s)
        pr = pr / jnp.sum(pr, axis=-1, keepdims=True)
        heads.append(_dot(pr.astype(BF16), vh))
    o = jnp.concatenate(heads, axis=1).astype(BF16)
    o_ref[0] = x + _dot(o, wo_ref[...])


def _xattn(x, g, w_q, kv, w_o, *, tm=512):
    B, S, D = x.shape
    M = kv.shape[1]
    tm = min(tm, S)
    return pl.pallas_call(
        _xattn_kernel,
        grid=(B, S // tm),
        in_specs=[
            pl.BlockSpec((1, tm, D), lambda b, i: (b, i, 0)),
            pl.BlockSpec((1, D), lambda b, i: (0, 0)),
            pl.BlockSpec((D, D), lambda b, i: (0, 0)),
            pl.BlockSpec((1, M, 2 * D), lambda b, i: (b, 0, 0)),
            pl.BlockSpec((D, D), lambda b, i: (0, 0)),
        ],
        out_specs=pl.BlockSpec((1, tm, D), lambda b, i: (b, i, 0)),
        out_shape=jax.ShapeDtypeStruct((B, S, D), F32),
        compiler_params=_params("parallel", "parallel"),
        name="xattn",
    )(x, g.reshape(1, D), w_q, kv, w_o)


def kernel(x, mem, rel_bias, final_norm_g, ffn1_norm_g, ffn1_w_in, ffn1_w_out, mix_norm_g, w_mix_in, w_mix_out, rwkv_mu, rwkv_w0, rwkv_w_up, rwkv_a0, rwkv_a_up, rwkv_g_up, rwkv_k_k, rwkv_k_a, rwkv_r_k, rwkv_ln_g, rwkv_ln_b, xattn_norm_g, mem_norm_g, xattn_w_q, xattn_w_kv, xattn_w_o, ffn2_norm_g, ffn2_w_in, ffn2_w_out):
    B, S, D = x.shape
    M = mem.shape[1]
    depth = ffn1_w_in.shape[0]
    T = B * S
    bf = lambda w: w.astype(BF16)

    bias_tiles = _rel_bias_tiles(rel_bias, S // MOBA_BLOCK)
    mem2 = mem.reshape(B * M, D)
    x = x.reshape(T, D)
    for l in range(depth):
        x = _ffn(x, ffn1_norm_g[l], bf(ffn1_w_in[l]), bf(ffn1_w_out[l]))
        p_rwkv, qkv = _norm_proj(x, mix_norm_g[l], bf(w_mix_in[l]), (RWKV_PROJ, MOBA_PROJ))
        y_rwkv = _rwkv(p_rwkv.reshape(B, S, RWKV_PROJ), rwkv_mu[l], rwkv_w0[l], rwkv_w_up[l],
                       rwkv_a0[l], rwkv_a_up[l], rwkv_g_up[l], rwkv_k_k[l], rwkv_k_a[l],
                       rwkv_r_k[l], rwkv_ln_g[l], rwkv_ln_b[l])
        y_moba = _moba(qkv.reshape(B, S, MOBA_PROJ), bias_tiles)
        x = _mix_out(x, y_rwkv.reshape(T, RWKV_WIDTH), y_moba.reshape(T, MOBA_WIDTH), bf(w_mix_out[l]))
        (kv,) = _norm_proj(mem2, mem_norm_g[l], bf(xattn_w_kv[l]), (2 * D,))
        x = _xattn(x.reshape(B, S, D), xattn_norm_g[l], bf(xattn_w_q[l]), kv.reshape(B, M, 2 * D),
                   bf(xattn_w_o[l])).reshape(T, D)
        x = _ffn(x, ffn2_norm_g[l], bf(ffn2_w_in[l]), bf(ffn2_w_out[l]),
                 final_norm_g if l == depth - 1 else None)
    return x.reshape(B, S, D)
```

```python
import functools
import math

import jax
import jax.numpy as jnp
from jax import lax
from jax.experimental import pallas as pl
from jax.experimental.pallas import tpu as pltpu

F32 = jnp.float32
BF16 = jnp.bfloat16
HIGHEST = lax.Precision.HIGHEST

HEAD_DIM = 64
RWKV_WIDTH = 512
MOBA_WIDTH = 512
MOBA_HEADS = MOBA_WIDTH // HEAD_DIM
DECAY_LORA = 64
ICLR_LORA = 64
GATE_LORA = 128
RWKV_PROJ = 3 * RWKV_WIDTH + DECAY_LORA + ICLR_LORA + GATE_LORA
MOBA_PROJ = 3 * MOBA_WIDTH
LNX_EPS = 64e-5
MOBA_BLOCK = 256
MOBA_TOPK = 3
REL_BUCKETS = 32
REL_MAX_DISTANCE = 1024
XATTN_HEADS = 4
FFN_RES_WEIGHT = 0.5
NORM_EPS = 1e-6

LANES = 128
HEADS_PER_TILE = LANES // HEAD_DIM
HEAD_SHIFT = HEAD_DIM.bit_length() - 1
RWKV_CHUNK = 64
SPLIT_LORA = (1, 1, 1)
SPLIT_SUM = (2, 1, 2)
SPLIT_CUMSUM = (1, 2, 2)
SPLIT_CHUNK = (1, 1, 1)
VMEM_LIMIT = 56 * 1024 * 1024


def _rms(x, g):
    ms = jnp.mean(x * x, axis=-1, keepdims=True)
    return x * lax.rsqrt(ms + NORM_EPS) * g


def _dot(a, b, precision=None):
    return jnp.dot(a, b, precision=precision, preferred_element_type=F32)


def _dot_nt(a, b, precision=None):
    return lax.dot_general(a, b, (((1,), (1,)), ((), ())), precision=precision,
                           preferred_element_type=F32)


def _dot_tn(a, b, precision=None):
    return lax.dot_general(a, b, (((0,), (0,)), ((), ())), precision=precision,
                           preferred_element_type=F32)


def _bf16_terms(x, n):
    terms = []
    for i in range(n):
        t = x.astype(BF16)
        terms.append(t)
        if i + 1 < n:
            x = x - t.astype(F32)
    return terms


def _mm(dot, a, b, split):
    na, nb, order = split
    at, bt = _bf16_terms(a, na), _bf16_terms(b, nb)
    acc = None
    for i in range(na):
        for j in range(nb):
            if i + j < order:
                t = dot(at[i], bt[j])
                acc = t if acc is None else acc + t
    return acc


def _params(*semantics):
    return pltpu.CompilerParams(dimension_semantics=semantics, vmem_limit_bytes=VMEM_LIMIT)


def _ffn_kernel(x_ref, g_ref, wg_ref, wu_ref, wo_ref, *rest, final_norm):
    if final_norm:
        fg_ref, o_ref, h_ref, acc_ref = rest
    else:
        o_ref, h_ref, acc_ref = rest
    j = pl.program_id(1)

    @pl.when(j == 0)
    def _():
        h_ref[...] = _rms(x_ref[...], g_ref[...]).astype(BF16)
        acc_ref[...] = jnp.zeros_like(acc_ref)

    h = h_ref[...]
    gate = _dot(h, wg_ref[...])
    up = _dot(h, wu_ref[...])
    act = gate * jax.nn.sigmoid(gate) * up
    acc_ref[...] += _dot(act.astype(BF16), wo_ref[...])

    @pl.when(j == pl.num_programs(1) - 1)
    def _():
        y = x_ref[...] + FFN_RES_WEIGHT * acc_ref[...]
        if final_norm:
            y = _rms(y, fg_ref[...])
        o_ref[...] = y


def _ffn(x, g, w_in, w_out, final_g=None, *, tm=512, tf=1408):
    T, D = x.shape
    F = w_out.shape[0]
    tm = min(tm, T)
    nf = F // tf
    in_specs = [
        pl.BlockSpec((tm, D), lambda i, j: (i, 0)),
        pl.BlockSpec((1, D), lambda i, j: (0, 0)),
        pl.BlockSpec((D, tf), lambda i, j: (0, j)),
        pl.BlockSpec((D, tf), lambda i, j: (0, j + nf)),
        pl.BlockSpec((tf, D), lambda i, j: (j, 0)),
    ]
    args = [x, g.reshape(1, D), w_in, w_in, w_out]
    if final_g is not None:
        in_specs.append(pl.BlockSpec((1, D), lambda i, j: (0, 0)))
        args.append(final_g.reshape(1, D))
    return pl.pallas_call(
        functools.partial(_ffn_kernel, final_norm=final_g is not None),
        grid=(T // tm, nf),
        in_specs=in_specs,
        out_specs=pl.BlockSpec((tm, D), lambda i, j: (i, 0)),
        out_shape=jax.ShapeDtypeStruct((T, D), F32),
        scratch_shapes=[pltpu.VMEM((tm, D), BF16), pltpu.VMEM((tm, D), F32)],
        compiler_params=_params("parallel", "arbitrary"),
        name="ffn",
    )(*args)


def _norm_proj_kernel(x_ref, g_ref, w_ref, *o_refs, splits):
    h = _rms(x_ref[...], g_ref[...]).astype(BF16)
    off = 0
    for o_ref, n in zip(o_refs, splits):
        o_ref[...] = _dot(h, w_ref[:, off:off + n])
        off += n


def _norm_proj(x, g, w, splits, *, tm=512):
    T, D = x.shape
    N = w.shape[1]
    tm = min(tm, T)
    assert sum(splits) == N
    return pl.pallas_call(
        functools.partial(_norm_proj_kernel, splits=splits),
        grid=(T // tm,),
        in_specs=[
            pl.BlockSpec((tm, D), lambda i: (i, 0)),
            pl.BlockSpec((1, D), lambda i: (0, 0)),
            pl.BlockSpec((D, N), lambda i: (0, 0)),
        ],
        out_specs=[pl.BlockSpec((tm, n), lambda i: (i, 0)) for n in splits],
        out_shape=[jax.ShapeDtypeStruct((T, n), F32) for n in splits],
        compiler_params=_params("parallel"),
        name="norm_proj",
    )(x, g.reshape(1, D), w)


def _rwkv_kernel(p_ref, mu_ref, w0_ref, a0_ref, wwa_ref, gup_ref, kk_ref, ka_ref, rk_ref,
                 lng_ref, lnb_ref, o_ref, prev_ref, state_ref):
    C = RWKV_CHUNK
    W = RWKV_WIDTH
    s_idx = pl.program_id(1)

    @pl.when(s_idx == 0)
    def _():
        prev_ref[...] = jnp.zeros_like(prev_ref)
        state_ref[...] = jnp.zeros_like(state_ref)

    p = p_ref[0]
    row = lax.broadcasted_iota(jnp.int32, (C, 1), 0)
    shifted = jnp.where(row == 0, prev_ref[...], pltpu.roll(p, 1, axis=0))
    prev_ref[...] = p[C - 1:C, :]
    p = p + (shifted - p) * mu_ref[...]

    r = p[:, 0:W]
    k = p[:, W:2 * W]
    v = p[:, 2 * W:3 * W]
    lora = p[:, 3 * W:3 * W + LANES]
    g_lo = p[:, 3 * W + LANES:3 * W + 2 * LANES]

    lane = lax.broadcasted_iota(jnp.int32, (1, LANES), 1)
    first_head = lane < HEAD_DIM
    z = jnp.where(first_head, jnp.tanh(lora), lora)
    wa = _mm(_dot, z, wwa_ref[...], SPLIT_LORA)
    logw = -math.exp(-0.5) * jax.nn.sigmoid(w0_ref[...] + wa[:, :W])
    a = jax.nn.sigmoid(a0_ref[...] + wa[:, W:])
    g = _mm(_dot, jax.nn.sigmoid(g_lo), gup_ref[...], SPLIT_LORA)

    kk = k * kk_ref[...]
    k2 = k * (1.0 + (a - 1.0) * ka_ref[...])

    ri = lax.broadcasted_iota(jnp.int32, (LANES, LANES), 0)
    ci = lax.broadcasted_iota(jnp.int32, (LANES, LANES), 1)
    same_head = (ri >> HEAD_SHIFT) == (ci >> HEAD_SHIFT)
    head_ones = same_head.astype(F32)

    def head_sum(t):
        return jnp.concatenate(
            [_mm(_dot, t[:, i * LANES:(i + 1) * LANES], head_ones, SPLIT_SUM)
             for i in range(W // LANES)],
            axis=1)

    kkn = kk / jnp.maximum(jnp.sqrt(head_sum(kk * kk)), 1e-12)
    b = kkn * a
    bonus = head_sum(r * k2 * rk_ref[...]) * v

    tr = lax.broadcasted_iota(jnp.int32, (C, C), 0)
    tc = lax.broadcasted_iota(jnp.int32, (C, C), 1)
    logp = _mm(_dot, (tc <= tr).astype(F32), logw, SPLIT_CUMSUM)
    logp_end = logp[C - 1:C, :]
    r_t = r * jnp.exp(logp)
    a_t = -kkn * jnp.exp(logp - logw)
    inv = jnp.exp(-logp)
    b_t = b * inv
    k_t = k2 * inv
    to_end = jnp.exp(logp_end - logp)
    b_e = b * to_end
    k_e = k2 * to_end
    decay_end = jnp.exp(logp_end)

    strict = same_head & (ci < ri)
    incl = same_head & (ci <= ri)
    eye = (ri == ci).astype(F32)

    def stack(t):
        return jnp.concatenate([jnp.where(first_head, t, 0.0), jnp.where(first_head, 0.0, t)], axis=0)

    ys = []
    for i in range(W // LANES):
        sl = slice(i * LANES, (i + 1) * LANES)
        a_s, r_s = stack(a_t[:, sl]), stack(r_t[:, sl])
        b_s, k_s, v_s = stack(b_t[:, sl]), stack(k_t[:, sl]), stack(v[:, sl])
        state = state_ref[i]

        gram = _mm(_dot_nt, jnp.concatenate([a_s, r_s], axis=0),
                   jnp.concatenate([b_s, k_s], axis=0), SPLIT_CHUNK)
        n = 2 * C
        l_ab = jnp.where(strict, gram[:n, :n], 0.0)
        a_ak = jnp.where(strict, gram[:n, n:], 0.0)
        a_rb = jnp.where(incl, gram[n:, :n], 0.0)
        a_rk = jnp.where(incl, gram[n:, n:], 0.0)

        t_inv = eye + l_ab
        l_pow = l_ab
        for _ in range(int(math.log2(C)) - 1):
            l_pow = _mm(_dot, l_pow, l_pow, SPLIT_CHUNK)
            t_inv = t_inv + _mm(_dot, l_pow, t_inv, SPLIT_CHUNK)

        from_state = _mm(_dot_nt, jnp.concatenate([a_s, r_s], axis=0), state, SPLIT_CHUNK)
        rhs = from_state[:n] + _mm(_dot, a_ak, v_s, SPLIT_CHUNK)
        u_s = _mm(_dot, t_inv, rhs, SPLIT_CHUNK)
        uv = jnp.concatenate([u_s, v_s], axis=0)
        y_s = from_state[n:] + _mm(_dot, jnp.concatenate([a_rb, a_rk], axis=1), uv, SPLIT_CHUNK)
        ys.append(y_s[:C] + y_s[C:])

        bk_e = jnp.concatenate([stack(b_e[:, sl]), stack(k_e[:, sl])], axis=0)
        state_ref[i] = state * decay_end[:, sl] + _mm(_dot_tn, uv, bk_e, SPLIT_CHUNK)

    y = jnp.concatenate(ys, axis=1)
    mean = head_sum(y) * (1.0 / HEAD_DIM)
    yc = y - mean
    var = head_sum(yc * yc) * (1.0 / HEAD_DIM)
    y = yc * lax.rsqrt(var + LNX_EPS) * lng_ref[...] + lnb_ref[...]
    o_ref[0] = (y + bonus) * g


def _rwkv(p, mu, w0, w_up, a0, a_up, g_up, k_k, k_a, r_k, ln_g, ln_b):
    B, S, P = p.shape
    W = RWKV_WIDTH
    C = RWKV_CHUNK
    wwa = jnp.zeros((LANES, 2 * W), F32)
    wwa = wwa.at[:DECAY_LORA, :W].set(w_up).at[DECAY_LORA:, W:].set(a_up)
    vec = lambda t: t.reshape(1, -1)
    row_spec = lambda n: pl.BlockSpec((1, n), lambda b, s: (0, 0))
    return pl.pallas_call(
        _rwkv_kernel,
        grid=(B, S // C),
        in_specs=[
            pl.BlockSpec((1, C, P), lambda b, s: (b, s, 0)),
            row_spec(P), row_spec(W), row_spec(W),
            pl.BlockSpec((LANES, 2 * W), lambda b, s: (0, 0)),
            pl.BlockSpec((GATE_LORA, W), lambda b, s: (0, 0)),
            row_spec(W), row_spec(W), row_spec(W), row_spec(W), row_spec(W),
        ],
        out_specs=pl.BlockSpec((1, C, W), lambda b, s: (b, s, 0)),
        out_shape=jax.ShapeDtypeStruct((B, S, W), F32),
        scratch_shapes=[pltpu.VMEM((1, P), F32),
                        pltpu.VMEM((W // LANES, LANES, LANES), F32)],
        compiler_params=_params("parallel", "arbitrary"),
        name="rwkv7",
    )(p, vec(mu), vec(w0), vec(a0), wwa, g_up, vec(k_k), vec(k_a), vec(r_k), vec(ln_g), vec(ln_b))


def _rel_bias_kernel(tab_ref, o_ref):
    d = pl.program_id(0)
    n_blk = MOBA_BLOCK
    qi = lax.broadcasted_iota(jnp.int32, (n_blk, n_blk), 0)
    kj = lax.broadcasted_iota(jnp.int32, (n_blk, n_blk), 1)
    n = jnp.maximum(qi - kj + d * n_blk, 0)
    max_exact = REL_BUCKETS // 2
    nf = jnp.maximum(n, 1).astype(F32)
    large = max_exact + (jnp.log(nf / max_exact) / math.log(REL_MAX_DISTANCE / max_exact)
                         * (REL_BUCKETS - max_exact)).astype(jnp.int32)
    large = jnp.minimum(large, REL_BUCKETS - 1)
    bucket = jnp.where(n < max_exact, n, large)
    for h in range(MOBA_HEADS):
        tile = jnp.zeros((n_blk, n_blk), F32)
        for c in range(REL_BUCKETS):
            tile = jnp.where(bucket == c, tab_ref[h, c], tile)
        o_ref[h, 0] = tile


def _rel_bias_tiles(rel_bias, n_blocks):
    return pl.pallas_call(
        _rel_bias_kernel,
        grid=(n_blocks,),
        in_specs=[pl.BlockSpec(memory_space=pltpu.SMEM)],
        out_specs=pl.BlockSpec((MOBA_HEADS, 1, MOBA_BLOCK, MOBA_BLOCK), lambda d: (0, d, 0, 0)),
        out_shape=jax.ShapeDtypeStruct((MOBA_HEADS, n_blocks, MOBA_BLOCK, MOBA_BLOCK), F32),
        compiler_params=_params("parallel"),
        name="rel_bias_tiles",
    )(rel_bias)


def _moba_kernel(q_ref, k_ref, v_ref, bias_ref, o_ref, kmean_ref, sel_ref, *, n_blocks):
    blk = MOBA_BLOCK
    qb = pl.program_id(2)

    @pl.when(qb == 0)
    def _():
        kmean_ref[...] = jnp.mean(k_ref[0].reshape(n_blocks, blk, LANES), axis=1)

    lane = lax.broadcasted_iota(jnp.int32, (1, LANES), 1)
    q = q_ref[0] * (HEAD_DIM ** -0.5)
    ri = lax.broadcasted_iota(jnp.int32, (blk, blk), 0)
    ci = lax.broadcasted_iota(jnp.int32, (blk, blk), 1)
    causal = ci <= ri
    own_start = pl.multiple_of(qb * blk, blk)

    outs = []
    for e in range(HEADS_PER_TILE):
        head_lanes = (lane >> HEAD_SHIFT) == e
        qe = jnp.where(head_lanes, q, 0.0)
        gate = _dot_nt(qe, kmean_ref[...], HIGHEST)
        cols = [gate[:, j:j + 1] for j in range(n_blocks)]
        for j in range(n_blocks - 1):
            rank = jnp.zeros((blk, 1), jnp.int32)
            for jj in range(n_blocks - 1):
                if jj == j:
                    continue
                ahead = (cols[jj] >= cols[j]) if jj < j else (cols[jj] > cols[j])
                rank = rank + jnp.where(ahead, 1, 0) * (jj < qb).astype(jnp.int32)
            sel_ref[e, j] = jnp.where(rank < MOBA_TOPK, 1.0, 0.0)

        qe16 = qe.astype(BF16)

        def scores(start, bias):
            kb = k_ref[0, pl.ds(start, blk), :].astype(BF16)
            return _dot_nt(qe16, kb) + bias

        s = jnp.where(causal, scores(own_start, bias_ref[e, 0]), -jnp.inf)
        m = jnp.max(s, axis=-1, keepdims=True)
        pr = jnp.exp(s - m)
        l = jnp.sum(pr, axis=-1, keepdims=True)
        acc = _dot(pr.astype(BF16), v_ref[0, pl.ds(own_start, blk), :].astype(BF16))

        def past_block(j, carry):
            m, l, acc = carry
            start = pl.multiple_of(j * blk, blk)
            s = jnp.where(sel_ref[e, j] > 0.5, scores(start, bias_ref[e, qb - j]), -jnp.inf)
            m_new = jnp.maximum(m, jnp.max(s, axis=-1, keepdims=True))
            alpha = jnp.exp(m - m_new)
            pr = jnp.exp(s - m_new)
            l = alpha * l + jnp.sum(pr, axis=-1, keepdims=True)
            acc = alpha * acc + _dot(pr.astype(BF16), v_ref[0, pl.ds(start, blk), :].astype(BF16))
            return m_new, l, acc

        m, l, acc = lax.fori_loop(0, qb, past_block, (m, l, acc))
        outs.append(acc / l)

    o_ref[0] = jnp.where(lane < HEAD_DIM, outs[0], outs[1])


def _moba(qkv, bias_tiles):
    B, S, _ = qkv.shape
    blk = MOBA_BLOCK
    n_blocks = S // blk
    n_pairs = MOBA_WIDTH // LANES
    return pl.pallas_call(
        functools.partial(_moba_kernel, n_blocks=n_blocks),
        grid=(n_pairs, B, n_blocks),
        in_specs=[
            pl.BlockSpec((1, blk, LANES), lambda p, b, i: (b, i, p)),
            pl.BlockSpec((1, S, LANES), lambda p, b, i: (b, 0, n_pairs + p)),
            pl.BlockSpec((1, S, LANES), lambda p, b, i: (b, 0, 2 * n_pairs + p)),
            pl.BlockSpec((HEADS_PER_TILE, n_blocks, blk, blk), lambda p, b, i: (p, 0, 0, 0)),
        ],
        out_specs=pl.BlockSpec((1, blk, LANES), lambda p, b, i: (b, i, p)),
        out_shape=jax.ShapeDtypeStruct((B, S, MOBA_WIDTH), F32),
        scratch_shapes=[pltpu.VMEM((n_blocks, LANES), F32),
                        pltpu.VMEM((HEADS_PER_TILE, n_blocks, blk, 1), F32)],
        compiler_params=_params("parallel", "parallel", "arbitrary"),
        name="moba",
    )(qkv, qkv, qkv, bias_tiles)


def _mix_out_kernel(x_ref, a_ref, b_ref, wa_ref, wb_ref, o_ref):
    o_ref[...] = (x_ref[...] + _dot(a_ref[...].astype(BF16), wa_ref[...])
                  + _dot(b_ref[...].astype(BF16), wb_ref[...]))


def _mix_out(x, ya, yb, w, *, tm=512):
    T, D = x.shape
    na, nb = ya.shape[1], yb.shape[1]
    tm = min(tm, T)
    return pl.pallas_call(
        _mix_out_kernel,
        grid=(T // tm,),
        in_specs=[
            pl.BlockSpec((tm, D), lambda i: (i, 0)),
            pl.BlockSpec((tm, na), lambda i: (i, 0)),
            pl.BlockSpec((tm, nb), lambda i: (i, 0)),
            pl.BlockSpec((na, D), lambda i: (0, 0)),
            pl.BlockSpec((nb, D), lambda i: (0, 0)),
        ],
        out_specs=pl.BlockSpec((tm, D), lambda i: (i, 0)),
        out_shape=jax.ShapeDtypeStruct((T, D), F32),
        compiler_params=_params("parallel"),
        name="mix_out",
    )(x, ya, yb, w[:na], w[na:])


def _xattn_kernel(x_ref, g_ref, wq_ref, kv_ref, wo_ref, o_ref):
    x = x_ref[0]
    D = x.shape[-1]
    dh = D // XATTN_HEADS
    h = _rms(x, g_ref[...]).astype(BF16)
    q = _dot(h, wq_ref[...])
    heads = []
    for i in range(XATTN_HEADS):
        qh = q[:, i * dh:(i + 1) * dh].astype(BF16)
        kh = kv_ref[0, :, i * dh:(i + 1) * dh].astype(BF16)
        vh = kv_ref[0, :, D + i * dh:D + (i + 1) * dh].astype(BF16)
        s = _dot_nt(qh, kh) * (dh ** -0.5)
        s = s - jnp.max(s, axis=-1, keepdims=True)
        pr = jnp.exp(s)
        pr = pr / jnp.sum(pr, axis=-1, keepdims=True)
        heads.append(_dot(pr.astype(BF16), vh))
    o = jnp.concatenate(heads, axis=1).astype(BF16)
    o_ref[0] = x + _dot(o, wo_ref[...])


def _xattn(x, g, w_q, kv, w_o, *, tm=512):
    B, S, D = x.shape
    M = kv.shape[1]
    tm = min(tm, S)
    return pl.pallas_call(
        _xattn_kernel,
        grid=(B, S // tm),
        in_specs=[
            pl.BlockSpec((1, tm, D), lambda b, i: (b, i, 0)),
            pl.BlockSpec((1, D), lambda b, i: (0, 0)),
            pl.BlockSpec((D, D), lambda b, i: (0, 0)),
            pl.BlockSpec((1, M, 2 * D), lambda b, i: (b, 0, 0)),
            pl.BlockSpec((D, D), lambda b, i: (0, 0)),
        ],
        out_specs=pl.BlockSpec((1, tm, D), lambda b, i: (b, i, 0)),
        out_shape=jax.ShapeDtypeStruct((B, S, D), F32),
        compiler_params=_params("parallel", "parallel"),
        name="xattn",
    )(x, g.reshape(1, D), w_q, kv, w_o)


def kernel(x, mem, rel_bias, final_norm_g, ffn1_norm_g, ffn1_w_in, ffn1_w_out, mix_norm_g, w_mix_in, w_mix_out, rwkv_mu, rwkv_w0, rwkv_w_up, rwkv_a0, rwkv_a_up, rwkv_g_up, rwkv_k_k, rwkv_k_a, rwkv_r_k, rwkv_ln_g, rwkv_ln_b, xattn_norm_g, mem_norm_g, xattn_w_q, xattn_w_kv, xattn_w_o, ffn2_norm_g, ffn2_w_in, ffn2_w_out):
    B, S, D = x.shape
    M = mem.shape[1]
    depth = ffn1_w_in.shape[0]
    T = B * S
    bf = lambda w: w.astype(BF16)

    bias_tiles = _rel_bias_tiles(rel_bias, S // MOBA_BLOCK)
    mem2 = mem.reshape(B * M, D)
    x = x.reshape(T, D)
    for l in range(depth):
        x = _ffn(x, ffn1_norm_g[l], bf(ffn1_w_in[l]), bf(ffn1_w_out[l]))
        p_rwkv, qkv = _norm_proj(x, mix_norm_g[l], bf(w_mix_in[l]), (RWKV_PROJ, MOBA_PROJ))
        y_rwkv = _rwkv(p_rwkv.reshape(B, S, RWKV_PROJ), rwkv_mu[l], rwkv_w0[l], rwkv_w_up[l],
                       rwkv_a0[l], rwkv_a_up[l], rwkv_g_up[l], rwkv_k_k[l], rwkv_k_a[l],
                       rwkv_r_k[l], rwkv_ln_g[l], rwkv_ln_b[l])
        y_moba = _moba(qkv.reshape(B, S, MOBA_PROJ), bias_tiles)
        x = _mix_out(x, y_rwkv.reshape(T, RWKV_WIDTH), y_moba.reshape(T, MOBA_WIDTH), bf(w_mix_out[l]))
        (kv,) = _norm_proj(mem2, mem_norm_g[l], bf(xattn_w_kv[l]), (2 * D,))
        x = _xattn(x.reshape(B, S, D), xattn_norm_g[l], bf(xattn_w_q[l]), kv.reshape(B, M, 2 * D),
                   bf(xattn_w_o[l])).reshape(T, D)
        x = _ffn(x, ffn2_norm_g[l], bf(ffn2_w_in[l]), bf(ffn2_w_out[l]),
                 final_norm_g if l == depth - 1 else None)
    return x.reshape(B, S, D)
```

```python
import functools
import math

import jax
import jax.numpy as jnp
from jax import lax
from jax.experimental import pallas as pl
from jax.experimental.pallas import tpu as pltpu

F32 = jnp.float32
BF16 = jnp.bfloat16
HIGHEST = lax.Precision.HIGHEST

HEAD_DIM = 64
RWKV_WIDTH = 512
MOBA_WIDTH = 512
MOBA_HEADS = MOBA_WIDTH // HEAD_DIM
DECAY_LORA = 64
ICLR_LORA = 64
GATE_LORA = 128
RWKV_PROJ = 3 * RWKV_WIDTH + DECAY_LORA + ICLR_LORA + GATE_LORA
MOBA_PROJ = 3 * MOBA_WIDTH
LNX_EPS = 64e-5
MOBA_BLOCK = 256
MOBA_TOPK = 3
REL_BUCKETS = 32
REL_MAX_DISTANCE = 1024
XATTN_HEADS = 4
FFN_RES_WEIGHT = 0.5
NORM_EPS = 1e-6

LANES = 128
HEADS_PER_TILE = LANES // HEAD_DIM
HEAD_SHIFT = HEAD_DIM.bit_length() - 1
RWKV_CHUNK = 64
SPLIT_LORA = (1, 1, 1)
SPLIT_SUM = (2, 1, 2)
SPLIT_CUMSUM = (1, 2, 2)
SPLIT_CHUNK = (1, 1, 1)
VMEM_LIMIT = 56 * 1024 * 1024


def _rms(x, g):
    ms = jnp.mean(x * x, axis=-1, keepdims=True)
    return x * lax.rsqrt(ms + NORM_EPS) * g


def _dot(a, b, precision=None):
    return jnp.dot(a, b, precision=precision, preferred_element_type=F32)


def _dot_nt(a, b, precision=None):
    return lax.dot_general(a, b, (((1,), (1,)), ((), ())), precision=precision,
                           preferred_element_type=F32)


def _dot_tn(a, b, precision=None):
    return lax.dot_general(a, b, (((0,), (0,)), ((), ())), precision=precision,
                           preferred_element_type=F32)


def _bf16_terms(x, n):
    terms = []
    for i in range(n):
        t = x.astype(BF16)
        terms.append(t)
        if i + 1 < n:
            x = x - t.astype(F32)
    return terms


def _mm(dot, a, b, split):
    na, nb, order = split
    at, bt = _bf16_terms(a, na), _bf16_terms(b, nb)
    acc = None
    for i in range(na):
        for j in range(nb):
            if i + j < order:
                t = dot(at[i], bt[j])
                acc = t if acc is None else acc + t
    return acc


def _params(*semantics):
    return pltpu.CompilerParams(dimension_semantics=semantics, vmem_limit_bytes=VMEM_LIMIT)


def _ffn_kernel(x_ref, g_ref, wg_ref, wu_ref, wo_ref, *rest, final_norm):
    if final_norm:
        fg_ref, o_ref, h_ref, acc_ref = rest
    else:
        o_ref, h_ref, acc_ref = rest
    j = pl.program_id(1)

    @pl.when(j == 0)
    def _():
        h_ref[...] = _rms(x_ref[...], g_ref[...]).astype(BF16)
        acc_ref[...] = jnp.zeros_like(acc_ref)

    h = h_ref[...]
    gate = _dot(h, wg_ref[...])
    up = _dot(h, wu_ref[...])
    act = gate * jax.nn.sigmoid(gate) * up
    acc_ref[...] += _dot(act.astype(BF16), wo_ref[...])

    @pl.when(j == pl.num_programs(1) - 1)
    def _():
        y = x_ref[...] + FFN_RES_WEIGHT * acc_ref[...]
        if final_norm:
            y = _rms(y, fg_ref[...])
        o_ref[...] = y


def _ffn(x, g, w_in, w_out, final_g=None, *, tm=512, tf=1408):
    T, D = x.shape
    F = w_out.shape[0]
    tm = min(tm, T)
    nf = F // tf
    in_specs = [
        pl.BlockSpec((tm, D), lambda i, j: (i, 0)),
        pl.BlockSpec((1, D), lambda i, j: (0, 0)),
        pl.BlockSpec((D, tf), lambda i, j: (0, j)),
        pl.BlockSpec((D, tf), lambda i, j: (0, j + nf)),
        pl.BlockSpec((tf, D), lambda i, j: (j, 0)),
    ]
    args = [x, g.reshape(1, D), w_in, w_in, w_out]
    if final_g is not None:
        in_specs.append(pl.BlockSpec((1, D), lambda i, j: (0, 0)))
        args.append(final_g.reshape(1, D))
    return pl.pallas_call(
        functools.partial(_ffn_kernel, final_norm=final_g is not None),
        grid=(T // tm, nf),
        in_specs=in_specs,
        out_specs=pl.BlockSpec((tm, D), lambda i, j: (i, 0)),
        out_shape=jax.ShapeDtypeStruct((T, D), F32),
        scratch_shapes=[pltpu.VMEM((tm, D), BF16), pltpu.VMEM((tm, D), F32)],
        compiler_params=_params("parallel", "arbitrary"),
        name="ffn",
    )(*args)


def _norm_proj_kernel(x_ref, g_ref, w_ref, *o_refs, splits):
    h = _rms(x_ref[...], g_ref[...]).astype(BF16)
    off = 0
    for o_ref, n in zip(o_refs, splits):
        o_ref[...] = _dot(h, w_ref[:, off:off + n])
        off += n


def _norm_proj(x, g, w, splits, *, tm=512):
    T, D = x.shape
    N = w.shape[1]
    tm = min(tm, T)
    assert sum(splits) == N
    return pl.pallas_call(
        functools.partial(_norm_proj_kernel, splits=splits),
        grid=(T // tm,),
        in_specs=[
            pl.BlockSpec((tm, D), lambda i: (i, 0)),
            pl.BlockSpec((1, D), lambda i: (0, 0)),
            pl.BlockSpec((D, N), lambda i: (0, 0)),
        ],
        out_specs=[pl.BlockSpec((tm, n), lambda i: (i, 0)) for n in splits],
        out_shape=[jax.ShapeDtypeStruct((T, n), F32) for n in splits],
        compiler_params=_params("parallel"),
        name="norm_proj",
    )(x, g.reshape(1, D), w)


def _rwkv_kernel(p_ref, mu_ref, w0_ref, a0_ref, wwa_ref, gup_ref, kk_ref, ka_ref, rk_ref,
                 lng_ref, lnb_ref, o_ref, prev_ref, state_ref):
    C = RWKV_CHUNK
    W = RWKV_WIDTH
    s_idx = pl.program_id(1)

    @pl.when(s_idx == 0)
    def _():
        prev_ref[...] = jnp.zeros_like(prev_ref)
        state_ref[...] = jnp.zeros_like(state_ref)

    p = p_ref[0]
    row = lax.broadcasted_iota(jnp.int32, (C, 1), 0)
    shifted = jnp.where(row == 0, prev_ref[...], pltpu.roll(p, 1, axis=0))
    prev_ref[...] = p[C - 1:C, :]
    p = p + (shifted - p) * mu_ref[...]

    r = p[:, 0:W]
    k = p[:, W:2 * W]
    v = p[:, 2 * W:3 * W]
    lora = p[:, 3 * W:3 * W + LANES]
    g_lo = p[:, 3 * W + LANES:3 * W + 2 * LANES]

    lane = lax.broadcasted_iota(jnp.int32, (1, LANES), 1)
    first_head = lane < HEAD_DIM
    z = jnp.where(first_head, jnp.tanh(lora), lora)
    wa = _mm(_dot, z, wwa_ref[...], SPLIT_LORA)
    logw = -math.exp(-0.5) * jax.nn.sigmoid(w0_ref[...] + wa[:, :W])
    a = jax.nn.sigmoid(a0_ref[...] + wa[:, W:])
    g = _mm(_dot, jax.nn.sigmoid(g_lo), gup_ref[...], SPLIT_LORA)

    kk = k * kk_ref[...]
    k2 = k * (1.0 + (a - 1.0) * ka_ref[...])

    ri = lax.broadcasted_iota(jnp.int32, (LANES, LANES), 0)
    ci = lax.broadcasted_iota(jnp.int32, (LANES, LANES), 1)
    same_head = (ri >> HEAD_SHIFT) == (ci >> HEAD_SHIFT)
    head_ones = same_head.astype(F32)

    def head_sum(t):
        return jnp.concatenate(
            [_mm(_dot, t[:, i * LANES:(i + 1) * LANES], head_ones, SPLIT_SUM)
             for i in range(W // LANES)],
            axis=1)

    kkn = kk / jnp.maximum(jnp.sqrt(head_sum(kk * kk)), 1e-12)
    b = kkn * a
    bonus = head_sum(r * k2 * rk_ref[...]) * v

    tr = lax.broadcasted_iota(jnp.int32, (C, C), 0)
    tc = lax.broadcasted_iota(jnp.int32, (C, C), 1)
    logp = _mm(_dot, (tc <= tr).astype(F32), logw, SPLIT_CUMSUM)
    logp_end = logp[C - 1:C, :]
    r_t = r * jnp.exp(logp)
    a_t = -kkn * jnp.exp(logp - logw)
    inv = jnp.exp(-logp)
    b_t = b * inv
    k_t = k2 * inv
    to_end = jnp.exp(logp_end - logp)
    b_e = b * to_end
    k_e = k2 * to_end
    decay_end = jnp.exp(logp_end)

    strict = same_head & (ci < ri)
    incl = same_head & (ci <= ri)
    eye = (ri == ci).astype(F32)

    def stack(t):
        return jnp.concatenate([jnp.where(first_head, t, 0.0), jnp.where(first_head, 0.0, t)], axis=0)

    ys = []
    for i in range(W // LANES):
        sl = slice(i * LANES, (i + 1) * LANES)
        a_s, r_s = stack(a_t[:, sl]), stack(r_t[:, sl])
        b_s, k_s, v_s = stack(b_t[:, sl]), stack(k_t[:, sl]), stack(v[:, sl])
        state = state_ref[i]

        gram = _mm(_dot_nt, jnp.concatenate([a_s, r_s], axis=0),
                   jnp.concatenate([b_s, k_s], axis=0), SPLIT_CHUNK)
        n = 2 * C
        l_ab = jnp.where(strict, gram[:n, :n], 0.0)
        a_ak = jnp.where(strict, gram[:n, n:], 0.0)
        a_rb = jnp.where(incl, gram[n:, :n], 0.0)
        a_rk = jnp.where(incl, gram[n:, n:], 0.0)

        t_inv = eye + l_ab
        l_pow = l_ab
        for _ in range(int(math.log2(C)) - 1):
            l_pow = _mm(_dot, l_pow, l_pow, SPLIT_CHUNK)
            t_inv = t_inv + _mm(_dot, l_pow, t_inv, SPLIT_CHUNK)

        from_state = _mm(_dot_nt, jnp.concatenate([a_s, r_s], axis=0), state, SPLIT_CHUNK)
        rhs = from_state[:n] + _mm(_dot, a_ak, v_s, SPLIT_CHUNK)
        u_s = _mm(_dot, t_inv, rhs, SPLIT_CHUNK)
        uv = jnp.concatenate([u_s, v_s], axis=0)
        y_s = from_state[n:] + _mm(_dot, jnp.concatenate([a_rb, a_rk], axis=1), uv, SPLIT_CHUNK)
        ys.append(y_s[:C] + y_s[C:])

        bk_e = jnp.concatenate([stack(b_e[:, sl]), stack(k_e[:, sl])], axis=0)
        state_ref[i] = state * decay_end[:, sl] + _mm(_dot_tn, uv, bk_e, SPLIT_CHUNK)

    y = jnp.concatenate(ys, axis=1)
    mean = head_sum(y) * (1.0 / HEAD_DIM)
    yc = y - mean
    var = head_sum(yc * yc) * (1.0 / HEAD_DIM)
    y = yc * lax.rsqrt(var + LNX_EPS) * lng_ref[...] + lnb_ref[...]
    o_ref[0] = (y + bonus) * g


def _rwkv(p, mu, w0, w_up, a0, a_up, g_up, k_k, k_a, r_k, ln_g, ln_b):
    B, S, P = p.shape
    W = RWKV_WIDTH
    C = RWKV_CHUNK
    wwa = jnp.zeros((LANES, 2 * W), F32)
    wwa = wwa.at[:DECAY_LORA, :W].set(w_up).at[DECAY_LORA:, W:].set(a_up)
    vec = lambda t: t.reshape(1, -1)
    row_spec = lambda n: pl.BlockSpec((1, n), lambda b, s: (0, 0))
    return pl.pallas_call(
        _rwkv_kernel,
        grid=(B, S // C),
        in_specs=[
            pl.BlockSpec((1, C, P), lambda b, s: (b, s, 0)),
            row_spec(P), row_spec(W), row_spec(W),
            pl.BlockSpec((LANES, 2 * W), lambda b, s: (0, 0)),
            pl.BlockSpec((GATE_LORA, W), lambda b, s: (0, 0)),
            row_spec(W), row_spec(W), row_spec(W), row_spec(W), row_spec(W),
        ],
        out_specs=pl.BlockSpec((1, C, W), lambda b, s: (b, s, 0)),
        out_shape=jax.ShapeDtypeStruct((B, S, W), F32),
        scratch_shapes=[pltpu.VMEM((1, P), F32),
                        pltpu.VMEM((W // LANES, LANES, LANES), F32)],
        compiler_params=_params("parallel", "arbitrary"),
        name="rwkv7",
    )(p, vec(mu), vec(w0), vec(a0), wwa, g_up, vec(k_k), vec(k_a), vec(r_k), vec(ln_g), vec(ln_b))


def _rel_bias_kernel(tab_ref, o_ref, *, n_blocks):
    blk = MOBA_BLOCK
    c0 = pl.program_id(0) * blk
    kc = lax.broadcasted_iota(jnp.int32, (blk, blk), 0) + c0
    qi = lax.broadcasted_iota(jnp.int32, (blk, blk), 1)
    dist = qi + (n_blocks - 1) * blk - kc
    n = jnp.maximum(dist, 0)
    max_exact = REL_BUCKETS // 2
    nf = jnp.maximum(n, 1).astype(F32)
    large = max_exact + (jnp.log(nf / max_exact) / math.log(REL_MAX_DISTANCE / max_exact)
                         * (REL_BUCKETS - max_exact)).astype(jnp.int32)
    large = jnp.minimum(large, REL_BUCKETS - 1)
    bucket = jnp.where(n < max_exact, n, large)
    for h in range(MOBA_HEADS):
        tile = jnp.zeros((blk, blk), F32)
        for c in range(REL_BUCKETS):
            tile = jnp.where(bucket == c, tab_ref[h, c], tile)
        o_ref[h] = jnp.where(dist < 0, -jnp.inf, tile)


def _rel_bias_strip(rel_bias, n_blocks):
    blk = MOBA_BLOCK
    return pl.pallas_call(
        functools.partial(_rel_bias_kernel, n_blocks=n_blocks),
        grid=(n_blocks,),
        in_specs=[pl.BlockSpec(memory_space=pltpu.SMEM)],
        out_specs=pl.BlockSpec((MOBA_HEADS, blk, blk), lambda c: (0, c, 0)),
        out_shape=jax.ShapeDtypeStruct((MOBA_HEADS, n_blocks * blk, blk), F32),
        compiler_params=_params("parallel"),
        name="rel_bias_strip",
    )(rel_bias)


def _moba_kernel(q_ref, k_ref, v_ref, bias_ref, o_ref, *, n_blocks):
    blk = MOBA_BLOCK
    lane = lax.broadcasted_iota(jnp.int32, (1, LANES), 1)
    vrow = lax.broadcasted_iota(jnp.int32, (LANES, 1), 0)
    q = q_ref[0] * (HEAD_DIM ** -0.5)
    k = k_ref[0]
    k16 = k.astype(BF16)
    vt16 = v_ref[0].T.astype(BF16)
    kmean = jnp.mean(k.reshape(n_blocks, blk, LANES), axis=1)

    gates, q16 = [], []
    for e in range(HEADS_PER_TILE):
        qe = jnp.where((lane >> HEAD_SHIFT) == e, q, 0.0)
        gates.append(_dot_nt(kmean, qe, HIGHEST))
        q16.append(qe.astype(BF16))

    for qb in range(n_blocks):
        rows = slice(qb * blk, (qb + 1) * blk)
        n_keys = (qb + 1) * blk
        bias_lo = (n_blocks - 1 - qb) * blk
        out_t = []
        for e in range(HEADS_PER_TILE):
            s_t = _dot_nt(k16[:n_keys], q16[e][rows])
            g = [gates[e][j:j + 1, rows] for j in range(qb)]
            tiles = []
            for j in range(qb + 1):
                t = s_t[j * blk:(j + 1) * blk] + bias_ref[e, bias_lo + j * blk:bias_lo + (j + 1) * blk, :]
                if j < qb:
                    rank = jnp.zeros((1, blk), jnp.int32)
                    for jj in range(qb):
                        if jj != j:
                            ahead = (g[jj] >= g[j]) if jj < j else (g[jj] > g[j])
                            rank = rank + jnp.where(ahead, 1, 0)
                    t = jnp.where(rank < MOBA_TOPK, t, -jnp.inf)
                tiles.append(t)
            m = tiles[0].max(axis=0, keepdims=True)
            for t in tiles[1:]:
                m = jnp.maximum(m, t.max(axis=0, keepdims=True))
            probs = [jnp.exp(t - m) for t in tiles]
            l = probs[0].sum(axis=0, keepdims=True)
            for pr in probs[1:]:
                l = l + pr.sum(axis=0, keepdims=True)
            p16 = jnp.concatenate([pr.astype(BF16) for pr in probs], axis=0)
            out_t.append(_dot(vt16[:, :n_keys], p16) / l)
        o_ref[0, rows, :] = jnp.where(vrow < HEAD_DIM, out_t[0], out_t[1]).T


def _moba(qkv, bias_strip):
    B, S, _ = qkv.shape
    blk = MOBA_BLOCK
    n_blocks = S // blk
    n_pairs = MOBA_WIDTH // LANES
    return pl.pallas_call(
        functools.partial(_moba_kernel, n_blocks=n_blocks),
        grid=(n_pairs, B),
        in_specs=[
            pl.BlockSpec((1, S, LANES), lambda p, b: (b, 0, p)),
            pl.BlockSpec((1, S, LANES), lambda p, b: (b, 0, n_pairs + p)),
            pl.BlockSpec((1, S, LANES), lambda p, b: (b, 0, 2 * n_pairs + p)),
            pl.BlockSpec((HEADS_PER_TILE, S, blk), lambda p, b: (p, 0, 0)),
        ],
        out_specs=pl.BlockSpec((1, S, LANES), lambda p, b: (b, 0, p)),
        out_shape=jax.ShapeDtypeStruct((B, S, MOBA_WIDTH), F32),
        compiler_params=_params("parallel", "parallel"),
        name="moba",
    )(qkv, qkv, qkv, bias_strip)


def _mix_out_kernel(x_ref, a_ref, b_ref, wa_ref, wb_ref, o_ref):
    o_ref[...] = (x_ref[...] + _dot(a_ref[...].astype(BF16), wa_ref[...])
                  + _dot(b_ref[...].astype(BF16), wb_ref[...]))


def _mix_out(x, ya, yb, w, *, tm=512):
    T, D = x.shape
    na, nb = ya.shape[1], yb.shape[1]
    tm = min(tm, T)
    return pl.pallas_call(
        _mix_out_kernel,
        grid=(T // tm,),
        in_specs=[
            pl.BlockSpec((tm, D), lambda i: (i, 0)),
            pl.BlockSpec((tm, na), lambda i: (i, 0)),
            pl.BlockSpec((tm, nb), lambda i: (i, 0)),
            pl.BlockSpec((na, D), lambda i: (0, 0)),
            pl.BlockSpec((nb, D), lambda i: (0, 0)),
        ],
        out_specs=pl.BlockSpec((tm, D), lambda i: (i, 0)),
        out_shape=jax.ShapeDtypeStruct((T, D), F32),
        compiler_params=_params("parallel"),
        name="mix_out",
    )(x, ya, yb, w[:na], w[na:])


def _xattn_kernel(x_ref, g_ref, wq_ref, kv_ref, wo_ref, o_ref):
    x = x_ref[0]
    D = x.shape[-1]
    dh = D // XATTN_HEADS
    h = _rms(x, g_ref[...]).astype(BF16)
    q = _dot(h, wq_ref[...])
    heads = []
    for i in range(XATTN_HEADS):
        qh = q[:, i * dh:(i + 1) * dh].astype(BF16)
        kh = kv_ref[0, :, i * dh:(i + 1) * dh].astype(BF16)
        vh = kv_ref[0, :, D + i * dh:D + (i + 1) * dh].astype(BF16)
        s = _dot_nt(qh, kh) * (dh ** -0.5)
        s = s - jnp.max(s, axis=-1, keepdims=True)
        pr = jnp.exp(s)
        pr = pr / jnp.sum(pr, axis=-1, keepdims=True)
        heads.append(_dot(pr.astype(BF16), vh))
    o = jnp.concatenate(heads, axis=1).astype(BF16)
    o_ref[0] = x + _dot(o, wo_ref[...])


def _xattn(x, g, w_q, kv, w_o, *, tm=512):
    B, S, D = x.shape
    M = kv.shape[1]
    tm = min(tm, S)
    return pl.pallas_call(
        _xattn_kernel,
        grid=(B, S // tm),
        in_specs=[
            pl.BlockSpec((1, tm, D), lambda b, i: (b, i, 0)),
            pl.BlockSpec((1, D), lambda b, i: (0, 0)),
            pl.BlockSpec((D, D), lambda b, i: (0, 0)),
            pl.BlockSpec((1, M, 2 * D), lambda b, i: (b, 0, 0)),
            pl.BlockSpec((D, D), lambda b, i: (0, 0)),
        ],
        out_specs=pl.BlockSpec((1, tm, D), lambda b, i: (b, i, 0)),
        out_shape=jax.ShapeDtypeStruct((B, S, D), F32),
        compiler_params=_params("parallel", "parallel"),
        name="xattn",
    )(x, g.reshape(1, D), w_q, kv, w_o)


def kernel(x, mem, rel_bias, final_norm_g, ffn1_norm_g, ffn1_w_in, ffn1_w_out, mix_norm_g, w_mix_in, w_mix_out, rwkv_mu, rwkv_w0, rwkv_w_up, rwkv_a0, rwkv_a_up, rwkv_g_up, rwkv_k_k, rwkv_k_a, rwkv_r_k, rwkv_ln_g, rwkv_ln_b, xattn_norm_g, mem_norm_g, xattn_w_q, xattn_w_kv, xattn_w_o, ffn2_norm_g, ffn2_w_in, ffn2_w_out):
    B, S, D = x.shape
    M = mem.shape[1]
    depth = ffn1_w_in.shape[0]
    T = B * S
    bf = lambda w: w.astype(BF16)

    bias_strip = _rel_bias_strip(rel_bias, S // MOBA_BLOCK)
    mem2 = mem.reshape(B * M, D)
    x = x.reshape(T, D)
    for l in range(depth):
        x = _ffn(x, ffn1_norm_g[l], bf(ffn1_w_in[l]), bf(ffn1_w_out[l]))
        p_rwkv, qkv = _norm_proj(x, mix_norm_g[l], bf(w_mix_in[l]), (RWKV_PROJ, MOBA_PROJ))
        y_rwkv = _rwkv(p_rwkv.reshape(B, S, RWKV_PROJ), rwkv_mu[l], rwkv_w0[l], rwkv_w_up[l],
                       rwkv_a0[l], rwkv_a_up[l], rwkv_g_up[l], rwkv_k_k[l], rwkv_k_a[l],
                       rwkv_r_k[l], rwkv_ln_g[l], rwkv_ln_b[l])
        y_moba = _moba(qkv.reshape(B, S, MOBA_PROJ), bias_strip)
        x = _mix_out(x, y_rwkv.reshape(T, RWKV_WIDTH), y_moba.reshape(T, MOBA_WIDTH), bf(w_mix_out[l]))
        (kv,) = _norm_proj(mem2, mem_norm_g[l], bf(xattn_w_kv[l]), (2 * D,))
        x = _xattn(x.reshape(B, S, D), xattn_norm_g[l], bf(xattn_w_q[l]), kv.reshape(B, M, 2 * D),
                   bf(xattn_w_o[l])).reshape(T, D)
        x = _ffn(x, ffn2_norm_g[l], bf(ffn2_w_in[l]), bf(ffn2_w_out[l]),
                 final_norm_g if l == depth - 1 else None)
    return x.reshape(B, S, D)
```

```python
import functools
import math

import jax
import jax.numpy as jnp
from jax import lax
from jax.experimental import pallas as pl
from jax.experimental.pallas import tpu as pltpu

F32 = jnp.float32
BF16 = jnp.bfloat16
HIGHEST = lax.Precision.HIGHEST

HEAD_DIM = 64
RWKV_WIDTH = 512
MOBA_WIDTH = 512
MOBA_HEADS = MOBA_WIDTH // HEAD_DIM
DECAY_LORA = 64
ICLR_LORA = 64
GATE_LORA = 128
RWKV_PROJ = 3 * RWKV_WIDTH + DECAY_LORA + ICLR_LORA + GATE_LORA
MOBA_PROJ = 3 * MOBA_WIDTH
LNX_EPS = 64e-5
MOBA_BLOCK = 256
MOBA_TOPK = 3
REL_BUCKETS = 32
REL_MAX_DISTANCE = 1024
XATTN_HEADS = 4
FFN_RES_WEIGHT = 0.5
NORM_EPS = 1e-6

LANES = 128
HEADS_PER_TILE = LANES // HEAD_DIM
HEAD_SHIFT = HEAD_DIM.bit_length() - 1
RWKV_CHUNK = 64
CHUNK_SHIFT = RWKV_CHUNK.bit_length() - 1
SPLIT_LORA = (1, 1, 1)
SPLIT_SUM = (2, 1, 2)
SPLIT_CUMSUM = (1, 2, 2)
SPLIT_CHUNK = (1, 1, 1)
VMEM_LIMIT = 56 * 1024 * 1024


def _rms(x, g):
    ms = jnp.mean(x * x, axis=-1, keepdims=True)
    return x * lax.rsqrt(ms + NORM_EPS) * g


def _dot(a, b, precision=None):
    return jnp.dot(a, b, precision=precision, preferred_element_type=F32)


def _dot_nt(a, b, precision=None):
    return lax.dot_general(a, b, (((1,), (1,)), ((), ())), precision=precision,
                           preferred_element_type=F32)


def _dot_tn(a, b, precision=None):
    return lax.dot_general(a, b, (((0,), (0,)), ((), ())), precision=precision,
                           preferred_element_type=F32)


def _bf16_terms(x, n):
    terms = []
    for i in range(n):
        t = x.astype(BF16)
        terms.append(t)
        if i + 1 < n:
            x = x - t.astype(F32)
    return terms


def _mm(dot, a, b, split):
    na, nb, order = split
    at, bt = _bf16_terms(a, na), _bf16_terms(b, nb)
    acc = None
    for i in range(na):
        for j in range(nb):
            if i + j < order:
                t = dot(at[i], bt[j])
                acc = t if acc is None else acc + t
    return acc


def _params(*semantics):
    return pltpu.CompilerParams(dimension_semantics=semantics, vmem_limit_bytes=VMEM_LIMIT)


def _ffn_kernel(x_ref, g_ref, wg_ref, wu_ref, wo_ref, *rest, final_norm):
    if final_norm:
        fg_ref, o_ref, h_ref, acc_ref = rest
    else:
        o_ref, h_ref, acc_ref = rest
    j = pl.program_id(1)

    @pl.when(j == 0)
    def _():
        h_ref[...] = _rms(x_ref[...], g_ref[...]).astype(BF16)
        acc_ref[...] = jnp.zeros_like(acc_ref)

    h = h_ref[...]
    gate = _dot(h, wg_ref[...])
    up = _dot(h, wu_ref[...])
    act = gate * jax.nn.sigmoid(gate) * up
    acc_ref[...] += _dot(act.astype(BF16), wo_ref[...])

    @pl.when(j == pl.num_programs(1) - 1)
    def _():
        y = x_ref[...] + FFN_RES_WEIGHT * acc_ref[...]
        if final_norm:
            y = _rms(y, fg_ref[...])
        o_ref[...] = y


def _ffn(x, g, w_in, w_out, final_g=None, *, tm=512, tf=1408):
    T, D = x.shape
    F = w_out.shape[0]
    tm = min(tm, T)
    nf = F // tf
    in_specs = [
        pl.BlockSpec((tm, D), lambda i, j: (i, 0)),
        pl.BlockSpec((1, D), lambda i, j: (0, 0)),
        pl.BlockSpec((D, tf), lambda i, j: (0, j)),
        pl.BlockSpec((D, tf), lambda i, j: (0, j + nf)),
        pl.BlockSpec((tf, D), lambda i, j: (j, 0)),
    ]
    args = [x, g.reshape(1, D), w_in, w_in, w_out]
    if final_g is not None:
        in_specs.append(pl.BlockSpec((1, D), lambda i, j: (0, 0)))
        args.append(final_g.reshape(1, D))
    return pl.pallas_call(
        functools.partial(_ffn_kernel, final_norm=final_g is not None),
        grid=(T // tm, nf),
        in_specs=in_specs,
        out_specs=pl.BlockSpec((tm, D), lambda i, j: (i, 0)),
        out_shape=jax.ShapeDtypeStruct((T, D), F32),
        scratch_shapes=[pltpu.VMEM((tm, D), BF16), pltpu.VMEM((tm, D), F32)],
        compiler_params=_params("parallel", "arbitrary"),
        name="ffn",
    )(*args)


def _norm_proj_kernel(x_ref, g_ref, w_ref, *o_refs, splits):
    h = _rms(x_ref[...], g_ref[...]).astype(BF16)
    off = 0
    for o_ref, n in zip(o_refs, splits):
        o_ref[...] = _dot(h, w_ref[:, off:off + n])
        off += n


def _norm_proj(x, g, w, splits, *, tm=512):
    T, D = x.shape
    N = w.shape[1]
    tm = min(tm, T)
    assert sum(splits) == N
    return pl.pallas_call(
        functools.partial(_norm_proj_kernel, splits=splits),
        grid=(T // tm,),
        in_specs=[
            pl.BlockSpec((tm, D), lambda i: (i, 0)),
            pl.BlockSpec((1, D), lambda i: (0, 0)),
            pl.BlockSpec((D, N), lambda i: (0, 0)),
        ],
        out_specs=[pl.BlockSpec((tm, n), lambda i: (i, 0)) for n in splits],
        out_shape=[jax.ShapeDtypeStruct((T, n), F32) for n in splits],
        compiler_params=_params("parallel"),
        name="norm_proj",
    )(x, g.reshape(1, D), w)


def _rwkv_kernel(p_ref, mu_ref, w0_ref, a0_ref, wwa_ref, gup_ref, kk_ref, ka_ref, rk_ref,
                 lng_ref, lnb_ref, o_ref, prev_ref, state_ref):
    C = RWKV_CHUNK
    W = RWKV_WIDTH
    T = p_ref.shape[1]
    s_idx = pl.program_id(1)

    @pl.when(s_idx == 0)
    def _():
        prev_ref[...] = jnp.zeros_like(prev_ref)
        state_ref[...] = jnp.zeros_like(state_ref)

    p = p_ref[0]
    row = lax.broadcasted_iota(jnp.int32, (T, 1), 0)
    shifted = jnp.where(row == 0, prev_ref[...], pltpu.roll(p, 1, axis=0))
    prev_ref[...] = p[T - 1:T, :]
    p = p + (shifted - p) * mu_ref[...]

    r = p[:, 0:W]
    k = p[:, W:2 * W]
    v = p[:, 2 * W:3 * W]
    lora = p[:, 3 * W:3 * W + LANES]
    g_lo = p[:, 3 * W + LANES:3 * W + 2 * LANES]

    lane = lax.broadcasted_iota(jnp.int32, (1, LANES), 1)
    first_head = lane < HEAD_DIM
    z = jnp.where(first_head, jnp.tanh(lora), lora)
    wa = _mm(_dot, z, wwa_ref[...], SPLIT_LORA)
    logw = -math.exp(-0.5) * jax.nn.sigmoid(w0_ref[...] + wa[:, :W])
    a = jax.nn.sigmoid(a0_ref[...] + wa[:, W:])
    g = _mm(_dot, jax.nn.sigmoid(g_lo), gup_ref[...], SPLIT_LORA)

    kk = k * kk_ref[...]
    k2 = k * (1.0 + (a - 1.0) * ka_ref[...])

    ri = lax.broadcasted_iota(jnp.int32, (LANES, LANES), 0)
    ci = lax.broadcasted_iota(jnp.int32, (LANES, LANES), 1)
    same_head = (ri >> HEAD_SHIFT) == (ci >> HEAD_SHIFT)
    head_ones = same_head.astype(F32)

    def head_sum(t):
        return jnp.concatenate(
            [_mm(_dot, t[:, i * LANES:(i + 1) * LANES], head_ones, SPLIT_SUM)
             for i in range(W // LANES)],
            axis=1)

    kkn = kk / jnp.maximum(jnp.sqrt(head_sum(kk * kk)), 1e-12)
    b = kkn * a
    bonus = head_sum(r * k2 * rk_ref[...]) * v

    tr = lax.broadcasted_iota(jnp.int32, (T, T), 0)
    tc = lax.broadcasted_iota(jnp.int32, (T, T), 1)
    in_chunk_prefix = (tc <= tr) & ((tc >> CHUNK_SHIFT) == (tr >> CHUNK_SHIFT))
    logp = _mm(_dot, in_chunk_prefix.astype(F32), logw, SPLIT_CUMSUM)
    r_t = r * jnp.exp(logp)
    a_t = -kkn * jnp.exp(logp - logw)
    inv = jnp.exp(-logp)
    b_t = b * inv
    k_t = k2 * inv

    strict = same_head & (ci < ri)
    incl = same_head & (ci <= ri)
    eye = (ri == ci).astype(F32)

    def stack(t):
        return jnp.concatenate([jnp.where(first_head, t, 0.0), jnp.where(first_head, 0.0, t)], axis=0)

    n = 2 * C
    n_pairs = W // LANES
    states = [state_ref[i] for i in range(n_pairs)]
    y_chunks = []
    for c in range(T // C):
        rows = slice(c * C, (c + 1) * C)
        logp_end = logp[(c + 1) * C - 1:(c + 1) * C, :]
        to_end = jnp.exp(logp_end - logp[rows])
        b_e = b[rows] * to_end
        k_e = k2[rows] * to_end
        decay_end = jnp.exp(logp_end)
        pairs = range(n_pairs)
        lanes = [slice(i * LANES, (i + 1) * LANES) for i in pairs]
        ar_s = [jnp.concatenate([stack(a_t[rows, sl]), stack(r_t[rows, sl])], axis=0) for sl in lanes]
        bk_s = [jnp.concatenate([stack(b_t[rows, sl]), stack(k_t[rows, sl])], axis=0) for sl in lanes]
        v_s = [stack(v[rows, sl]) for sl in lanes]
        bk_e = [jnp.concatenate([stack(b_e[:, sl]), stack(k_e[:, sl])], axis=0) for sl in lanes]

        gram = [_mm(_dot_nt, ar_s[i], bk_s[i], SPLIT_CHUNK) for i in pairs]
        l_ab = [jnp.where(strict, gm[:n, :n], 0.0) for gm in gram]
        a_ak = [jnp.where(strict, gm[:n, n:], 0.0) for gm in gram]
        a_r = [jnp.concatenate([jnp.where(incl, gm[n:, :n], 0.0), jnp.where(incl, gm[n:, n:], 0.0)],
                               axis=1) for gm in gram]

        t_inv = [eye + l for l in l_ab]
        l_pow = [_mm(_dot, l, l, SPLIT_CHUNK) for l in l_ab]
        levels = int(math.log2(C)) - 1
        for level in range(levels - 1):
            both = [_mm(_dot, jnp.concatenate([t_inv[i], l_pow[i]], axis=0), l_pow[i], SPLIT_CHUNK)
                    for i in pairs]
            t_inv = [t_inv[i] + both[i][:n] for i in pairs]
            l_pow = [both[i][n:] for i in pairs]
        t_inv = [t_inv[i] + _mm(_dot, t_inv[i], l_pow[i], SPLIT_CHUNK) for i in pairs]

        from_state = [_mm(_dot_nt, ar_s[i], states[i], SPLIT_CHUNK) for i in pairs]
        av = [_mm(_dot, a_ak[i], v_s[i], SPLIT_CHUNK) for i in pairs]
        u_s = [_mm(_dot, t_inv[i], from_state[i][:n] + av[i], SPLIT_CHUNK) for i in pairs]
        uv = [jnp.concatenate([u_s[i], v_s[i]], axis=0) for i in pairs]
        y_s = [from_state[i][n:] + _mm(_dot, a_r[i], uv[i], SPLIT_CHUNK) for i in pairs]
        states = [states[i] * decay_end[:, lanes[i]] + _mm(_dot_tn, uv[i], bk_e[i], SPLIT_CHUNK)
                  for i in pairs]
        y_chunks.append(jnp.concatenate([y[:C] + y[C:] for y in y_s], axis=1))
    for i in range(n_pairs):
        state_ref[i] = states[i]

    y = jnp.concatenate(y_chunks, axis=0)
    mean = head_sum(y) * (1.0 / HEAD_DIM)
    yc = y - mean
    var = head_sum(yc * yc) * (1.0 / HEAD_DIM)
    y = yc * lax.rsqrt(var + LNX_EPS) * lng_ref[...] + lnb_ref[...]
    o_ref[0] = (y + bonus) * g


def _rwkv(p, mu, w0, w_up, a0, a_up, g_up, k_k, k_a, r_k, ln_g, ln_b, *, ts=256):
    B, S, P = p.shape
    W = RWKV_WIDTH
    C = RWKV_CHUNK
    wwa = jnp.zeros((LANES, 2 * W), F32)
    wwa = wwa.at[:DECAY_LORA, :W].set(w_up).at[DECAY_LORA:, W:].set(a_up)
    vec = lambda t: t.reshape(1, -1)
    row_spec = lambda n: pl.BlockSpec((1, n), lambda b, s: (0, 0))
    ts = min(ts, S)
    assert ts % C == 0 and S % ts == 0
    return pl.pallas_call(
        _rwkv_kernel,
        grid=(B, S // ts),
        in_specs=[
            pl.BlockSpec((1, ts, P), lambda b, s: (b, s, 0)),
            row_spec(P), row_spec(W), row_spec(W),
            pl.BlockSpec((LANES, 2 * W), lambda b, s: (0, 0)),
            pl.BlockSpec((GATE_LORA, W), lambda b, s: (0, 0)),
            row_spec(W), row_spec(W), row_spec(W), row_spec(W), row_spec(W),
        ],
        out_specs=pl.BlockSpec((1, ts, W), lambda b, s: (b, s, 0)),
        out_shape=jax.ShapeDtypeStruct((B, S, W), F32),
        scratch_shapes=[pltpu.VMEM((1, P), F32),
                        pltpu.VMEM((W // LANES, LANES, LANES), F32)],
        compiler_params=_params("parallel", "arbitrary"),
        name="rwkv7",
    )(p, vec(mu), vec(w0), vec(a0), wwa, g_up, vec(k_k), vec(k_a), vec(r_k), vec(ln_g), vec(ln_b))


def _rel_bias_kernel(tab_ref, o_ref, *, n_blocks):
    blk = MOBA_BLOCK
    c0 = pl.program_id(0) * blk
    kc = lax.broadcasted_iota(jnp.int32, (blk, blk), 0) + c0
    qi = lax.broadcasted_iota(jnp.int32, (blk, blk), 1)
    dist = qi + (n_blocks - 1) * blk - kc
    n = jnp.maximum(dist, 0)
    max_exact = REL_BUCKETS // 2
    nf = jnp.maximum(n, 1).astype(F32)
    large = max_exact + (jnp.log(nf / max_exact) / math.log(REL_MAX_DISTANCE / max_exact)
                         * (REL_BUCKETS - max_exact)).astype(jnp.int32)
    large = jnp.minimum(large, REL_BUCKETS - 1)
    bucket = jnp.where(n < max_exact, n, large)
    for h in range(MOBA_HEADS):
        tile = jnp.zeros((blk, blk), F32)
        for c in range(REL_BUCKETS):
            tile = jnp.where(bucket == c, tab_ref[h, c], tile)
        o_ref[h] = jnp.where(dist < 0, -jnp.inf, tile)


def _rel_bias_strip(rel_bias, n_blocks):
    blk = MOBA_BLOCK
    return pl.pallas_call(
        functools.partial(_rel_bias_kernel, n_blocks=n_blocks),
        grid=(n_blocks,),
        in_specs=[pl.BlockSpec(memory_space=pltpu.SMEM)],
        out_specs=pl.BlockSpec((MOBA_HEADS, blk, blk), lambda c: (0, c, 0)),
        out_shape=jax.ShapeDtypeStruct((MOBA_HEADS, n_blocks * blk, blk), F32),
        compiler_params=_params("parallel"),
        name="rel_bias_strip",
    )(rel_bias)


def _moba_kernel(q_ref, k_ref, v_ref, bias_ref, o_ref, *, n_blocks):
    blk = MOBA_BLOCK
    lane = lax.broadcasted_iota(jnp.int32, (1, LANES), 1)
    vrow = lax.broadcasted_iota(jnp.int32, (LANES, 1), 0)
    q = q_ref[0] * (HEAD_DIM ** -0.5)
    k = k_ref[0]
    k16 = k.astype(BF16)
    vt16 = v_ref[0].T.astype(BF16)
    kmean = jnp.mean(k.reshape(n_blocks, blk, LANES), axis=1)

    gates, q16 = [], []
    for e in range(HEADS_PER_TILE):
        qe = jnp.where((lane >> HEAD_SHIFT) == e, q, 0.0)
        gates.append(_dot_nt(kmean, qe, HIGHEST))
        q16.append(qe.astype(BF16))

    for qb in range(n_blocks):
        rows = slice(qb * blk, (qb + 1) * blk)
        n_keys = (qb + 1) * blk
        bias_lo = (n_blocks - 1 - qb) * blk
        out_t = []
        for e in range(HEADS_PER_TILE):
            s_t = _dot_nt(k16[:n_keys], q16[e][rows])
            g = [gates[e][j:j + 1, rows] for j in range(qb)]
            tiles = []
            for j in range(qb + 1):
                t = s_t[j * blk:(j + 1) * blk] + bias_ref[e, bias_lo + j * blk:bias_lo + (j + 1) * blk, :]
                if j < qb:
                    rank = jnp.zeros((1, blk), jnp.int32)
                    for jj in range(qb):
                        if jj != j:
                            ahead = (g[jj] >= g[j]) if jj < j else (g[jj] > g[j])
                            rank = rank + jnp.where(ahead, 1, 0)
                    t = jnp.where(rank < MOBA_TOPK, t, -jnp.inf)
                tiles.append(t)
            m = tiles[0].max(axis=0, keepdims=True)
            for t in tiles[1:]:
                m = jnp.maximum(m, t.max(axis=0, keepdims=True))
            probs = [jnp.exp(t - m) for t in tiles]
            l = probs[0].sum(axis=0, keepdims=True)
            for pr in probs[1:]:
                l = l + pr.sum(axis=0, keepdims=True)
            p16 = jnp.concatenate([pr.astype(BF16) for pr in probs], axis=0)
            out_t.append(_dot(vt16[:, :n_keys], p16) / l)
        o_ref[0, rows, :] = jnp.where(vrow < HEAD_DIM, out_t[0], out_t[1]).T


def _moba(qkv, bias_strip):
    B, S, _ = qkv.shape
    blk = MOBA_BLOCK
    n_blocks = S // blk
    n_pairs = MOBA_WIDTH // LANES
    return pl.pallas_call(
        functools.partial(_moba_kernel, n_blocks=n_blocks),
        grid=(n_pairs, B),
        in_specs=[
            pl.BlockSpec((1, S, LANES), lambda p, b: (b, 0, p)),
            pl.BlockSpec((1, S, LANES), lambda p, b: (b, 0, n_pairs + p)),
            pl.BlockSpec((1, S, LANES), lambda p, b: (b, 0, 2 * n_pairs + p)),
            pl.BlockSpec((HEADS_PER_TILE, S, blk), lambda p, b: (p, 0, 0)),
        ],
        out_specs=pl.BlockSpec((1, S, LANES), lambda p, b: (b, 0, p)),
        out_shape=jax.ShapeDtypeStruct((B, S, MOBA_WIDTH), F32),
        compiler_params=_params("parallel", "parallel"),
        name="moba",
    )(qkv, qkv, qkv, bias_strip)


def _mix_out_kernel(x_ref, a_ref, b_ref, wa_ref, wb_ref, o_ref):
    o_ref[...] = (x_ref[...] + _dot(a_ref[...].astype(BF16), wa_ref[...])
                  + _dot(b_ref[...].astype(BF16), wb_ref[...]))


def _mix_out(x, ya, yb, w, *, tm=512):
    T, D = x.shape
    na, nb = ya.shape[1], yb.shape[1]
    tm = min(tm, T)
    return pl.pallas_call(
        _mix_out_kernel,
        grid=(T // tm,),
        in_specs=[
            pl.BlockSpec((tm, D), lambda i: (i, 0)),
            pl.BlockSpec((tm, na), lambda i: (i, 0)),
            pl.BlockSpec((tm, nb), lambda i: (i, 0)),
            pl.BlockSpec((na, D), lambda i: (0, 0)),
            pl.BlockSpec((nb, D), lambda i: (0, 0)),
        ],
        out_specs=pl.BlockSpec((tm, D), lambda i: (i, 0)),
        out_shape=jax.ShapeDtypeStruct((T, D), F32),
        compiler_params=_params("parallel"),
        name="mix_out",
    )(x, ya, yb, w[:na], w[na:])


def _xattn_kernel(x_ref, g_ref, wq_ref, kv_ref, wo_ref, o_ref):
    x = x_ref[0]
    D = x.shape[-1]
    dh = D // XATTN_HEADS
    h = _rms(x, g_ref[...]).astype(BF16)
    q = _dot(h, wq_ref[...])
    heads = []
    for i in range(XATTN_HEADS):
        qh = q[:, i * dh:(i + 1) * dh].astype(BF16)
        kh = kv_ref[0, :, i * dh:(i + 1) * dh].astype(BF16)
        vh = kv_ref[0, :, D + i * dh:D + (i + 1) * dh].astype(BF16)
        s = _dot_nt(qh, kh) * (dh ** -0.5)
        s = s - jnp.max(s, axis=-1, keepdims=True)
        pr = jnp.exp(s)
        pr = pr / jnp.sum(pr, axis=-1, keepdims=True)
        heads.append(_dot(pr.astype(BF16), vh))
    o = jnp.concatenate(heads, axis=1).astype(BF16)
    o_ref[0] = x + _dot(o, wo_ref[...])


def _xattn(x, g, w_q, kv, w_o, *, tm=512):
    B, S, D = x.shape
    M = kv.shape[1]
    tm = min(tm, S)
    return pl.pallas_call(
        _xattn_kernel,
        grid=(B, S // tm),
        in_specs=[
            pl.BlockSpec((1, tm, D), lambda b, i: (b, i, 0)),
            pl.BlockSpec((1, D), lambda b, i: (0, 0)),
            pl.BlockSpec((D, D), lambda b, i: (0, 0)),
            pl.BlockSpec((1, M, 2 * D), lambda b, i: (b, 0, 0)),
            pl.BlockSpec((D, D), lambda b, i: (0, 0)),
        ],
        out_specs=pl.BlockSpec((1, tm, D), lambda b, i: (b, i, 0)),
        out_shape=jax.ShapeDtypeStruct((B, S, D), F32),
        compiler_params=_params("parallel", "parallel"),
        name="xattn",
    )(x, g.reshape(1, D), w_q, kv, w_o)


def kernel(x, mem, rel_bias, final_norm_g, ffn1_norm_g, ffn1_w_in, ffn1_w_out, mix_norm_g, w_mix_in, w_mix_out, rwkv_mu, rwkv_w0, rwkv_w_up, rwkv_a0, rwkv_a_up, rwkv_g_up, rwkv_k_k, rwkv_k_a, rwkv_r_k, rwkv_ln_g, rwkv_ln_b, xattn_norm_g, mem_norm_g, xattn_w_q, xattn_w_kv, xattn_w_o, ffn2_norm_g, ffn2_w_in, ffn2_w_out):
    B, S, D = x.shape
    M = mem.shape[1]
    depth = ffn1_w_in.shape[0]
    T = B * S
    bf = lambda w: w.astype(BF16)

    bias_strip = _rel_bias_strip(rel_bias, S // MOBA_BLOCK)
    mem2 = mem.reshape(B * M, D)
    x = x.reshape(T, D)
    for l in range(depth):
        x = _ffn(x, ffn1_norm_g[l], bf(ffn1_w_in[l]), bf(ffn1_w_out[l]))
        p_rwkv, qkv = _norm_proj(x, mix_norm_g[l], bf(w_mix_in[l]), (RWKV_PROJ, MOBA_PROJ))
        y_rwkv = _rwkv(p_rwkv.reshape(B, S, RWKV_PROJ), rwkv_mu[l], rwkv_w0[l], rwkv_w_up[l],
                       rwkv_a0[l], rwkv_a_up[l], rwkv_g_up[l], rwkv_k_k[l], rwkv_k_a[l],
                       rwkv_r_k[l], rwkv_ln_g[l], rwkv_ln_b[l])
        y_moba = _moba(qkv.reshape(B, S, MOBA_PROJ), bias_strip)
        x = _mix_out(x, y_rwkv.reshape(T, RWKV_WIDTH), y_moba.reshape(T, MOBA_WIDTH), bf(w_mix_out[l]))
        (kv,) = _norm_proj(mem2, mem_norm_g[l], bf(xattn_w_kv[l]), (2 * D,))
        x = _xattn(x.reshape(B, S, D), xattn_norm_g[l], bf(xattn_w_q[l]), kv.reshape(B, M, 2 * D),
                   bf(xattn_w_o[l])).reshape(T, D)
        x = _ffn(x, ffn2_norm_g[l], bf(ffn2_w_in[l]), bf(ffn2_w_out[l]),
                 final_norm_g if l == depth - 1 else None)
    return x.reshape(B, S, D)
```

```python
import functools
import math

import jax
import jax.numpy as jnp
from jax import lax
from jax.experimental import pallas as pl
from jax.experimental.pallas import tpu as pltpu

F32 = jnp.float32
BF16 = jnp.bfloat16
HIGHEST = lax.Precision.HIGHEST

HEAD_DIM = 64
RWKV_WIDTH = 512
MOBA_WIDTH = 512
MOBA_HEADS = MOBA_WIDTH // HEAD_DIM
DECAY_LORA = 64
ICLR_LORA = 64
GATE_LORA = 128
RWKV_PROJ = 3 * RWKV_WIDTH + DECAY_LORA + ICLR_LORA + GATE_LORA
MOBA_PROJ = 3 * MOBA_WIDTH
LNX_EPS = 64e-5
MOBA_BLOCK = 256
MOBA_TOPK = 3
REL_BUCKETS = 32
REL_MAX_DISTANCE = 1024
XATTN_HEADS = 4
FFN_RES_WEIGHT = 0.5
NORM_EPS = 1e-6

LANES = 128
MXU_DIM = 256
HEADS_PER_TILE = LANES // HEAD_DIM
HEAD_SHIFT = HEAD_DIM.bit_length() - 1
RWKV_CHUNK = 64
CHUNK_SHIFT = RWKV_CHUNK.bit_length() - 1
SPLIT_LORA = (1, 1, 1)
SPLIT_SUM = (2, 1, 2)
SPLIT_CUMSUM = (1, 2, 2)
SPLIT_CHUNK = (1, 1, 1)
VMEM_LIMIT = 56 * 1024 * 1024


def _rms(x, g):
    ms = jnp.mean(x * x, axis=-1, keepdims=True)
    return x * lax.rsqrt(ms + NORM_EPS) * g


def _dot(a, b, precision=None):
    return jnp.dot(a, b, precision=precision, preferred_element_type=F32)


def _dot_nt(a, b, precision=None):
    return lax.dot_general(a, b, (((1,), (1,)), ((), ())), precision=precision,
                           preferred_element_type=F32)


def _dot_tn(a, b, precision=None):
    return lax.dot_general(a, b, (((0,), (0,)), ((), ())), precision=precision,
                           preferred_element_type=F32)


def _bf16_terms(x, n):
    terms = []
    for i in range(n):
        t = x.astype(BF16)
        terms.append(t)
        if i + 1 < n:
            x = x - t.astype(F32)
    return terms


def _mm(dot, a, b, split):
    na, nb, order = split
    at, bt = _bf16_terms(a, na), _bf16_terms(b, nb)
    acc = None
    for i in range(na):
        for j in range(nb):
            if i + j < order:
                t = dot(at[i], bt[j])
                acc = t if acc is None else acc + t
    return acc


def _params(*semantics):
    return pltpu.CompilerParams(dimension_semantics=semantics, vmem_limit_bytes=VMEM_LIMIT)


def _ffn_kernel(x_ref, g_ref, wg_ref, wu_ref, wo_ref, *rest, final_norm):
    if final_norm:
        fg_ref, o_ref, h_ref, acc_ref = rest
    else:
        o_ref, h_ref, acc_ref = rest
    j = pl.program_id(1)

    @pl.when(j == 0)
    def _():
        h_ref[...] = _rms(x_ref[...], g_ref[...]).astype(BF16)
        acc_ref[...] = jnp.zeros_like(acc_ref)

    h = h_ref[...]
    gate = _dot(h, wg_ref[...])
    up = _dot(h, wu_ref[...])
    act = gate * jax.nn.sigmoid(gate) * up
    acc_ref[...] += _dot(act.astype(BF16), wo_ref[...])

    @pl.when(j == pl.num_programs(1) - 1)
    def _():
        y = x_ref[...] + FFN_RES_WEIGHT * acc_ref[...]
        if final_norm:
            y = _rms(y, fg_ref[...])
        o_ref[...] = y


def _ffn(x, g, w_in, w_out, final_g=None, *, tm=512, tf=1408):
    T, D = x.shape
    F = w_out.shape[0]
    tm = min(tm, T)
    nf = F // tf
    in_specs = [
        pl.BlockSpec((tm, D), lambda i, j: (i, 0)),
        pl.BlockSpec((1, D), lambda i, j: (0, 0)),
        pl.BlockSpec((D, tf), lambda i, j: (0, j)),
        pl.BlockSpec((D, tf), lambda i, j: (0, j + nf)),
        pl.BlockSpec((tf, D), lambda i, j: (j, 0)),
    ]
    args = [x, g.reshape(1, D), w_in, w_in, w_out]
    if final_g is not None:
        in_specs.append(pl.BlockSpec((1, D), lambda i, j: (0, 0)))
        args.append(final_g.reshape(1, D))
    return pl.pallas_call(
        functools.partial(_ffn_kernel, final_norm=final_g is not None),
        grid=(T // tm, nf),
        in_specs=in_specs,
        out_specs=pl.BlockSpec((tm, D), lambda i, j: (i, 0)),
        out_shape=jax.ShapeDtypeStruct((T, D), F32),
        scratch_shapes=[pltpu.VMEM((tm, D), BF16), pltpu.VMEM((tm, D), F32)],
        compiler_params=_params("parallel", "arbitrary"),
        name="ffn",
    )(*args)


def _norm_proj_kernel(x_ref, g_ref, w_ref, *o_refs, splits):
    h = _rms(x_ref[...], g_ref[...]).astype(BF16)
    off = 0
    for o_ref, n in zip(o_refs, splits):
        o_ref[...] = _dot(h, w_ref[:, off:off + n])
        off += n


def _norm_proj(x, g, w, splits, *, tm=512):
    T, D = x.shape
    N = w.shape[1]
    tm = min(tm, T)
    assert sum(splits) == N
    return pl.pallas_call(
        functools.partial(_norm_proj_kernel, splits=splits),
        grid=(T // tm,),
        in_specs=[
            pl.BlockSpec((tm, D), lambda i: (i, 0)),
            pl.BlockSpec((1, D), lambda i: (0, 0)),
            pl.BlockSpec((D, N), lambda i: (0, 0)),
        ],
        out_specs=[pl.BlockSpec((tm, n), lambda i: (i, 0)) for n in splits],
        out_shape=[jax.ShapeDtypeStruct((T, n), F32) for n in splits],
        compiler_params=_params("parallel"),
        name="norm_proj",
    )(x, g.reshape(1, D), w)


def _rwkv_kernel(p_ref, mu_ref, w0_ref, a0_ref, wwa_ref, gup_ref, kk_ref, ka_ref, rk_ref,
                 lng_ref, lnb_ref, o_ref, prev_ref, state_ref):
    C = RWKV_CHUNK
    W = RWKV_WIDTH
    NB, T, P = p_ref.shape
    R = NB * T
    s_idx = pl.program_id(1)

    @pl.when(s_idx == 0)
    def _():
        prev_ref[...] = jnp.zeros_like(prev_ref)
        state_ref[...] = jnp.zeros_like(state_ref)

    p = p_ref[...].reshape(R, P)
    row = lax.broadcasted_iota(jnp.int32, (R, 1), 0)
    shifted = pltpu.roll(p, 1, axis=0)
    for nb in range(NB):
        shifted = jnp.where(row == nb * T, prev_ref[nb], shifted)
        prev_ref[nb] = p[(nb + 1) * T - 1:(nb + 1) * T, :]
    p = p + (shifted - p) * mu_ref[...]

    r = p[:, 0:W]
    k = p[:, W:2 * W]
    v = p[:, 2 * W:3 * W]
    lora = p[:, 3 * W:3 * W + LANES]
    g_lo = p[:, 3 * W + LANES:3 * W + 2 * LANES]

    lane = lax.broadcasted_iota(jnp.int32, (1, LANES), 1)
    first_head = lane < HEAD_DIM
    z = jnp.where(first_head, jnp.tanh(lora), lora)
    wa = _mm(_dot, z, wwa_ref[...], SPLIT_LORA)
    logw = -math.exp(-0.5) * jax.nn.sigmoid(w0_ref[...] + wa[:, :W])
    a = jax.nn.sigmoid(a0_ref[...] + wa[:, W:])
    g = _mm(_dot, jax.nn.sigmoid(g_lo), gup_ref[...], SPLIT_LORA)

    kk = k * kk_ref[...]
    k2 = k * (1.0 + (a - 1.0) * ka_ref[...])

    ri = lax.broadcasted_iota(jnp.int32, (LANES, LANES), 0)
    ci = lax.broadcasted_iota(jnp.int32, (LANES, LANES), 1)
    same_head = (ri >> HEAD_SHIFT) == (ci >> HEAD_SHIFT)
    ri2 = lax.broadcasted_iota(jnp.int32, (MXU_DIM, MXU_DIM), 0)
    ci2 = lax.broadcasted_iota(jnp.int32, (MXU_DIM, MXU_DIM), 1)
    head_ones = ((ri2 >> HEAD_SHIFT) == (ci2 >> HEAD_SHIFT)).astype(F32)

    def head_sum(t):
        return jnp.concatenate(
            [_mm(_dot, t[:, i * MXU_DIM:(i + 1) * MXU_DIM], head_ones, SPLIT_SUM)
             for i in range(W // MXU_DIM)],
            axis=1)

    kkn = kk / jnp.maximum(jnp.sqrt(head_sum(kk * kk)), 1e-12)
    b = kkn * a
    bonus = head_sum(r * k2 * rk_ref[...]) * v

    tr = lax.broadcasted_iota(jnp.int32, (MXU_DIM, MXU_DIM), 0)
    tc = lax.broadcasted_iota(jnp.int32, (MXU_DIM, MXU_DIM), 1)
    in_chunk_prefix = ((tc <= tr) & ((tc >> CHUNK_SHIFT) == (tr >> CHUNK_SHIFT))).astype(F32)
    logp = jnp.concatenate(
        [_mm(_dot, in_chunk_prefix, logw[i * MXU_DIM:(i + 1) * MXU_DIM], SPLIT_CUMSUM)
         for i in range(R // MXU_DIM)], axis=0)
    r_t = r * jnp.exp(logp)
    a_t = -kkn * jnp.exp(logp - logw)
    inv = jnp.exp(-logp)
    b_t = b * inv
    k_t = k2 * inv

    tok = lax.broadcasted_iota(jnp.int32, (C, LANES), 0)
    col_tok = lax.broadcasted_iota(jnp.int32, (C, LANES), 1) & (C - 1)
    strict = col_tok < tok
    incl = col_tok <= tok
    incl2 = jnp.concatenate([incl, incl], axis=1)
    eye = (col_tok == tok).astype(F32)

    def stack(t):
        return jnp.concatenate([jnp.where(first_head, t, 0.0), jnp.where(first_head, 0.0, t)], axis=0)

    n_pairs = W // LANES
    n_chunks = T // C
    lanes = [slice(i * LANES, (i + 1) * LANES) for i in range(n_pairs)]
    units = [(q, i) for q in range(R // C) for i in range(n_pairs)]
    rows_of = lambda q: slice(q * C, (q + 1) * C)

    ar = {(q, i): jnp.concatenate([a_t[rows_of(q), lanes[i]], r_t[rows_of(q), lanes[i]]], axis=0)
          for q, i in units}
    bk = {(q, i): jnp.concatenate([stack(b_t[rows_of(q), lanes[i]]), stack(k_t[rows_of(q), lanes[i]])],
                                  axis=0) for q, i in units}
    v_s = {(q, i): stack(v[rows_of(q), lanes[i]]) for q, i in units}
    gram = {u_: _mm(_dot_nt, ar[u_], bk[u_], SPLIT_CHUNK) for u_ in units}
    l_ab = {u_: jnp.where(strict, gram[u_][:C, :LANES], 0.0) for u_ in units}
    a_ak = {u_: jnp.where(strict, gram[u_][:C, LANES:], 0.0) for u_ in units}
    a_r = {u_: jnp.where(incl2, gram[u_][C:], 0.0) for u_ in units}

    t_inv = {u_: eye + l_ab[u_] for u_ in units}
    l_pow = {u_: _mm(_dot, l_ab[u_], stack(l_ab[u_]), SPLIT_CHUNK) for u_ in units}
    levels = int(math.log2(C)) - 1
    for level in range(levels - 1):
        both = {u_: _mm(_dot, jnp.concatenate([t_inv[u_], l_pow[u_]], axis=0), stack(l_pow[u_]), SPLIT_CHUNK)
                for u_ in units}
        t_inv = {u_: t_inv[u_] + both[u_][:C] for u_ in units}
        l_pow = {u_: both[u_][C:] for u_ in units}
    t_inv = {u_: t_inv[u_] + _mm(_dot, t_inv[u_], stack(l_pow[u_]), SPLIT_CHUNK) for u_ in units}
    av = {u_: _mm(_dot, a_ak[u_], v_s[u_], SPLIT_CHUNK) for u_ in units}

    states = {(nb, i): state_ref[nb, i] for nb in range(NB) for i in range(n_pairs)}
    y_chunks = {}
    for c in range(n_chunks):
        now = [(nb, nb * n_chunks + c, i) for nb in range(NB) for i in range(n_pairs)]
        logp_end = {q: logp[(q + 1) * C - 1:(q + 1) * C, :] for _, q, _ in now}
        to_end = {q: jnp.exp(logp_end[q] - logp[rows_of(q)]) for q in logp_end}
        bk_e = {(q, i): jnp.concatenate([b[rows_of(q), lanes[i]] * to_end[q][:, lanes[i]],
                                         k2[rows_of(q), lanes[i]] * to_end[q][:, lanes[i]]], axis=0)
                for _, q, i in now}
        from_state = {(q, i): _mm(_dot_nt, ar[q, i], states[nb, i], SPLIT_CHUNK) for nb, q, i in now}
        u = {(q, i): _mm(_dot, t_inv[q, i], stack(from_state[q, i][:C] + av[q, i]), SPLIT_CHUNK)
             for _, q, i in now}
        outer = {(q, i): _mm(_dot_tn, jnp.concatenate([u[q, i], v[rows_of(q), lanes[i]]], axis=0),
                             bk_e[q, i], SPLIT_CHUNK) for _, q, i in now}
        for nb, q, i in now:
            y_chunks[q, i] = from_state[q, i][C:] + _mm(
                _dot, a_r[q, i], jnp.concatenate([stack(u[q, i]), v_s[q, i]], axis=0), SPLIT_CHUNK)
            states[nb, i] = (states[nb, i] * jnp.exp(logp_end[q][:, lanes[i]])
                             + jnp.where(same_head, outer[q, i], 0.0))
    for nb in range(NB):
        for i in range(n_pairs):
            state_ref[nb, i] = states[nb, i]

    y = jnp.concatenate([jnp.concatenate([y_chunks[q, i] for i in range(n_pairs)], axis=1)
                         for q in range(R // C)], axis=0)
    mean = head_sum(y) * (1.0 / HEAD_DIM)
    yc = y - mean
    var = head_sum(yc * yc) * (1.0 / HEAD_DIM)
    y = yc * lax.rsqrt(var + LNX_EPS) * lng_ref[...] + lnb_ref[...]
    o_ref[...] = ((y + bonus) * g).reshape(NB, T, W)


def _rwkv(p, mu, w0, w_up, a0, a_up, g_up, k_k, k_a, r_k, ln_g, ln_b, *, ts=256, nb=2):
    B, S, P = p.shape
    W = RWKV_WIDTH
    C = RWKV_CHUNK
    wwa = jnp.zeros((LANES, 2 * W), F32)
    wwa = wwa.at[:DECAY_LORA, :W].set(w_up).at[DECAY_LORA:, W:].set(a_up)
    vec = lambda t: t.reshape(1, -1)
    row_spec = lambda n: pl.BlockSpec((1, n), lambda b, s: (0, 0))
    ts = min(ts, S)
    nb = min(nb, B)
    assert ts % C == 0 and S % ts == 0 and B % nb == 0 and (nb * ts) % MXU_DIM == 0
    return pl.pallas_call(
        _rwkv_kernel,
        grid=(B // nb, S // ts),
        in_specs=[
            pl.BlockSpec((nb, ts, P), lambda b, s: (b, s, 0)),
            row_spec(P), row_spec(W), row_spec(W),
            pl.BlockSpec((LANES, 2 * W), lambda b, s: (0, 0)),
            pl.BlockSpec((GATE_LORA, W), lambda b, s: (0, 0)),
            row_spec(W), row_spec(W), row_spec(W), row_spec(W), row_spec(W),
        ],
        out_specs=pl.BlockSpec((nb, ts, W), lambda b, s: (b, s, 0)),
        out_shape=jax.ShapeDtypeStruct((B, S, W), F32),
        scratch_shapes=[pltpu.VMEM((nb, 1, P), F32),
                        pltpu.VMEM((nb, W // LANES, LANES, LANES), F32)],
        compiler_params=_params("parallel", "arbitrary"),
        name="rwkv7",
    )(p, vec(mu), vec(w0), vec(a0), wwa, g_up, vec(k_k), vec(k_a), vec(r_k), vec(ln_g), vec(ln_b))


def _rel_bias_kernel(tab_ref, o_ref, *, n_blocks):
    blk = MOBA_BLOCK
    c0 = pl.program_id(0) * blk
    kc = lax.broadcasted_iota(jnp.int32, (blk, blk), 0) + c0
    qi = lax.broadcasted_iota(jnp.int32, (blk, blk), 1)
    dist = qi + (n_blocks - 1) * blk - kc
    n = jnp.maximum(dist, 0)
    max_exact = REL_BUCKETS // 2
    nf = jnp.maximum(n, 1).astype(F32)
    large = max_exact + (jnp.log(nf / max_exact) / math.log(REL_MAX_DISTANCE / max_exact)
                         * (REL_BUCKETS - max_exact)).astype(jnp.int32)
    large = jnp.minimum(large, REL_BUCKETS - 1)
    bucket = jnp.where(n < max_exact, n, large)
    for h in range(MOBA_HEADS):
        tile = jnp.zeros((blk, blk), F32)
        for c in range(REL_BUCKETS):
            tile = jnp.where(bucket == c, tab_ref[h, c], tile)
        o_ref[h] = jnp.where(dist < 0, -jnp.inf, tile)


def _rel_bias_strip(rel_bias, n_blocks):
    blk = MOBA_BLOCK
    return pl.pallas_call(
        functools.partial(_rel_bias_kernel, n_blocks=n_blocks),
        grid=(n_blocks,),
        in_specs=[pl.BlockSpec(memory_space=pltpu.SMEM)],
        out_specs=pl.BlockSpec((MOBA_HEADS, blk, blk), lambda c: (0, c, 0)),
        out_shape=jax.ShapeDtypeStruct((MOBA_HEADS, n_blocks * blk, blk), F32),
        compiler_params=_params("parallel"),
        name="rel_bias_strip",
    )(rel_bias)


def _moba_kernel(q_ref, k_ref, v_ref, bias_ref, o_ref, *, n_blocks):
    blk = MOBA_BLOCK
    lane = lax.broadcasted_iota(jnp.int32, (1, LANES), 1)
    vrow = lax.broadcasted_iota(jnp.int32, (LANES, 1), 0)
    q = q_ref[0] * (HEAD_DIM ** -0.5)
    k = k_ref[0]
    k16 = k.astype(BF16)
    vt16 = v_ref[0].T.astype(BF16)
    kmean = jnp.mean(k.reshape(n_blocks, blk, LANES), axis=1)

    gates, q16 = [], []
    for e in range(HEADS_PER_TILE):
        qe = jnp.where((lane >> HEAD_SHIFT) == e, q, 0.0)
        gates.append(_dot_nt(kmean, qe, HIGHEST))
        q16.append(qe.astype(BF16))

    for qb in range(n_blocks):
        rows = slice(qb * blk, (qb + 1) * blk)
        n_keys = (qb + 1) * blk
        bias_lo = (n_blocks - 1 - qb) * blk
        out_t = []
        for e in range(HEADS_PER_TILE):
            s_t = _dot_nt(k16[:n_keys], q16[e][rows])
            g = [gates[e][j:j + 1, rows] for j in range(qb)]
            tiles = []
            for j in range(qb + 1):
                t = s_t[j * blk:(j + 1) * blk] + bias_ref[e, bias_lo + j * blk:bias_lo + (j + 1) * blk, :]
                if j < qb:
                    rank = jnp.zeros((1, blk), jnp.int32)
                    for jj in range(qb):
                        if jj != j:
                            ahead = (g[jj] >= g[j]) if jj < j else (g[jj] > g[j])
                            rank = rank + jnp.where(ahead, 1, 0)
                    t = jnp.where(rank < MOBA_TOPK, t, -jnp.inf)
                tiles.append(t)
            m = tiles[0].max(axis=0, keepdims=True)
            for t in tiles[1:]:
                m = jnp.maximum(m, t.max(axis=0, keepdims=True))
            probs = [jnp.exp(t - m) for t in tiles]
            l = probs[0].sum(axis=0, keepdims=True)
            for pr in probs[1:]:
                l = l + pr.sum(axis=0, keepdims=True)
            p16 = jnp.concatenate([pr.astype(BF16) for pr in probs], axis=0)
            out_t.append(_dot(vt16[:, :n_keys], p16) / l)
        o_ref[0, rows, :] = jnp.where(vrow < HEAD_DIM, out_t[0], out_t[1]).T


def _moba(qkv, bias_strip):
    B, S, _ = qkv.shape
    blk = MOBA_BLOCK
    n_blocks = S // blk
    n_pairs = MOBA_WIDTH // LANES
    return pl.pallas_call(
        functools.partial(_moba_kernel, n_blocks=n_blocks),
        grid=(n_pairs, B),
        in_specs=[
            pl.BlockSpec((1, S, LANES), lambda p, b: (b, 0, p)),
            pl.BlockSpec((1, S, LANES), lambda p, b: (b, 0, n_pairs + p)),
            pl.BlockSpec((1, S, LANES), lambda p, b: (b, 0, 2 * n_pairs + p)),
            pl.BlockSpec((HEADS_PER_TILE, S, blk), lambda p, b: (p, 0, 0)),
        ],
        out_specs=pl.BlockSpec((1, S, LANES), lambda p, b: (b, 0, p)),
        out_shape=jax.ShapeDtypeStruct((B, S, MOBA_WIDTH), F32),
        compiler_params=_params("parallel", "parallel"),
        name="moba",
    )(qkv, qkv, qkv, bias_strip)


def _mix_out_kernel(x_ref, a_ref, b_ref, wa_ref, wb_ref, o_ref):
    o_ref[...] = (x_ref[...] + _dot(a_ref[...].astype(BF16), wa_ref[...])
                  + _dot(b_ref[...].astype(BF16), wb_ref[...]))


def _mix_out(x, ya, yb, w, *, tm=512):
    T, D = x.shape
    na, nb = ya.shape[1], yb.shape[1]
    tm = min(tm, T)
    return pl.pallas_call(
        _mix_out_kernel,
        grid=(T // tm,),
        in_specs=[
            pl.BlockSpec((tm, D), lambda i: (i, 0)),
            pl.BlockSpec((tm, na), lambda i: (i, 0)),
            pl.BlockSpec((tm, nb), lambda i: (i, 0)),
            pl.BlockSpec((na, D), lambda i: (0, 0)),
            pl.BlockSpec((nb, D), lambda i: (0, 0)),
        ],
        out_specs=pl.BlockSpec((tm, D), lambda i: (i, 0)),
        out_shape=jax.ShapeDtypeStruct((T, D), F32),
        compiler_params=_params("parallel"),
        name="mix_out",
    )(x, ya, yb, w[:na], w[na:])


def _xattn_kernel(x_ref, g_ref, wq_ref, kv_ref, wo_ref, o_ref):
    x = x_ref[0]
    D = x.shape[-1]
    dh = D // XATTN_HEADS
    h = _rms(x, g_ref[...]).astype(BF16)
    q = _dot(h, wq_ref[...])
    heads = []
    for i in range(XATTN_HEADS):
        qh = q[:, i * dh:(i + 1) * dh].astype(BF16)
        kh = kv_ref[0, :, i * dh:(i + 1) * dh].astype(BF16)
        vh = kv_ref[0, :, D + i * dh:D + (i + 1) * dh].astype(BF16)
        s = _dot_nt(qh, kh) * (dh ** -0.5)
        s = s - jnp.max(s, axis=-1, keepdims=True)
        pr = jnp.exp(s)
        pr = pr / jnp.sum(pr, axis=-1, keepdims=True)
        heads.append(_dot(pr.astype(BF16), vh))
    o = jnp.concatenate(heads, axis=1).astype(BF16)
    o_ref[0] = x + _dot(o, wo_ref[...])


def _xattn(x, g, w_q, kv, w_o, *, tm=512):
    B, S, D = x.shape
    M = kv.shape[1]
    tm = min(tm, S)
    return pl.pallas_call(
        _xattn_kernel,
        grid=(B, S // tm),
        in_specs=[
            pl.BlockSpec((1, tm, D), lambda b, i: (b, i, 0)),
            pl.BlockSpec((1, D), lambda b, i: (0, 0)),
            pl.BlockSpec((D, D), lambda b, i: (0, 0)),
            pl.BlockSpec((1, M, 2 * D), lambda b, i: (b, 0, 0)),
            pl.BlockSpec((D, D), lambda b, i: (0, 0)),
        ],
        out_specs=pl.BlockSpec((1, tm, D), lambda b, i: (b, i, 0)),
        out_shape=jax.ShapeDtypeStruct((B, S, D), F32),
        compiler_params=_params("parallel", "parallel"),
        name="xattn",
    )(x, g.reshape(1, D), w_q, kv, w_o)


def kernel(x, mem, rel_bias, final_norm_g, ffn1_norm_g, ffn1_w_in, ffn1_w_out, mix_norm_g, w_mix_in, w_mix_out, rwkv_mu, rwkv_w0, rwkv_w_up, rwkv_a0, rwkv_a_up, rwkv_g_up, rwkv_k_k, rwkv_k_a, rwkv_r_k, rwkv_ln_g, rwkv_ln_b, xattn_norm_g, mem_norm_g, xattn_w_q, xattn_w_kv, xattn_w_o, ffn2_norm_g, ffn2_w_in, ffn2_w_out):
    B, S, D = x.shape
    M = mem.shape[1]
    depth = ffn1_w_in.shape[0]
    T = B * S
    bf = lambda w: w.astype(BF16)

    bias_strip = _rel_bias_strip(rel_bias, S // MOBA_BLOCK)
    mem2 = mem.reshape(B * M, D)
    x = x.reshape(T, D)
    for l in range(depth):
        x = _ffn(x, ffn1_norm_g[l], bf(ffn1_w_in[l]), bf(ffn1_w_out[l]))
        p_rwkv, qkv = _norm_proj(x, mix_norm_g[l], bf(w_mix_in[l]), (RWKV_PROJ, MOBA_PROJ))
        y_rwkv = _rwkv(p_rwkv.reshape(B, S, RWKV_PROJ), rwkv_mu[l], rwkv_w0[l], rwkv_w_up[l],
                       rwkv_a0[l], rwkv_a_up[l], rwkv_g_up[l], rwkv_k_k[l], rwkv_k_a[l],
                       rwkv_r_k[l], rwkv_ln_g[l], rwkv_ln_b[l])
        y_moba = _moba(qkv.reshape(B, S, MOBA_PROJ), bias_strip)
        x = _mix_out(x, y_rwkv.reshape(T, RWKV_WIDTH), y_moba.reshape(T, MOBA_WIDTH), bf(w_mix_out[l]))
        (kv,) = _norm_proj(mem2, mem_norm_g[l], bf(xattn_w_kv[l]), (2 * D,))
        x = _xattn(x.reshape(B, S, D), xattn_norm_g[l], bf(xattn_w_q[l]), kv.reshape(B, M, 2 * D),
                   bf(xattn_w_o[l])).reshape(T, D)
        x = _ffn(x, ffn2_norm_g[l], bf(ffn2_w_in[l]), bf(ffn2_w_out[l]),
                 final_norm_g if l == depth - 1 else None)
    return x.reshape(B, S, D)
```

```python
import functools
import math

import jax
import jax.numpy as jnp
from jax import lax
from jax.experimental import pallas as pl
from jax.experimental.pallas import tpu as pltpu

F32 = jnp.float32
BF16 = jnp.bfloat16
HIGHEST = lax.Precision.HIGHEST

HEAD_DIM = 64
RWKV_WIDTH = 512
MOBA_WIDTH = 512
MOBA_HEADS = MOBA_WIDTH // HEAD_DIM
DECAY_LORA = 64
ICLR_LORA = 64
GATE_LORA = 128
RWKV_PROJ = 3 * RWKV_WIDTH + DECAY_LORA + ICLR_LORA + GATE_LORA
MOBA_PROJ = 3 * MOBA_WIDTH
LNX_EPS = 64e-5
MOBA_BLOCK = 256
MOBA_TOPK = 3
REL_BUCKETS = 32
REL_MAX_DISTANCE = 1024
XATTN_HEADS = 4
FFN_RES_WEIGHT = 0.5
NORM_EPS = 1e-6
LOG2E = math.log2(math.e)

LANES = 128
MXU_DIM = 256
HEADS_PER_TILE = LANES // HEAD_DIM
HEAD_SHIFT = HEAD_DIM.bit_length() - 1
RWKV_CHUNK = 64
CHUNK_SHIFT = RWKV_CHUNK.bit_length() - 1
SPLIT_LORA = (1, 1, 1)
SPLIT_SUM = (2, 1, 2)
SPLIT_CUMSUM = (1, 2, 2)
SPLIT_CHUNK = (1, 1, 1)
VMEM_LIMIT = 56 * 1024 * 1024


def _rms(x, g):
    ms = jnp.mean(x * x, axis=-1, keepdims=True)
    return x * lax.rsqrt(ms + NORM_EPS) * g


def _dot(a, b, precision=None):
    return jnp.dot(a, b, precision=precision, preferred_element_type=F32)


def _dot_nt(a, b, precision=None):
    return lax.dot_general(a, b, (((1,), (1,)), ((), ())), precision=precision,
                           preferred_element_type=F32)


def _dot_tn(a, b, precision=None):
    return lax.dot_general(a, b, (((0,), (0,)), ((), ())), precision=precision,
                           preferred_element_type=F32)


def _bf16_terms(x, n):
    terms = []
    for i in range(n):
        t = x.astype(BF16)
        terms.append(t)
        if i + 1 < n:
            x = x - t.astype(F32)
    return terms


def _mm(dot, a, b, split):
    na, nb, order = split
    at, bt = _bf16_terms(a, na), _bf16_terms(b, nb)
    acc = None
    for i in range(na):
        for j in range(nb):
            if i + j < order:
                t = dot(at[i], bt[j])
                acc = t if acc is None else acc + t
    return acc


def _params(*semantics):
    return pltpu.CompilerParams(dimension_semantics=semantics, vmem_limit_bytes=VMEM_LIMIT)


def _ffn_kernel(x_ref, g_ref, wg_ref, wu_ref, wo_ref, *rest, final_norm):
    if final_norm:
        fg_ref, o_ref, h_ref, acc_ref = rest
    else:
        o_ref, h_ref, acc_ref = rest
    j = pl.program_id(1)

    @pl.when(j == 0)
    def _():
        h_ref[...] = _rms(x_ref[...], g_ref[...]).astype(BF16)
        acc_ref[...] = jnp.zeros_like(acc_ref)

    h = h_ref[...]
    gate = _dot(h, wg_ref[...])
    up = _dot(h, wu_ref[...])
    act = gate * jax.nn.sigmoid(gate) * up
    acc_ref[...] += _dot(act.astype(BF16), wo_ref[...])

    @pl.when(j == pl.num_programs(1) - 1)
    def _():
        y = x_ref[...] + FFN_RES_WEIGHT * acc_ref[...]
        if final_norm:
            y = _rms(y, fg_ref[...])
        o_ref[...] = y


def _ffn(x, g, w_in, w_out, final_g=None, *, tm=1024, tf=256):
    T, D = x.shape
    F = w_out.shape[0]
    tm = min(tm, T)
    nf = F // tf
    in_specs = [
        pl.BlockSpec((tm, D), lambda i, j: (i, 0)),
        pl.BlockSpec((1, D), lambda i, j: (0, 0)),
        pl.BlockSpec((D, tf), lambda i, j: (0, j)),
        pl.BlockSpec((D, tf), lambda i, j: (0, j + nf)),
        pl.BlockSpec((tf, D), lambda i, j: (j, 0)),
    ]
    args = [x, g.reshape(1, D), w_in, w_in, w_out]
    if final_g is not None:
        in_specs.append(pl.BlockSpec((1, D), lambda i, j: (0, 0)))
        args.append(final_g.reshape(1, D))
    return pl.pallas_call(
        functools.partial(_ffn_kernel, final_norm=final_g is not None),
        grid=(T // tm, nf),
        in_specs=in_specs,
        out_specs=pl.BlockSpec((tm, D), lambda i, j: (i, 0)),
        out_shape=jax.ShapeDtypeStruct((T, D), F32),
        scratch_shapes=[pltpu.VMEM((tm, D), BF16), pltpu.VMEM((tm, D), F32)],
        compiler_params=_params("parallel", "arbitrary"),
        name="ffn",
    )(*args)


def _norm_proj_kernel(x_ref, g_ref, w_ref, *o_refs, splits):
    h = _rms(x_ref[...], g_ref[...]).astype(BF16)
    off = 0
    for o_ref, n in zip(o_refs, splits):
        o_ref[...] = _dot(h, w_ref[:, off:off + n])
        off += n


def _norm_proj(x, g, w, splits, *, tm=512):
    T, D = x.shape
    N = w.shape[1]
    tm = min(tm, T)
    assert sum(splits) == N
    return pl.pallas_call(
        functools.partial(_norm_proj_kernel, splits=splits),
        grid=(T // tm,),
        in_specs=[
            pl.BlockSpec((tm, D), lambda i: (i, 0)),
            pl.BlockSpec((1, D), lambda i: (0, 0)),
            pl.BlockSpec((D, N), lambda i: (0, 0)),
        ],
        out_specs=[pl.BlockSpec((tm, n), lambda i: (i, 0)) for n in splits],
        out_shape=[jax.ShapeDtypeStruct((T, n), F32) for n in splits],
        compiler_params=_params("parallel"),
        name="norm_proj",
    )(x, g.reshape(1, D), w)


def _rwkv_kernel(p_ref, mu_ref, w0_ref, a0_ref, wwa_ref, gup_ref, kk_ref, ka_ref, rk_ref,
                 lng_ref, lnb_ref, o_ref, prev_ref, state_ref):
    C = RWKV_CHUNK
    W = RWKV_WIDTH
    NB, T, P = p_ref.shape
    R = NB * T
    s_idx = pl.program_id(1)

    @pl.when(s_idx == 0)
    def _():
        prev_ref[...] = jnp.zeros_like(prev_ref)
        state_ref[...] = jnp.zeros_like(state_ref)

    p = p_ref[...].reshape(R, P)
    row = lax.broadcasted_iota(jnp.int32, (R, 1), 0)
    shifted = pltpu.roll(p, 1, axis=0)
    for nb in range(NB):
        shifted = jnp.where(row == nb * T, prev_ref[nb], shifted)
        prev_ref[nb] = p[(nb + 1) * T - 1:(nb + 1) * T, :]
    p = p + (shifted - p) * mu_ref[...]

    r = p[:, 0:W]
    k = p[:, W:2 * W]
    v = p[:, 2 * W:3 * W]
    lora = p[:, 3 * W:3 * W + LANES]
    g_lo = p[:, 3 * W + LANES:3 * W + 2 * LANES]

    lane = lax.broadcasted_iota(jnp.int32, (1, LANES), 1)
    first_head = lane < HEAD_DIM
    z = jnp.where(first_head, jnp.tanh(lora), lora)
    wa = _mm(_dot, z, wwa_ref[...], SPLIT_LORA)
    logw = -math.exp(-0.5) * jax.nn.sigmoid(w0_ref[...] + wa[:, :W])
    a = jax.nn.sigmoid(a0_ref[...] + wa[:, W:])
    g = _mm(_dot, jax.nn.sigmoid(g_lo), gup_ref[...], SPLIT_LORA)

    kk = k * kk_ref[...]
    k2 = k * (1.0 + (a - 1.0) * ka_ref[...])

    ri = lax.broadcasted_iota(jnp.int32, (LANES, LANES), 0)
    ci = lax.broadcasted_iota(jnp.int32, (LANES, LANES), 1)
    same_head = (ri >> HEAD_SHIFT) == (ci >> HEAD_SHIFT)
    ri2 = lax.broadcasted_iota(jnp.int32, (MXU_DIM, MXU_DIM), 0)
    ci2 = lax.broadcasted_iota(jnp.int32, (MXU_DIM, MXU_DIM), 1)
    head_ones = ((ri2 >> HEAD_SHIFT) == (ci2 >> HEAD_SHIFT)).astype(F32)

    def head_sum(t):
        return jnp.concatenate(
            [_mm(_dot, t[:, i * MXU_DIM:(i + 1) * MXU_DIM], head_ones, SPLIT_SUM)
             for i in range(W // MXU_DIM)],
            axis=1)

    kkn = kk / jnp.maximum(jnp.sqrt(head_sum(kk * kk)), 1e-12)
    b = kkn * a
    bonus = head_sum(r * k2 * rk_ref[...]) * v

    tr = lax.broadcasted_iota(jnp.int32, (MXU_DIM, MXU_DIM), 0)
    tc = lax.broadcasted_iota(jnp.int32, (MXU_DIM, MXU_DIM), 1)
    in_chunk_prefix = ((tc <= tr) & ((tc >> CHUNK_SHIFT) == (tr >> CHUNK_SHIFT))).astype(F32)
    logp = jnp.concatenate(
        [_mm(_dot, in_chunk_prefix, logw[i * MXU_DIM:(i + 1) * MXU_DIM], SPLIT_CUMSUM)
         for i in range(R // MXU_DIM)], axis=0)
    r_t = r * jnp.exp(logp)
    a_t = -kkn * jnp.exp(logp - logw)
    inv = jnp.exp(-logp)
    b_t = b * inv
    k_t = k2 * inv

    tok = lax.broadcasted_iota(jnp.int32, (C, LANES), 0)
    col_tok = lax.broadcasted_iota(jnp.int32, (C, LANES), 1) & (C - 1)
    strict = col_tok < tok
    incl = col_tok <= tok
    incl2 = jnp.concatenate([incl, incl], axis=1)
    eye = (col_tok == tok).astype(F32)

    def stack(t):
        return jnp.concatenate([jnp.where(first_head, t, 0.0), jnp.where(first_head, 0.0, t)], axis=0)

    n_pairs = W // LANES
    n_chunks = T // C
    lanes = [slice(i * LANES, (i + 1) * LANES) for i in range(n_pairs)]
    units = [(q, i) for q in range(R // C) for i in range(n_pairs)]
    rows_of = lambda q: slice(q * C, (q + 1) * C)

    ar = {(q, i): jnp.concatenate([a_t[rows_of(q), lanes[i]], r_t[rows_of(q), lanes[i]]], axis=0)
          for q, i in units}
    bk = {(q, i): jnp.concatenate([stack(b_t[rows_of(q), lanes[i]]), stack(k_t[rows_of(q), lanes[i]])],
                                  axis=0) for q, i in units}
    v_s = {(q, i): stack(v[rows_of(q), lanes[i]]) for q, i in units}
    gram = {u_: _mm(_dot_nt, ar[u_], bk[u_], SPLIT_CHUNK) for u_ in units}
    l_ab = {u_: jnp.where(strict, gram[u_][:C, :LANES], 0.0) for u_ in units}
    a_ak = {u_: jnp.where(strict, gram[u_][:C, LANES:], 0.0) for u_ in units}
    a_r = {u_: jnp.where(incl2, gram[u_][C:], 0.0) for u_ in units}

    t_inv = {u_: eye + l_ab[u_] for u_ in units}
    l_pow = {u_: _mm(_dot, l_ab[u_], stack(l_ab[u_]), SPLIT_CHUNK) for u_ in units}
    levels = int(math.log2(C)) - 1
    for level in range(levels - 1):
        both = {u_: _mm(_dot, jnp.concatenate([t_inv[u_], l_pow[u_]], axis=0), stack(l_pow[u_]), SPLIT_CHUNK)
                for u_ in units}
        t_inv = {u_: t_inv[u_] + both[u_][:C] for u_ in units}
        l_pow = {u_: both[u_][C:] for u_ in units}
    t_inv = {u_: t_inv[u_] + _mm(_dot, t_inv[u_], stack(l_pow[u_]), SPLIT_CHUNK) for u_ in units}
    av = {u_: _mm(_dot, a_ak[u_], v_s[u_], SPLIT_CHUNK) for u_ in units}

    states = {(nb, i): state_ref[nb, i] for nb in range(NB) for i in range(n_pairs)}
    y_chunks = {}
    for c in range(n_chunks):
        now = [(nb, nb * n_chunks + c, i) for nb in range(NB) for i in range(n_pairs)]
        logp_end = {q: logp[(q + 1) * C - 1:(q + 1) * C, :] for _, q, _ in now}
        to_end = {q: jnp.exp(logp_end[q] - logp[rows_of(q)]) for q in logp_end}
        bk_e = {(q, i): jnp.concatenate([b[rows_of(q), lanes[i]] * to_end[q][:, lanes[i]],
                                         k2[rows_of(q), lanes[i]] * to_end[q][:, lanes[i]]], axis=0)
                for _, q, i in now}
        from_state = {(q, i): _mm(_dot_nt, ar[q, i], states[nb, i], SPLIT_CHUNK) for nb, q, i in now}
        u = {(q, i): _mm(_dot, t_inv[q, i], stack(from_state[q, i][:C] + av[q, i]), SPLIT_CHUNK)
             for _, q, i in now}
        outer = {(q, i): _mm(_dot_tn, jnp.concatenate([u[q, i], v[rows_of(q), lanes[i]]], axis=0),
                             bk_e[q, i], SPLIT_CHUNK) for _, q, i in now}
        for nb, q, i in now:
            y_chunks[q, i] = from_state[q, i][C:] + _mm(
                _dot, a_r[q, i], jnp.concatenate([stack(u[q, i]), v_s[q, i]], axis=0), SPLIT_CHUNK)
            states[nb, i] = (states[nb, i] * jnp.exp(logp_end[q][:, lanes[i]])
                             + jnp.where(same_head, outer[q, i], 0.0))
    for nb in range(NB):
        for i in range(n_pairs):
            state_ref[nb, i] = states[nb, i]

    y = jnp.concatenate([jnp.concatenate([y_chunks[q, i] for i in range(n_pairs)], axis=1)
                         for q in range(R // C)], axis=0)
    mean = head_sum(y) * (1.0 / HEAD_DIM)
    yc = y - mean
    var = head_sum(yc * yc) * (1.0 / HEAD_DIM)
    y = yc * lax.rsqrt(var + LNX_EPS) * lng_ref[...] + lnb_ref[...]
    o_ref[...] = ((y + bonus) * g).reshape(NB, T, W)


def _rwkv(p, mu, w0, w_up, a0, a_up, g_up, k_k, k_a, r_k, ln_g, ln_b, *, ts=256, nb=2):
    B, S, P = p.shape
    W = RWKV_WIDTH
    C = RWKV_CHUNK
    wwa = jnp.zeros((LANES, 2 * W), F32)
    wwa = wwa.at[:DECAY_LORA, :W].set(w_up).at[DECAY_LORA:, W:].set(a_up)
    vec = lambda t: t.reshape(1, -1)
    row_spec = lambda n: pl.BlockSpec((1, n), lambda b, s: (0, 0))
    ts = min(ts, S)
    nb = min(nb, B)
    assert ts % C == 0 and S % ts == 0 and B % nb == 0 and (nb * ts) % MXU_DIM == 0
    return pl.pallas_call(
        _rwkv_kernel,
        grid=(B // nb, S // ts),
        in_specs=[
            pl.BlockSpec((nb, ts, P), lambda b, s: (b, s, 0)),
            row_spec(P), row_spec(W), row_spec(W),
            pl.BlockSpec((LANES, 2 * W), lambda b, s: (0, 0)),
            pl.BlockSpec((GATE_LORA, W), lambda b, s: (0, 0)),
            row_spec(W), row_spec(W), row_spec(W), row_spec(W), row_spec(W),
        ],
        out_specs=pl.BlockSpec((nb, ts, W), lambda b, s: (b, s, 0)),
        out_shape=jax.ShapeDtypeStruct((B, S, W), F32),
        scratch_shapes=[pltpu.VMEM((nb, 1, P), F32),
                        pltpu.VMEM((nb, W // LANES, LANES, LANES), F32)],
        compiler_params=_params("parallel", "arbitrary"),
        name="rwkv7",
    )(p, vec(mu), vec(w0), vec(a0), wwa, g_up, vec(k_k), vec(k_a), vec(r_k), vec(ln_g), vec(ln_b))


def _rel_bias_kernel(tab_ref, o_ref, *, n_blocks):
    blk = MOBA_BLOCK
    c0 = pl.program_id(0) * blk
    kc = lax.broadcasted_iota(jnp.int32, (blk, blk), 0) + c0
    qi = lax.broadcasted_iota(jnp.int32, (blk, blk), 1)
    dist = qi + (n_blocks - 1) * blk - kc
    n = jnp.maximum(dist, 0)
    max_exact = REL_BUCKETS // 2
    nf = jnp.maximum(n, 1).astype(F32)
    large = max_exact + (jnp.log(nf / max_exact) / math.log(REL_MAX_DISTANCE / max_exact)
                         * (REL_BUCKETS - max_exact)).astype(jnp.int32)
    large = jnp.minimum(large, REL_BUCKETS - 1)
    bucket = jnp.where(n < max_exact, n, large)
    for h in range(MOBA_HEADS):
        tile = jnp.zeros((blk, blk), F32)
        for c in range(REL_BUCKETS):
            tile = jnp.where(bucket == c, tab_ref[h, c], tile)
        o_ref[h] = jnp.where(dist < 0, -jnp.inf, tile * LOG2E)


def _rel_bias_strip(rel_bias, n_blocks):
    blk = MOBA_BLOCK
    return pl.pallas_call(
        functools.partial(_rel_bias_kernel, n_blocks=n_blocks),
        grid=(n_blocks,),
        in_specs=[pl.BlockSpec(memory_space=pltpu.SMEM)],
        out_specs=pl.BlockSpec((MOBA_HEADS, blk, blk), lambda c: (0, c, 0)),
        out_shape=jax.ShapeDtypeStruct((MOBA_HEADS, n_blocks * blk, blk), F32),
        compiler_params=_params("parallel"),
        name="rel_bias_strip",
    )(rel_bias)


def _moba_kernel(q_ref, k_ref, v_ref, bias_ref, o_ref, *, n_blocks):
    blk = MOBA_BLOCK
    lane = lax.broadcasted_iota(jnp.int32, (1, LANES), 1)
    vrow = lax.broadcasted_iota(jnp.int32, (LANES, 1), 0)
    q = q_ref[0] * (HEAD_DIM ** -0.5 * LOG2E)
    k = k_ref[0]
    k16 = k.astype(BF16)
    vt = v_ref[0].T
    vt16 = [jnp.where((vrow >> HEAD_SHIFT) == e, vt, 1.0).astype(BF16) for e in range(HEADS_PER_TILE)]
    kmean = jnp.mean(k.reshape(n_blocks, blk, LANES), axis=1)

    gates, q16 = [], []
    for e in range(HEADS_PER_TILE):
        qe = jnp.where((lane >> HEAD_SHIFT) == e, q, 0.0)
        gates.append(_dot_nt(kmean, qe, HIGHEST))
        q16.append(qe.astype(BF16))

    def scores(qb, e):
        return _dot_nt(k16[:(qb + 1) * blk], q16[e][qb * blk:(qb + 1) * blk])

    def weights(qb, e, s_t):
        rows = slice(qb * blk, (qb + 1) * blk)
        bias_lo = (n_blocks - 1 - qb) * blk
        g = [gates[e][j:j + 1, rows] for j in range(qb)]
        tiles = []
        for j in range(qb + 1):
            t = s_t[j * blk:(j + 1) * blk] + bias_ref[e, bias_lo + j * blk:bias_lo + (j + 1) * blk, :]
            if j < qb:
                rank = jnp.zeros((1, blk), jnp.int32)
                for jj in range(qb):
                    if jj != j:
                        ahead = (g[jj] >= g[j]) if jj < j else (g[jj] > g[j])
                        rank = rank + jnp.where(ahead, 1, 0)
                t = jnp.where(rank < MOBA_TOPK, t, -jnp.inf)
            tiles.append(t)
        m = tiles[0].max(axis=0, keepdims=True)
        for t in tiles[1:]:
            m = jnp.maximum(m, t.max(axis=0, keepdims=True))
        return jnp.concatenate([jnp.exp2(t - m).astype(BF16) for t in tiles], axis=0)

    def attend(qb, e, p16):
        pv = _dot(vt16[e][:, :(qb + 1) * blk], p16)
        den_row = (1 - e) * HEAD_DIM
        return pv / pv[den_row:den_row + 1]

    units = [(qb, e) for qb in range(n_blocks) for e in range(HEADS_PER_TILE)]
    s_t, p16, out_t = {}, {}, {}
    for step in range(len(units) + 2):
        if step < len(units):
            s_t[units[step]] = scores(*units[step])
        if 0 <= step - 1 < len(units):
            u = units[step - 1]
            p16[u] = weights(*u, s_t.pop(u))
        if 0 <= step - 2 < len(units):
            u = units[step - 2]
            out_t[u] = attend(*u, p16.pop(u))
            qb, e = u
            if e == HEADS_PER_TILE - 1:
                o_ref[0, qb * blk:(qb + 1) * blk, :] = jnp.where(
                    vrow < HEAD_DIM, out_t.pop((qb, 0)), out_t.pop((qb, 1))).T


def _moba(qkv, bias_strip):
    B, S, _ = qkv.shape
    blk = MOBA_BLOCK
    n_blocks = S // blk
    n_pairs = MOBA_WIDTH // LANES
    return pl.pallas_call(
        functools.partial(_moba_kernel, n_blocks=n_blocks),
        grid=(n_pairs, B),
        in_specs=[
            pl.BlockSpec((1, S, LANES), lambda p, b: (b, 0, p)),
            pl.BlockSpec((1, S, LANES), lambda p, b: (b, 0, n_pairs + p)),
            pl.BlockSpec((1, S, LANES), lambda p, b: (b, 0, 2 * n_pairs + p)),
            pl.BlockSpec((HEADS_PER_TILE, S, blk), lambda p, b: (p, 0, 0)),
        ],
        out_specs=pl.BlockSpec((1, S, LANES), lambda p, b: (b, 0, p)),
        out_shape=jax.ShapeDtypeStruct((B, S, MOBA_WIDTH), F32),
        compiler_params=_params("parallel", "parallel"),
        name="moba",
    )(qkv, qkv, qkv, bias_strip)


def _xattn_kernel(x_ref, ya_ref, yb_ref, wa_ref, wb_ref, g_ref, wq_ref, kv_ref, wo_ref, o_ref):
    x = (x_ref[0] + _dot(ya_ref[0].astype(BF16), wa_ref[...])
         + _dot(yb_ref[0].astype(BF16), wb_ref[...]))
    D = x.shape[-1]
    dh = D // XATTN_HEADS
    h = _rms(x, g_ref[...]).astype(BF16)
    q = _dot(h, wq_ref[...])
    heads = []
    for i in range(XATTN_HEADS):
        qh = q[:, i * dh:(i + 1) * dh].astype(BF16)
        kh = kv_ref[0, :, i * dh:(i + 1) * dh].astype(BF16)
        vh = kv_ref[0, :, D + i * dh:D + (i + 1) * dh].astype(BF16)
        s = _dot_nt(qh, kh) * (dh ** -0.5)
        s = s - jnp.max(s, axis=-1, keepdims=True)
        pr = jnp.exp(s)
        pr = pr / jnp.sum(pr, axis=-1, keepdims=True)
        heads.append(_dot(pr.astype(BF16), vh))
    o = jnp.concatenate(heads, axis=1).astype(BF16)
    o_ref[0] = x + _dot(o, wo_ref[...])


def _xattn(x, ya, yb, w_mix, g, w_q, kv, w_o, *, tm=512):
    B, S, D = x.shape
    M = kv.shape[1]
    na, nb = ya.shape[-1], yb.shape[-1]
    tm = min(tm, S)
    return pl.pallas_call(
        _xattn_kernel,
        grid=(B, S // tm),
        in_specs=[
            pl.BlockSpec((1, tm, D), lambda b, i: (b, i, 0)),
            pl.BlockSpec((1, tm, na), lambda b, i: (b, i, 0)),
            pl.BlockSpec((1, tm, nb), lambda b, i: (b, i, 0)),
            pl.BlockSpec((na, D), lambda b, i: (0, 0)),
            pl.BlockSpec((nb, D), lambda b, i: (0, 0)),
            pl.BlockSpec((1, D), lambda b, i: (0, 0)),
            pl.BlockSpec((D, D), lambda b, i: (0, 0)),
            pl.BlockSpec((1, M, 2 * D), lambda b, i: (b, 0, 0)),
            pl.BlockSpec((D, D), lambda b, i: (0, 0)),
        ],
        out_specs=pl.BlockSpec((1, tm, D), lambda b, i: (b, i, 0)),
        out_shape=jax.ShapeDtypeStruct((B, S, D), F32),
        compiler_params=_params("parallel", "parallel"),
        name="xattn",
    )(x, ya, yb, w_mix[:na], w_mix[na:], g.reshape(1, D), w_q, kv, w_o)


def kernel(x, mem, rel_bias, final_norm_g, ffn1_norm_g, ffn1_w_in, ffn1_w_out, mix_norm_g, w_mix_in, w_mix_out, rwkv_mu, rwkv_w0, rwkv_w_up, rwkv_a0, rwkv_a_up, rwkv_g_up, rwkv_k_k, rwkv_k_a, rwkv_r_k, rwkv_ln_g, rwkv_ln_b, xattn_norm_g, mem_norm_g, xattn_w_q, xattn_w_kv, xattn_w_o, ffn2_norm_g, ffn2_w_in, ffn2_w_out):
    B, S, D = x.shape
    M = mem.shape[1]
    depth = ffn1_w_in.shape[0]
    T = B * S
    bf = lambda w: w.astype(BF16)

    bias_strip = _rel_bias_strip(rel_bias, S // MOBA_BLOCK)
    mem2 = mem.reshape(B * M, D)
    x = x.reshape(T, D)
    for l in range(depth):
        x = _ffn(x, ffn1_norm_g[l], bf(ffn1_w_in[l]), bf(ffn1_w_out[l]))
        p_rwkv, qkv = _norm_proj(x, mix_norm_g[l], bf(w_mix_in[l]), (RWKV_PROJ, MOBA_PROJ))
        y_rwkv = _rwkv(p_rwkv.reshape(B, S, RWKV_PROJ), rwkv_mu[l], rwkv_w0[l], rwkv_w_up[l],
                       rwkv_a0[l], rwkv_a_up[l], rwkv_g_up[l], rwkv_k_k[l], rwkv_k_a[l],
                       rwkv_r_k[l], rwkv_ln_g[l], rwkv_ln_b[l])
        y_moba = _moba(qkv.reshape(B, S, MOBA_PROJ), bias_strip)
        (kv,) = _norm_proj(mem2, mem_norm_g[l], bf(xattn_w_kv[l]), (2 * D,))
        x = _xattn(x.reshape(B, S, D), y_rwkv, y_moba, bf(w_mix_out[l]), xattn_norm_g[l],
                   bf(xattn_w_q[l]), kv.reshape(B, M, 2 * D), bf(xattn_w_o[l])).reshape(T, D)
        x = _ffn(x, ffn2_norm_g[l], bf(ffn2_w_in[l]), bf(ffn2_w_out[l]),
                 final_norm_g if l == depth - 1 else None)
    return x.reshape(B, S, D)
```

```python
import functools
import math

import jax
import jax.numpy as jnp
from jax import lax
from jax.experimental import pallas as pl
from jax.experimental.pallas import tpu as pltpu

F32 = jnp.float32
BF16 = jnp.bfloat16
HIGHEST = lax.Precision.HIGHEST

HEAD_DIM = 64
RWKV_WIDTH = 512
MOBA_WIDTH = 512
MOBA_HEADS = MOBA_WIDTH // HEAD_DIM
DECAY_LORA = 64
ICLR_LORA = 64
GATE_LORA = 128
RWKV_PROJ = 3 * RWKV_WIDTH + DECAY_LORA + ICLR_LORA + GATE_LORA
MOBA_PROJ = 3 * MOBA_WIDTH
LNX_EPS = 64e-5
MOBA_BLOCK = 256
MOBA_TOPK = 3
REL_BUCKETS = 32
REL_MAX_DISTANCE = 1024
XATTN_HEADS = 4
FFN_RES_WEIGHT = 0.5
NORM_EPS = 1e-6
LOG2E = math.log2(math.e)

LANES = 128
MXU_DIM = 256
HEADS_PER_TILE = LANES // HEAD_DIM
HEAD_SHIFT = HEAD_DIM.bit_length() - 1
RWKV_CHUNK = 64
CHUNK_SHIFT = RWKV_CHUNK.bit_length() - 1
SPLIT_LORA = (1, 1, 1)
SPLIT_SUM = (1, 1, 1)
SPLIT_CUMSUM = (1, 2, 2)
SPLIT_CHUNK = (1, 1, 1)
VMEM_LIMIT = 56 * 1024 * 1024


def _rms(x, g):
    ms = jnp.mean(x * x, axis=-1, keepdims=True)
    return x * lax.rsqrt(ms + NORM_EPS) * g


def _dot(a, b, precision=None):
    return jnp.dot(a, b, precision=precision, preferred_element_type=F32)


def _dot_nt(a, b, precision=None):
    return lax.dot_general(a, b, (((1,), (1,)), ((), ())), precision=precision,
                           preferred_element_type=F32)


def _dot_tn(a, b, precision=None):
    return lax.dot_general(a, b, (((0,), (0,)), ((), ())), precision=precision,
                           preferred_element_type=F32)


def _bf16_terms(x, n):
    terms = []
    for i in range(n):
        t = x.astype(BF16)
        terms.append(t)
        if i + 1 < n:
            x = x - t.astype(F32)
    return terms


def _mm(dot, a, b, split):
    na, nb, order = split
    at, bt = _bf16_terms(a, na), _bf16_terms(b, nb)
    acc = None
    for i in range(na):
        for j in range(nb):
            if i + j < order:
                t = dot(at[i], bt[j])
                acc = t if acc is None else acc + t
    return acc


def _params(*semantics):
    return pltpu.CompilerParams(dimension_semantics=semantics, vmem_limit_bytes=VMEM_LIMIT)


def _ffn_kernel(x_ref, g_ref, wi_ref, wo_ref, *rest, final_norm, sub):
    if final_norm:
        fg_ref, o_ref = rest
    else:
        (o_ref,) = rest
    F = wo_ref.shape[0]
    for r0 in range(0, x_ref.shape[0], sub):
        x = x_ref[r0:r0 + sub, :]
        h = _rms(x, g_ref[...]).astype(BF16)
        gate = _dot(h, wi_ref[:, :F])
        up = _dot(h, wi_ref[:, F:])
        act = (gate * jax.nn.sigmoid(gate) * up).astype(BF16)
        y = x + FFN_RES_WEIGHT * _dot(act, wo_ref[...])
        if final_norm:
            y = _rms(y, fg_ref[...])
        o_ref[r0:r0 + sub, :] = y


def _ffn(x, g, w_in, w_out, final_g=None, *, tm=512, sub=256):
    T, D = x.shape
    F = w_out.shape[0]
    tm = min(tm, T)
    resident = lambda shape: pl.BlockSpec(shape, lambda i: (0, 0), pipeline_mode=pl.Buffered(1))
    in_specs = [
        pl.BlockSpec((tm, D), lambda i: (i, 0)),
        resident((1, D)),
        resident((D, 2 * F)),
        resident((F, D)),
    ]
    args = [x, g.reshape(1, D), w_in, w_out]
    if final_g is not None:
        in_specs.append(resident((1, D)))
        args.append(final_g.reshape(1, D))
    return pl.pallas_call(
        functools.partial(_ffn_kernel, final_norm=final_g is not None, sub=min(sub, tm)),
        grid=(T // tm,),
        in_specs=in_specs,
        out_specs=pl.BlockSpec((tm, D), lambda i: (i, 0)),
        out_shape=jax.ShapeDtypeStruct((T, D), F32),
        compiler_params=_params("parallel"),
        name="ffn",
    )(*args)


def _norm_proj_kernel(x_ref, g_ref, w_ref, *rest, splits, tiles_per_seq):
    if tiles_per_seq:
        mu_ref, *o_refs, prev_ref = rest
    else:
        o_refs = rest
    h = _rms(x_ref[...], g_ref[...]).astype(BF16)
    off = 0
    for idx, (o_ref, n) in enumerate(zip(o_refs, splits)):
        y = _dot(h, w_ref[:, off:off + n])
        if tiles_per_seq and idx == 0:
            tm = y.shape[0]
            @pl.when(pl.program_id(0) % tiles_per_seq == 0)
            def _():
                prev_ref[...] = jnp.zeros_like(prev_ref)

            before = prev_ref[...]
            prev_ref[...] = y[tm - 1:tm, :]
            row = lax.broadcasted_iota(jnp.int32, (tm, 1), 0)
            shifted = jnp.where(row == 0, before, pltpu.roll(y, 1, axis=0))
            y = y + (shifted - y) * mu_ref[...]
        o_ref[...] = y
        off += n


def _norm_proj(x, g, w, splits, *, shift_mu=None, seq_len=None, tm=512):
    T, D = x.shape
    N = w.shape[1]
    tm = min(tm, T)
    assert sum(splits) == N
    in_specs = [
        pl.BlockSpec((tm, D), lambda i: (i, 0)),
        pl.BlockSpec((1, D), lambda i: (0, 0)),
        pl.BlockSpec((D, N), lambda i: (0, 0)),
    ]
    args = [x, g.reshape(1, D), w]
    scratch, tiles_per_seq = [], 0
    if shift_mu is not None:
        assert seq_len % tm == 0
        tiles_per_seq = seq_len // tm
        in_specs.append(pl.BlockSpec((1, splits[0]), lambda i: (0, 0)))
        args.append(shift_mu.reshape(1, splits[0]))
        scratch = [pltpu.VMEM((1, splits[0]), F32)]
    return pl.pallas_call(
        functools.partial(_norm_proj_kernel, splits=splits, tiles_per_seq=tiles_per_seq),
        grid=(T // tm,),
        in_specs=in_specs,
        out_specs=[pl.BlockSpec((tm, n), lambda i: (i, 0)) for n in splits],
        out_shape=[jax.ShapeDtypeStruct((T, n), F32) for n in splits],
        scratch_shapes=scratch,
        compiler_params=_params("arbitrary" if tiles_per_seq else "parallel"),
        name="norm_proj",
    )(*args)


def _rwkv_kernel(p_ref, w0_ref, a0_ref, wwa_ref, gup_ref, kk_ref, ka_ref, rk_ref,
                 lng_ref, lnb_ref, o_ref, state_ref):
    C = RWKV_CHUNK
    W = RWKV_WIDTH
    NB, T, P = p_ref.shape
    R = NB * T
    s_idx = pl.program_id(1)

    @pl.when(s_idx == 0)
    def _():
        state_ref[...] = jnp.zeros_like(state_ref)

    p = p_ref[...].reshape(R, P)

    r = p[:, 0:W]
    k = p[:, W:2 * W]
    v = p[:, 2 * W:3 * W]
    lora = p[:, 3 * W:3 * W + LANES]
    g_lo = p[:, 3 * W + LANES:3 * W + 2 * LANES]

    lane = lax.broadcasted_iota(jnp.int32, (1, LANES), 1)
    first_head = lane < HEAD_DIM
    z = jnp.where(first_head, jnp.tanh(lora), lora)
    wa = _mm(_dot, z, wwa_ref[...], SPLIT_LORA)
    logw = -math.exp(-0.5) * jax.nn.sigmoid(w0_ref[...] + wa[:, :W])
    a = jax.nn.sigmoid(a0_ref[...] + wa[:, W:])
    g = _mm(_dot, jax.nn.sigmoid(g_lo), gup_ref[...], SPLIT_LORA)

    kk = k * kk_ref[...]
    k2 = k * (1.0 + (a - 1.0) * ka_ref[...])

    ri = lax.broadcasted_iota(jnp.int32, (LANES, LANES), 0)
    ci = lax.broadcasted_iota(jnp.int32, (LANES, LANES), 1)
    same_head = (ri >> HEAD_SHIFT) == (ci >> HEAD_SHIFT)
    ri2 = lax.broadcasted_iota(jnp.int32, (MXU_DIM, MXU_DIM), 0)
    ci2 = lax.broadcasted_iota(jnp.int32, (MXU_DIM, MXU_DIM), 1)
    head_ones = ((ri2 >> HEAD_SHIFT) == (ci2 >> HEAD_SHIFT)).astype(F32)

    def head_sum(t):
        return jnp.concatenate(
            [_mm(_dot, t[:, i * MXU_DIM:(i + 1) * MXU_DIM], head_ones, SPLIT_SUM)
             for i in range(W // MXU_DIM)],
            axis=1)

    kkn = kk / jnp.maximum(jnp.sqrt(head_sum(kk * kk)), 1e-12)
    b = kkn * a
    bonus = head_sum(r * k2 * rk_ref[...]) * v

    tr = lax.broadcasted_iota(jnp.int32, (MXU_DIM, MXU_DIM), 0)
    tc = lax.broadcasted_iota(jnp.int32, (MXU_DIM, MXU_DIM), 1)
    in_chunk_prefix = ((tc <= tr) & ((tc >> CHUNK_SHIFT) == (tr >> CHUNK_SHIFT))).astype(F32)
    logp = jnp.concatenate(
        [_mm(_dot, in_chunk_prefix, logw[i * MXU_DIM:(i + 1) * MXU_DIM], SPLIT_CUMSUM)
         for i in range(R // MXU_DIM)], axis=0)
    r_t = r * jnp.exp(logp)
    a_t = -kkn * jnp.exp(logp - logw)
    inv = jnp.exp(-logp)
    b_t = b * inv
    k_t = k2 * inv

    tok = lax.broadcasted_iota(jnp.int32, (C, LANES), 0)
    col_tok = lax.broadcasted_iota(jnp.int32, (C, LANES), 1) & (C - 1)
    strict = col_tok < tok
    incl = col_tok <= tok
    incl2 = jnp.concatenate([incl, incl], axis=1)
    eye = (col_tok == tok).astype(F32)

    def stack(t):
        zero = jnp.zeros_like(t)
        return jnp.concatenate([jnp.where(first_head, t, zero), jnp.where(first_head, zero, t)], axis=0)

    n_pairs = W // LANES
    n_chunks = T // C
    lanes = [slice(i * LANES, (i + 1) * LANES) for i in range(n_pairs)]
    units = [(q, i) for q in range(R // C) for i in range(n_pairs)]
    rows_of = lambda q: slice(q * C, (q + 1) * C)
    c16 = lambda t: t.astype(BF16)
    a16, r16, b16, k16, v16 = c16(a_t), c16(r_t), c16(b_t), c16(k_t), c16(v)

    ar = {(q, i): jnp.concatenate([a16[rows_of(q), lanes[i]], r16[rows_of(q), lanes[i]]], axis=0)
          for q, i in units}
    bk = {(q, i): jnp.concatenate([stack(b16[rows_of(q), lanes[i]]), stack(k16[rows_of(q), lanes[i]])],
                                  axis=0) for q, i in units}
    v_s = {(q, i): stack(v16[rows_of(q), lanes[i]]) for q, i in units}
    gram = {u_: _dot_nt(ar[u_], bk[u_]) for u_ in units}
    l_ab = {u_: jnp.where(strict, gram[u_][:C, :LANES], 0.0) for u_ in units}
    a_ak = {u_: c16(jnp.where(strict, gram[u_][:C, LANES:], 0.0)) for u_ in units}
    a_r = {u_: c16(jnp.where(incl2, gram[u_][C:], 0.0)) for u_ in units}

    t_inv = {u_: eye + l_ab[u_] for u_ in units}
    lp16 = {u_: c16(l_ab[u_]) for u_ in units}
    l_pow = {u_: _dot(lp16[u_], stack(lp16[u_])) for u_ in units}
    levels = int(math.log2(C)) - 1
    for level in range(levels - 1):
        lp16 = {u_: c16(l_pow[u_]) for u_ in units}
        both = {u_: _dot(jnp.concatenate([c16(t_inv[u_]), lp16[u_]], axis=0), stack(lp16[u_]))
                for u_ in units}
        t_inv = {u_: t_inv[u_] + both[u_][:C] for u_ in units}
        l_pow = {u_: both[u_][C:] for u_ in units}
    t16 = {u_: c16(t_inv[u_] + _dot(c16(t_inv[u_]), stack(c16(l_pow[u_])))) for u_ in units}
    av = {u_: _dot(a_ak[u_], v_s[u_]) for u_ in units}

    states = {(nb, i): state_ref[nb, i] for nb in range(NB) for i in range(n_pairs)}
    y_chunks = {}
    for c in range(n_chunks):
        now = [(nb, nb * n_chunks + c, i) for nb in range(NB) for i in range(n_pairs)]
        logp_end = {q: logp[(q + 1) * C - 1:(q + 1) * C, :] for _, q, _ in now}
        to_end = {q: jnp.exp(logp_end[q] - logp[rows_of(q)]) for q in logp_end}
        bk_e = {(q, i): c16(jnp.concatenate([b[rows_of(q), lanes[i]] * to_end[q][:, lanes[i]],
                                             k2[rows_of(q), lanes[i]] * to_end[q][:, lanes[i]]], axis=0))
                for _, q, i in now}
        from_state = {(q, i): _dot_nt(ar[q, i], c16(states[nb, i])) for nb, q, i in now}
        u16 = {(q, i): c16(_dot(t16[q, i], stack(c16(from_state[q, i][:C] + av[q, i]))))
               for _, q, i in now}
        outer = {(q, i): _dot_tn(jnp.concatenate([u16[q, i], v16[rows_of(q), lanes[i]]], axis=0), bk_e[q, i])
                 for _, q, i in now}
        for nb, q, i in now:
            y_chunks[q, i] = from_state[q, i][C:] + _dot(
                a_r[q, i], jnp.concatenate([stack(u16[q, i]), v_s[q, i]], axis=0))
            states[nb, i] = (states[nb, i] * jnp.exp(logp_end[q][:, lanes[i]])
                             + jnp.where(same_head, outer[q, i], 0.0))
    for nb in range(NB):
        for i in range(n_pairs):
            state_ref[nb, i] = states[nb, i]

    y = jnp.concatenate([jnp.concatenate([y_chunks[q, i] for i in range(n_pairs)], axis=1)
                         for q in range(R // C)], axis=0)
    mean = head_sum(y) * (1.0 / HEAD_DIM)
    yc = y - mean
    var = head_sum(yc * yc) * (1.0 / HEAD_DIM)
    y = yc * lax.rsqrt(var + LNX_EPS) * lng_ref[...] + lnb_ref[...]
    o_ref[...] = ((y + bonus) * g).reshape(NB, T, W)


def _rwkv(p, w0, w_up, a0, a_up, g_up, k_k, k_a, r_k, ln_g, ln_b, *, ts=256, nb=2):
    B, S, P = p.shape
    W = RWKV_WIDTH
    C = RWKV_CHUNK
    wwa = jnp.zeros((LANES, 2 * W), F32)
    wwa = wwa.at[:DECAY_LORA, :W].set(w_up).at[DECAY_LORA:, W:].set(a_up)
    vec = lambda t: t.reshape(1, -1)
    row_spec = lambda n: pl.BlockSpec((1, n), lambda b, s: (0, 0))
    ts = min(ts, S)
    nb = min(nb, B)
    assert ts % C == 0 and S % ts == 0 and B % nb == 0 and (nb * ts) % MXU_DIM == 0
    return pl.pallas_call(
        _rwkv_kernel,
        grid=(B // nb, S // ts),
        in_specs=[
            pl.BlockSpec((nb, ts, P), lambda b, s: (b, s, 0)),
            row_spec(W), row_spec(W),
            pl.BlockSpec((LANES, 2 * W), lambda b, s: (0, 0)),
            pl.BlockSpec((GATE_LORA, W), lambda b, s: (0, 0)),
            row_spec(W), row_spec(W), row_spec(W), row_spec(W), row_spec(W),
        ],
        out_specs=pl.BlockSpec((nb, ts, W), lambda b, s: (b, s, 0)),
        out_shape=jax.ShapeDtypeStruct((B, S, W), F32),
        scratch_shapes=[pltpu.VMEM((nb, W // LANES, LANES, LANES), F32)],
        compiler_params=_params("parallel", "arbitrary"),
        name="rwkv7",
    )(p, vec(w0), vec(a0), wwa, g_up, vec(k_k), vec(k_a), vec(r_k), vec(ln_g), vec(ln_b))


def _rel_bias_kernel(tab_ref, o_ref, *, n_blocks):
    blk = MOBA_BLOCK
    c0 = pl.program_id(0) * blk
    kc = lax.broadcasted_iota(jnp.int32, (blk, blk), 0) + c0
    qi = lax.broadcasted_iota(jnp.int32, (blk, blk), 1)
    dist = qi + (n_blocks - 1) * blk - kc
    n = jnp.maximum(dist, 0)
    max_exact = REL_BUCKETS // 2
    nf = jnp.maximum(n, 1).astype(F32)
    large = max_exact + (jnp.log(nf / max_exact) / math.log(REL_MAX_DISTANCE / max_exact)
                         * (REL_BUCKETS - max_exact)).astype(jnp.int32)
    large = jnp.minimum(large, REL_BUCKETS - 1)
    bucket = jnp.where(n < max_exact, n, large)
    for h in range(MOBA_HEADS):
        tile = jnp.zeros((blk, blk), F32)
        for c in range(REL_BUCKETS):
            tile = jnp.where(bucket == c, tab_ref[h, c], tile)
        o_ref[h] = jnp.where(dist < 0, -jnp.inf, tile * LOG2E)


def _rel_bias_strip(rel_bias, n_blocks):
    blk = MOBA_BLOCK
    return pl.pallas_call(
        functools.partial(_rel_bias_kernel, n_blocks=n_blocks),
        grid=(n_blocks,),
        in_specs=[pl.BlockSpec(memory_space=pltpu.SMEM)],
        out_specs=pl.BlockSpec((MOBA_HEADS, blk, blk), lambda c: (0, c, 0)),
        out_shape=jax.ShapeDtypeStruct((MOBA_HEADS, n_blocks * blk, blk), F32),
        compiler_params=_params("parallel"),
        name="rel_bias_strip",
    )(rel_bias)


def _moba_kernel(q_ref, k_ref, v_ref, bias_ref, o_ref, *, n_blocks):
    blk = MOBA_BLOCK
    lane = lax.broadcasted_iota(jnp.int32, (1, LANES), 1)
    vrow = lax.broadcasted_iota(jnp.int32, (LANES, 1), 0)
    q = q_ref[0] * (HEAD_DIM ** -0.5 * LOG2E)
    k = k_ref[0]
    k16 = k.astype(BF16)
    vt = v_ref[0].T
    vt16 = [jnp.where((vrow >> HEAD_SHIFT) == e, vt, 1.0).astype(BF16) for e in range(HEADS_PER_TILE)]
    kmean = jnp.mean(k.reshape(n_blocks, blk, LANES), axis=1)

    gates, q16 = [], []
    for e in range(HEADS_PER_TILE):
        qe = jnp.where((lane >> HEAD_SHIFT) == e, q, 0.0)
        gates.append(_dot_nt(kmean, qe, HIGHEST))
        q16.append(qe.astype(BF16))

    def scores(qb, e):
        return _dot_nt(k16[:(qb + 1) * blk], q16[e][qb * blk:(qb + 1) * blk])

    def weights(qb, e, s_t):
        rows = slice(qb * blk, (qb + 1) * blk)
        bias_lo = (n_blocks - 1 - qb) * blk
        g = [gates[e][j:j + 1, rows] for j in range(qb)]
        tiles = []
        for j in range(qb + 1):
            t = s_t[j * blk:(j + 1) * blk] + bias_ref[e, bias_lo + j * blk:bias_lo + (j + 1) * blk, :]
            if j < qb:
                rank = jnp.zeros((1, blk), jnp.int32)
                for jj in range(qb):
                    if jj != j:
                        ahead = (g[jj] >= g[j]) if jj < j else (g[jj] > g[j])
                        rank = rank + jnp.where(ahead, 1, 0)
                t = jnp.where(rank < MOBA_TOPK, t, -jnp.inf)
            tiles.append(t)
        m = tiles[0].max(axis=0, keepdims=True)
        for t in tiles[1:]:
            m = jnp.maximum(m, t.max(axis=0, keepdims=True))
        return jnp.concatenate([jnp.exp2(t - m).astype(BF16) for t in tiles], axis=0)

    def attend(qb, e, p16):
        pv = _dot(vt16[e][:, :(qb + 1) * blk], p16)
        den_row = (1 - e) * HEAD_DIM
        return pv / pv[den_row:den_row + 1]

    units = [(qb, e) for qb in range(n_blocks) for e in range(HEADS_PER_TILE)]
    s_t, p16, out_t = {}, {}, {}
    for step in range(len(units) + 2):
        if step < len(units):
            s_t[units[step]] = scores(*units[step])
        if 0 <= step - 1 < len(units):
            u = units[step - 1]
            p16[u] = weights(*u, s_t.pop(u))
        if 0 <= step - 2 < len(units):
            u = units[step - 2]
            out_t[u] = attend(*u, p16.pop(u))
            qb, e = u
            if e == HEADS_PER_TILE - 1:
                o_ref[0, qb * blk:(qb + 1) * blk, :] = jnp.where(
                    vrow < HEAD_DIM, out_t.pop((qb, 0)), out_t.pop((qb, 1))).T


def _moba(qkv, bias_strip):
    B, S, _ = qkv.shape
    blk = MOBA_BLOCK
    n_blocks = S // blk
    n_pairs = MOBA_WIDTH // LANES
    return pl.pallas_call(
        functools.partial(_moba_kernel, n_blocks=n_blocks),
        grid=(n_pairs, B),
        in_specs=[
            pl.BlockSpec((1, S, LANES), lambda p, b: (b, 0, p)),
            pl.BlockSpec((1, S, LANES), lambda p, b: (b, 0, n_pairs + p)),
            pl.BlockSpec((1, S, LANES), lambda p, b: (b, 0, 2 * n_pairs + p)),
            pl.BlockSpec((HEADS_PER_TILE, S, blk), lambda p, b: (p, 0, 0)),
        ],
        out_specs=pl.BlockSpec((1, S, LANES), lambda p, b: (b, 0, p)),
        out_shape=jax.ShapeDtypeStruct((B, S, MOBA_WIDTH), F32),
        compiler_params=_params("parallel", "parallel"),
        name="moba",
    )(qkv, qkv, qkv, bias_strip)


def _xattn_kernel(x_ref, ya_ref, yb_ref, wa_ref, wb_ref, g_ref, wq_ref, kv_ref, wo_ref, o_ref):
    x = (x_ref[0] + _dot(ya_ref[0].astype(BF16), wa_ref[...])
         + _dot(yb_ref[0].astype(BF16), wb_ref[...]))
    D = x.shape[-1]
    dh = D // XATTN_HEADS
    h = _rms(x, g_ref[...]).astype(BF16)
    q = _dot(h, wq_ref[...])
    heads = []
    for i in range(XATTN_HEADS):
        qh = q[:, i * dh:(i + 1) * dh].astype(BF16)
        kh = kv_ref[0, :, i * dh:(i + 1) * dh].astype(BF16)
        vh = kv_ref[0, :, D + i * dh:D + (i + 1) * dh].astype(BF16)
        s = _dot_nt(qh, kh) * (dh ** -0.5)
        s = s - jnp.max(s, axis=-1, keepdims=True)
        pr = jnp.exp(s)
        pr = pr / jnp.sum(pr, axis=-1, keepdims=True)
        heads.append(_dot(pr.astype(BF16), vh))
    o = jnp.concatenate(heads, axis=1).astype(BF16)
    o_ref[0] = x + _dot(o, wo_ref[...])


def _xattn(x, ya, yb, w_mix, g, w_q, kv, w_o, *, tm=512):
    B, S, D = x.shape
    M = kv.shape[1]
    na, nb = ya.shape[-1], yb.shape[-1]
    tm = min(tm, S)
    return pl.pallas_call(
        _xattn_kernel,
        grid=(B, S // tm),
        in_specs=[
            pl.BlockSpec((1, tm, D), lambda b, i: (b, i, 0)),
            pl.BlockSpec((1, tm, na), lambda b, i: (b, i, 0)),
            pl.BlockSpec((1, tm, nb), lambda b, i: (b, i, 0)),
            pl.BlockSpec((na, D), lambda b, i: (0, 0)),
            pl.BlockSpec((nb, D), lambda b, i: (0, 0)),
            pl.BlockSpec((1, D), lambda b, i: (0, 0)),
            pl.BlockSpec((D, D), lambda b, i: (0, 0)),
            pl.BlockSpec((1, M, 2 * D), lambda b, i: (b, 0, 0)),
            pl.BlockSpec((D, D), lambda b, i: (0, 0)),
        ],
        out_specs=pl.BlockSpec((1, tm, D), lambda b, i: (b, i, 0)),
        out_shape=jax.ShapeDtypeStruct((B, S, D), F32),
        compiler_params=_params("parallel", "parallel"),
        name="xattn",
    )(x, ya, yb, w_mix[:na], w_mix[na:], g.reshape(1, D), w_q, kv, w_o)


def kernel(x, mem, rel_bias, final_norm_g, ffn1_norm_g, ffn1_w_in, ffn1_w_out, mix_norm_g, w_mix_in, w_mix_out, rwkv_mu, rwkv_w0, rwkv_w_up, rwkv_a0, rwkv_a_up, rwkv_g_up, rwkv_k_k, rwkv_k_a, rwkv_r_k, rwkv_ln_g, rwkv_ln_b, xattn_norm_g, mem_norm_g, xattn_w_q, xattn_w_kv, xattn_w_o, ffn2_norm_g, ffn2_w_in, ffn2_w_out):
    B, S, D = x.shape
    M = mem.shape[1]
    depth = ffn1_w_in.shape[0]
    T = B * S
    bf = lambda w: w.astype(BF16)

    bias_strip = _rel_bias_strip(rel_bias, S // MOBA_BLOCK)
    mem2 = mem.reshape(B * M, D)
    x = x.reshape(T, D)
    for l in range(depth):
        x = _ffn(x, ffn1_norm_g[l], bf(ffn1_w_in[l]), bf(ffn1_w_out[l]))
        p_rwkv, qkv = _norm_proj(x, mix_norm_g[l], bf(w_mix_in[l]), (RWKV_PROJ, MOBA_PROJ),
                                 shift_mu=rwkv_mu[l], seq_len=S)
        y_rwkv = _rwkv(p_rwkv.reshape(B, S, RWKV_PROJ), rwkv_w0[l], rwkv_w_up[l],
                       rwkv_a0[l], rwkv_a_up[l], rwkv_g_up[l], rwkv_k_k[l], rwkv_k_a[l],
                       rwkv_r_k[l], rwkv_ln_g[l], rwkv_ln_b[l])
        y_moba = _moba(qkv.reshape(B, S, MOBA_PROJ), bias_strip)
        (kv,) = _norm_proj(mem2, mem_norm_g[l], bf(xattn_w_kv[l]), (2 * D,))
        x = _xattn(x.reshape(B, S, D), y_rwkv, y_moba, bf(w_mix_out[l]), xattn_norm_g[l],
                   bf(xattn_w_q[l]), kv.reshape(B, M, 2 * D), bf(xattn_w_o[l])).reshape(T, D)
        x = _ffn(x, ffn2_norm_g[l], bf(ffn2_w_in[l]), bf(ffn2_w_out[l]),
                 final_norm_g if l == depth - 1 else None)
    return x.reshape(B, S, D)
```

```python
import functools
import math

import jax
import jax.numpy as jnp
from jax import lax
from jax.experimental import pallas as pl
from jax.experimental.pallas import tpu as pltpu

F32 = jnp.float32
BF16 = jnp.bfloat16
HIGHEST = lax.Precision.HIGHEST

HEAD_DIM = 64
RWKV_WIDTH = 512
MOBA_WIDTH = 512
MOBA_HEADS = MOBA_WIDTH // HEAD_DIM
DECAY_LORA = 64
ICLR_LORA = 64
GATE_LORA = 128
RWKV_PROJ = 3 * RWKV_WIDTH + DECAY_LORA + ICLR_LORA + GATE_LORA
MOBA_PROJ = 3 * MOBA_WIDTH
LNX_EPS = 64e-5
MOBA_BLOCK = 256
MOBA_TOPK = 3
REL_BUCKETS = 32
REL_MAX_DISTANCE = 1024
XATTN_HEADS = 4
FFN_RES_WEIGHT = 0.5
NORM_EPS = 1e-6
LOG2E = math.log2(math.e)

LANES = 128
MXU_DIM = 256
HEADS_PER_TILE = LANES // HEAD_DIM
HEAD_SHIFT = HEAD_DIM.bit_length() - 1
RWKV_CHUNK = 64
CHUNK_SHIFT = RWKV_CHUNK.bit_length() - 1
SPLIT_LORA = (1, 1, 1)
SPLIT_SUM = (1, 1, 1)
SPLIT_CUMSUM = (1, 2, 2)
SPLIT_CHUNK = (1, 1, 1)
VMEM_LIMIT = 56 * 1024 * 1024


def _rms(x, g):
    ms = jnp.mean(x * x, axis=-1, keepdims=True)
    return x * lax.rsqrt(ms + NORM_EPS) * g


def _dot(a, b, precision=None):
    return jnp.dot(a, b, precision=precision, preferred_element_type=F32)


def _dot_nt(a, b, precision=None):
    return lax.dot_general(a, b, (((1,), (1,)), ((), ())), precision=precision,
                           preferred_element_type=F32)


def _dot_tn(a, b, precision=None):
    return lax.dot_general(a, b, (((0,), (0,)), ((), ())), precision=precision,
                           preferred_element_type=F32)


def _bf16_terms(x, n):
    terms = []
    for i in range(n):
        t = x.astype(BF16)
        terms.append(t)
        if i + 1 < n:
            x = x - t.astype(F32)
    return terms


def _mm(dot, a, b, split):
    na, nb, order = split
    at, bt = _bf16_terms(a, na), _bf16_terms(b, nb)
    acc = None
    for i in range(na):
        for j in range(nb):
            if i + j < order:
                t = dot(at[i], bt[j])
                acc = t if acc is None else acc + t
    return acc


def _params(*semantics):
    return pltpu.CompilerParams(dimension_semantics=semantics, vmem_limit_bytes=VMEM_LIMIT)


def _ffn_kernel(x_ref, g_ref, wi_ref, wo_ref, *rest, final_norm, sub):
    if final_norm:
        fg_ref, o_ref = rest
    else:
        (o_ref,) = rest
    F = wo_ref.shape[0]
    for r0 in range(0, x_ref.shape[0], sub):
        x = x_ref[r0:r0 + sub, :]
        h = _rms(x, g_ref[...]).astype(BF16)
        gate = _dot(h, wi_ref[:, :F])
        up = _dot(h, wi_ref[:, F:])
        act = (gate * jax.nn.sigmoid(gate) * up).astype(BF16)
        y = x + FFN_RES_WEIGHT * _dot(act, wo_ref[...])
        if final_norm:
            y = _rms(y, fg_ref[...])
        o_ref[r0:r0 + sub, :] = y


def _ffn(x, g, w_in, w_out, final_g=None, *, tm=512, sub=256):
    T, D = x.shape
    F = w_out.shape[0]
    tm = min(tm, T)
    resident = lambda shape: pl.BlockSpec(shape, lambda i: (0, 0), pipeline_mode=pl.Buffered(1))
    in_specs = [
        pl.BlockSpec((tm, D), lambda i: (i, 0)),
        resident((1, D)),
        resident((D, 2 * F)),
        resident((F, D)),
    ]
    args = [x, g.reshape(1, D), w_in, w_out]
    if final_g is not None:
        in_specs.append(resident((1, D)))
        args.append(final_g.reshape(1, D))
    return pl.pallas_call(
        functools.partial(_ffn_kernel, final_norm=final_g is not None, sub=min(sub, tm)),
        grid=(T // tm,),
        in_specs=in_specs,
        out_specs=pl.BlockSpec((tm, D), lambda i: (i, 0)),
        out_shape=jax.ShapeDtypeStruct((T, D), F32),
        compiler_params=_params("parallel"),
        name="ffn",
    )(*args)


def _norm_proj_kernel(x_ref, g_ref, w_ref, *rest, splits, tiles_per_seq, sub):
    if tiles_per_seq:
        mu_ref, *o_refs, prev_ref = rest

        @pl.when(pl.program_id(0) % tiles_per_seq == 0)
        def _():
            prev_ref[...] = jnp.zeros_like(prev_ref)

        before = prev_ref[...]
    else:
        o_refs = rest
    row = lax.broadcasted_iota(jnp.int32, (sub, 1), 0)
    for r0 in range(0, x_ref.shape[0], sub):
        h = _rms(x_ref[r0:r0 + sub, :], g_ref[...]).astype(BF16)
        off = 0
        for idx, (o_ref, n) in enumerate(zip(o_refs, splits)):
            y = _dot(h, w_ref[:, off:off + n])
            if tiles_per_seq and idx == 0:
                shifted = jnp.where(row == 0, before, pltpu.roll(y, 1, axis=0))
                before = y[sub - 1:sub, :]
                y = y + (shifted - y) * mu_ref[...]
            o_ref[r0:r0 + sub, :] = y
            off += n
    if tiles_per_seq:
        prev_ref[...] = before


def _norm_proj(x, g, w, splits, *, shift_mu=None, seq_len=None, tm=512, sub=128):
    T, D = x.shape
    N = w.shape[1]
    tm = min(tm, T)
    assert sum(splits) == N
    in_specs = [
        pl.BlockSpec((tm, D), lambda i: (i, 0)),
        pl.BlockSpec((1, D), lambda i: (0, 0)),
        pl.BlockSpec((D, N), lambda i: (0, 0)),
    ]
    args = [x, g.reshape(1, D), w]
    scratch, tiles_per_seq = [], 0
    if shift_mu is not None:
        assert seq_len % tm == 0
        tiles_per_seq = seq_len // tm
        in_specs.append(pl.BlockSpec((1, splits[0]), lambda i: (0, 0)))
        args.append(shift_mu.reshape(1, splits[0]))
        scratch = [pltpu.VMEM((1, splits[0]), F32)]
    return pl.pallas_call(
        functools.partial(_norm_proj_kernel, splits=splits, tiles_per_seq=tiles_per_seq,
                          sub=min(sub, tm)),
        grid=(T // tm,),
        in_specs=in_specs,
        out_specs=[pl.BlockSpec((tm, n), lambda i: (i, 0)) for n in splits],
        out_shape=[jax.ShapeDtypeStruct((T, n), F32) for n in splits],
        scratch_shapes=scratch,
        compiler_params=_params("arbitrary" if tiles_per_seq else "parallel"),
        name="norm_proj",
    )(*args)


def _rwkv_kernel(p_ref, w0_ref, a0_ref, wwa_ref, gup_ref, kk_ref, ka_ref, rk_ref,
                 lng_ref, lnb_ref, o_ref, state_ref):
    C = RWKV_CHUNK
    W = RWKV_WIDTH
    NB, T, P = p_ref.shape
    R = NB * T
    s_idx = pl.program_id(1)

    @pl.when(s_idx == 0)
    def _():
        state_ref[...] = jnp.zeros_like(state_ref)

    p = p_ref[...].reshape(R, P)

    r = p[:, 0:W]
    k = p[:, W:2 * W]
    v = p[:, 2 * W:3 * W]
    lora = p[:, 3 * W:3 * W + LANES]
    g_lo = p[:, 3 * W + LANES:3 * W + 2 * LANES]

    lane = lax.broadcasted_iota(jnp.int32, (1, LANES), 1)
    first_head = lane < HEAD_DIM
    z = jnp.where(first_head, jnp.tanh(lora), lora)
    wa = _mm(_dot, z, wwa_ref[...], SPLIT_LORA)
    logw = -math.exp(-0.5) * jax.nn.sigmoid(w0_ref[...] + wa[:, :W])
    a = jax.nn.sigmoid(a0_ref[...] + wa[:, W:])
    g = _mm(_dot, jax.nn.sigmoid(g_lo), gup_ref[...], SPLIT_LORA)

    kk = k * kk_ref[...]
    k2 = k * (1.0 + (a - 1.0) * ka_ref[...])

    ri = lax.broadcasted_iota(jnp.int32, (LANES, LANES), 0)
    ci = lax.broadcasted_iota(jnp.int32, (LANES, LANES), 1)
    same_head = (ri >> HEAD_SHIFT) == (ci >> HEAD_SHIFT)
    ri2 = lax.broadcasted_iota(jnp.int32, (MXU_DIM, MXU_DIM), 0)
    ci2 = lax.broadcasted_iota(jnp.int32, (MXU_DIM, MXU_DIM), 1)
    head_ones = ((ri2 >> HEAD_SHIFT) == (ci2 >> HEAD_SHIFT)).astype(F32)

    def head_sum(t):
        return jnp.concatenate(
            [_mm(_dot, t[:, i * MXU_DIM:(i + 1) * MXU_DIM], head_ones, SPLIT_SUM)
             for i in range(W // MXU_DIM)],
            axis=1)

    kkn = kk / jnp.maximum(jnp.sqrt(head_sum(kk * kk)), 1e-12)
    b = kkn * a
    bonus = head_sum(r * k2 * rk_ref[...]) * v

    tr = lax.broadcasted_iota(jnp.int32, (MXU_DIM, MXU_DIM), 0)
    tc = lax.broadcasted_iota(jnp.int32, (MXU_DIM, MXU_DIM), 1)
    in_chunk_prefix = ((tc <= tr) & ((tc >> CHUNK_SHIFT) == (tr >> CHUNK_SHIFT))).astype(F32)
    logp = jnp.concatenate(
        [_mm(_dot, in_chunk_prefix, logw[i * MXU_DIM:(i + 1) * MXU_DIM], SPLIT_CUMSUM)
         for i in range(R // MXU_DIM)], axis=0)
    r_t = r * jnp.exp(logp)
    a_t = -kkn * jnp.exp(logp - logw)
    inv = jnp.exp(-logp)
    b_t = b * inv
    k_t = k2 * inv

    tok = lax.broadcasted_iota(jnp.int32, (C, LANES), 0)
    col_tok = lax.broadcasted_iota(jnp.int32, (C, LANES), 1) & (C - 1)
    strict = col_tok < tok
    incl = col_tok <= tok
    incl2 = jnp.concatenate([incl, incl], axis=1)
    eye = (col_tok == tok).astype(F32)

    def stack(t):
        zero = jnp.zeros_like(t)
        return jnp.concatenate([jnp.where(first_head, t, zero), jnp.where(first_head, zero, t)], axis=0)

    n_pairs = W // LANES
    n_chunks = T // C
    lanes = [slice(i * LANES, (i + 1) * LANES) for i in range(n_pairs)]
    units = [(q, i) for q in range(R // C) for i in range(n_pairs)]
    rows_of = lambda q: slice(q * C, (q + 1) * C)
    c16 = lambda t: t.astype(BF16)
    a16, r16, b16, k16, v16 = c16(a_t), c16(r_t), c16(b_t), c16(k_t), c16(v)

    ar = {(q, i): jnp.concatenate([a16[rows_of(q), lanes[i]], r16[rows_of(q), lanes[i]]], axis=0)
          for q, i in units}
    bk = {(q, i): jnp.concatenate([stack(b16[rows_of(q), lanes[i]]), stack(k16[rows_of(q), lanes[i]])],
                                  axis=0) for q, i in units}
    v_s = {(q, i): stack(v16[rows_of(q), lanes[i]]) for q, i in units}
    gram = {u_: _dot_nt(ar[u_], bk[u_]) for u_ in units}
    l_ab = {u_: jnp.where(strict, gram[u_][:C, :LANES], 0.0) for u_ in units}
    a_ak = {u_: c16(jnp.where(strict, gram[u_][:C, LANES:], 0.0)) for u_ in units}
    a_r = {u_: c16(jnp.where(incl2, gram[u_][C:], 0.0)) for u_ in units}

    t_inv = {u_: eye + l_ab[u_] for u_ in units}
    lp16 = {u_: c16(l_ab[u_]) for u_ in units}
    l_pow = {u_: _dot(lp16[u_], stack(lp16[u_])) for u_ in units}
    levels = int(math.log2(C)) - 1
    for level in range(levels - 1):
        lp16 = {u_: c16(l_pow[u_]) for u_ in units}
        both = {u_: _dot(jnp.concatenate([c16(t_inv[u_]), lp16[u_]], axis=0), stack(lp16[u_]))
                for u_ in units}
        t_inv = {u_: t_inv[u_] + both[u_][:C] for u_ in units}
        l_pow = {u_: both[u_][C:] for u_ in units}
    t16 = {u_: c16(t_inv[u_] + _dot(c16(t_inv[u_]), stack(c16(l_pow[u_])))) for u_ in units}
    av = {u_: _dot(a_ak[u_], v_s[u_]) for u_ in units}

    states = {(nb, i): state_ref[nb, i] for nb in range(NB) for i in range(n_pairs)}
    y_chunks = {}
    for c in range(n_chunks):
        now = [(nb, nb * n_chunks + c, i) for nb in range(NB) for i in range(n_pairs)]
        logp_end = {q: logp[(q + 1) * C - 1:(q + 1) * C, :] for _, q, _ in now}
        to_end = {q: jnp.exp(logp_end[q] - logp[rows_of(q)]) for q in logp_end}
        bk_e = {(q, i): c16(jnp.concatenate([b[rows_of(q), lanes[i]] * to_end[q][:, lanes[i]],
                                             k2[rows_of(q), lanes[i]] * to_end[q][:, lanes[i]]], axis=0))
                for _, q, i in now}
        from_state = {(q, i): _dot_nt(ar[q, i], c16(states[nb, i])) for nb, q, i in now}
        u16 = {(q, i): c16(_dot(t16[q, i], stack(c16(from_state[q, i][:C] + av[q, i]))))
               for _, q, i in now}
        outer = {(q, i): _dot_tn(jnp.concatenate([u16[q, i], v16[rows_of(q), lanes[i]]], axis=0), bk_e[q, i])
                 for _, q, i in now}
        for nb, q, i in now:
            y_chunks[q, i] = from_state[q, i][C:] + _dot(
                a_r[q, i], jnp.concatenate([stack(u16[q, i]), v_s[q, i]], axis=0))
            states[nb, i] = (states[nb, i] * jnp.exp(logp_end[q][:, lanes[i]])
                             + jnp.where(same_head, outer[q, i], 0.0))
    for nb in range(NB):
        for i in range(n_pairs):
            state_ref[nb, i] = states[nb, i]

    y = jnp.concatenate([jnp.concatenate([y_chunks[q, i] for i in range(n_pairs)], axis=1)
                         for q in range(R // C)], axis=0)
    mean = head_sum(y) * (1.0 / HEAD_DIM)
    yc = y - mean
    var = head_sum(yc * yc) * (1.0 / HEAD_DIM)
    y = yc * lax.rsqrt(var + LNX_EPS) * lng_ref[...] + lnb_ref[...]
    o_ref[...] = ((y + bonus) * g).reshape(NB, T, W)


def _rwkv(p, w0, w_up, a0, a_up, g_up, k_k, k_a, r_k, ln_g, ln_b, *, ts=256, nb=2):
    B, S, P = p.shape
    W = RWKV_WIDTH
    C = RWKV_CHUNK
    wwa = jnp.zeros((LANES, 2 * W), F32)
    wwa = wwa.at[:DECAY_LORA, :W].set(w_up).at[DECAY_LORA:, W:].set(a_up)
    vec = lambda t: t.reshape(1, -1)
    row_spec = lambda n: pl.BlockSpec((1, n), lambda b, s: (0, 0))
    ts = min(ts, S)
    nb = min(nb, B)
    assert ts % C == 0 and S % ts == 0 and B % nb == 0 and (nb * ts) % MXU_DIM == 0
    return pl.pallas_call(
        _rwkv_kernel,
        grid=(B // nb, S // ts),
        in_specs=[
            pl.BlockSpec((nb, ts, P), lambda b, s: (b, s, 0)),
            row_spec(W), row_spec(W),
            pl.BlockSpec((LANES, 2 * W), lambda b, s: (0, 0)),
            pl.BlockSpec((GATE_LORA, W), lambda b, s: (0, 0)),
            row_spec(W), row_spec(W), row_spec(W), row_spec(W), row_spec(W),
        ],
        out_specs=pl.BlockSpec((nb, ts, W), lambda b, s: (b, s, 0)),
        out_shape=jax.ShapeDtypeStruct((B, S, W), F32),
        scratch_shapes=[pltpu.VMEM((nb, W // LANES, LANES, LANES), F32)],
        compiler_params=_params("parallel", "arbitrary"),
        name="rwkv7",
    )(p, vec(w0), vec(a0), wwa, g_up, vec(k_k), vec(k_a), vec(r_k), vec(ln_g), vec(ln_b))


def _rel_bias_kernel(tab_ref, o_ref, *, n_blocks):
    blk = MOBA_BLOCK
    c0 = pl.program_id(0) * blk
    kc = lax.broadcasted_iota(jnp.int32, (blk, blk), 0) + c0
    qi = lax.broadcasted_iota(jnp.int32, (blk, blk), 1)
    dist = qi + (n_blocks - 1) * blk - kc
    n = jnp.maximum(dist, 0)
    max_exact = REL_BUCKETS // 2
    nf = jnp.maximum(n, 1).astype(F32)
    large = max_exact + (jnp.log(nf / max_exact) / math.log(REL_MAX_DISTANCE / max_exact)
                         * (REL_BUCKETS - max_exact)).astype(jnp.int32)
    large = jnp.minimum(large, REL_BUCKETS - 1)
    bucket = jnp.where(n < max_exact, n, large)
    for h in range(MOBA_HEADS):
        tile = jnp.zeros((blk, blk), F32)
        for c in range(REL_BUCKETS):
            tile = jnp.where(bucket == c, tab_ref[h, c], tile)
        o_ref[h] = jnp.where(dist < 0, -jnp.inf, tile * LOG2E)


def _rel_bias_strip(rel_bias, n_blocks):
    blk = MOBA_BLOCK
    return pl.pallas_call(
        functools.partial(_rel_bias_kernel, n_blocks=n_blocks),
        grid=(n_blocks,),
        in_specs=[pl.BlockSpec(memory_space=pltpu.SMEM)],
        out_specs=pl.BlockSpec((MOBA_HEADS, blk, blk), lambda c: (0, c, 0)),
        out_shape=jax.ShapeDtypeStruct((MOBA_HEADS, n_blocks * blk, blk), F32),
        compiler_params=_params("parallel"),
        name="rel_bias_strip",
    )(rel_bias)


def _moba_kernel(q_ref, k_ref, v_ref, bias_ref, o_ref, *, n_blocks):
    blk = MOBA_BLOCK
    lane = lax.broadcasted_iota(jnp.int32, (1, LANES), 1)
    vrow = lax.broadcasted_iota(jnp.int32, (LANES, 1), 0)
    q = q_ref[0] * (HEAD_DIM ** -0.5 * LOG2E)
    k = k_ref[0]
    k16 = k.astype(BF16)
    vt = v_ref[0].T
    vt16 = [jnp.where((vrow >> HEAD_SHIFT) == e, vt, 1.0).astype(BF16) for e in range(HEADS_PER_TILE)]
    kmean = jnp.mean(k.reshape(n_blocks, blk, LANES), axis=1)

    gates, q16 = [], []
    for e in range(HEADS_PER_TILE):
        qe = jnp.where((lane >> HEAD_SHIFT) == e, q, 0.0)
        gates.append(_dot_nt(kmean, qe, HIGHEST))
        q16.append(qe.astype(BF16))

    def scores(qb, e):
        return _dot_nt(k16[:(qb + 1) * blk], q16[e][qb * blk:(qb + 1) * blk])

    def weights(qb, e, s_t):
        rows = slice(qb * blk, (qb + 1) * blk)
        bias_lo = (n_blocks - 1 - qb) * blk
        g = [gates[e][j:j + 1, rows] for j in range(qb)]
        tiles = []
        for j in range(qb + 1):
            t = s_t[j * blk:(j + 1) * blk] + bias_ref[e, bias_lo + j * blk:bias_lo + (j + 1) * blk, :]
            if j < qb:
                rank = jnp.zeros((1, blk), jnp.int32)
                for jj in range(qb):
                    if jj != j:
                        ahead = (g[jj] >= g[j]) if jj < j else (g[jj] > g[j])
                        rank = rank + jnp.where(ahead, 1, 0)
                t = jnp.where(rank < MOBA_TOPK, t, -jnp.inf)
            tiles.append(t)
        m = tiles[0].max(axis=0, keepdims=True)
        for t in tiles[1:]:
            m = jnp.maximum(m, t.max(axis=0, keepdims=True))
        return jnp.concatenate([jnp.exp2(t - m).astype(BF16) for t in tiles], axis=0)

    def attend(qb, e, p16):
        pv = _dot(vt16[e][:, :(qb + 1) * blk], p16)
        den_row = (1 - e) * HEAD_DIM
        return pv / pv[den_row:den_row + 1]

    units = [(qb, e) for qb in range(n_blocks) for e in range(HEADS_PER_TILE)]
    s_t, p16, out_t = {}, {}, {}
    for step in range(len(units) + 2):
        if step < len(units):
            s_t[units[step]] = scores(*units[step])
        if 0 <= step - 1 < len(units):
            u = units[step - 1]
            p16[u] = weights(*u, s_t.pop(u))
        if 0 <= step - 2 < len(units):
            u = units[step - 2]
            out_t[u] = attend(*u, p16.pop(u))
            qb, e = u
            if e == HEADS_PER_TILE - 1:
                o_ref[0, qb * blk:(qb + 1) * blk, :] = jnp.where(
                    vrow < HEAD_DIM, out_t.pop((qb, 0)), out_t.pop((qb, 1))).T


def _moba(qkv, bias_strip):
    B, S, _ = qkv.shape
    blk = MOBA_BLOCK
    n_blocks = S // blk
    n_pairs = MOBA_WIDTH // LANES
    return pl.pallas_call(
        functools.partial(_moba_kernel, n_blocks=n_blocks),
        grid=(n_pairs, B),
        in_specs=[
            pl.BlockSpec((1, S, LANES), lambda p, b: (b, 0, p)),
            pl.BlockSpec((1, S, LANES), lambda p, b: (b, 0, n_pairs + p)),
            pl.BlockSpec((1, S, LANES), lambda p, b: (b, 0, 2 * n_pairs + p)),
            pl.BlockSpec((HEADS_PER_TILE, S, blk), lambda p, b: (p, 0, 0)),
        ],
        out_specs=pl.BlockSpec((1, S, LANES), lambda p, b: (b, 0, p)),
        out_shape=jax.ShapeDtypeStruct((B, S, MOBA_WIDTH), F32),
        compiler_params=_params("parallel", "parallel"),
        name="moba",
    )(qkv, qkv, qkv, bias_strip)


def _xattn_kernel(x_ref, ya_ref, yb_ref, wa_ref, wb_ref, g_ref, wq_ref, kv_ref, wo_ref, o_ref, *, sub):
    D = x_ref.shape[-1]
    dh = D // XATTN_HEADS
    k16 = [kv_ref[0, :, i * dh:(i + 1) * dh].astype(BF16) for i in range(XATTN_HEADS)]
    v16 = [kv_ref[0, :, D + i * dh:D + (i + 1) * dh].astype(BF16) for i in range(XATTN_HEADS)]

    def project(r0):
        rows = slice(r0, r0 + sub)
        x = (x_ref[0, rows, :] + _dot(ya_ref[0, rows, :].astype(BF16), wa_ref[...])
             + _dot(yb_ref[0, rows, :].astype(BF16), wb_ref[...]))
        h = _rms(x, g_ref[...]).astype(BF16)
        q = (_dot(h, wq_ref[...]) * (dh ** -0.5 * LOG2E)).astype(BF16)
        return x, q

    def attend(q):
        hs = range(XATTN_HEADS)
        s = [_dot_nt(q[:, i * dh:(i + 1) * dh], k16[i]) for i in hs]
        pr = [jnp.exp2(s[i] - jnp.max(s[i], axis=-1, keepdims=True)) for i in hs]
        pr = [(pr[i] / jnp.sum(pr[i], axis=-1, keepdims=True)).astype(BF16) for i in hs]
        return jnp.concatenate([_dot(pr[i], v16[i]) for i in hs], axis=1).astype(BF16)

    starts = list(range(0, x_ref.shape[1], sub))
    xq, att = {}, {}
    for step in range(len(starts) + 2):
        if step < len(starts):
            xq[starts[step]] = project(starts[step])
        if 0 <= step - 1 < len(starts):
            r0 = starts[step - 1]
            att[r0] = attend(xq[r0][1])
        if 0 <= step - 2 < len(starts):
            r0 = starts[step - 2]
            o_ref[0, r0:r0 + sub, :] = xq.pop(r0)[0] + _dot(att.pop(r0), wo_ref[...])


def _xattn(x, ya, yb, w_mix, g, w_q, kv, w_o, *, tm=512, sub=512):
    B, S, D = x.shape
    M = kv.shape[1]
    na, nb = ya.shape[-1], yb.shape[-1]
    tm = min(tm, S)
    return pl.pallas_call(
        functools.partial(_xattn_kernel, sub=min(sub, tm)),
        grid=(B, S // tm),
        in_specs=[
            pl.BlockSpec((1, tm, D), lambda b, i: (b, i, 0)),
            pl.BlockSpec((1, tm, na), lambda b, i: (b, i, 0)),
            pl.BlockSpec((1, tm, nb), lambda b, i: (b, i, 0)),
            pl.BlockSpec((na, D), lambda b, i: (0, 0)),
            pl.BlockSpec((nb, D), lambda b, i: (0, 0)),
            pl.BlockSpec((1, D), lambda b, i: (0, 0)),
            pl.BlockSpec((D, D), lambda b, i: (0, 0)),
            pl.BlockSpec((1, M, 2 * D), lambda b, i: (b, 0, 0)),
            pl.BlockSpec((D, D), lambda b, i: (0, 0)),
        ],
        out_specs=pl.BlockSpec((1, tm, D), lambda b, i: (b, i, 0)),
        out_shape=jax.ShapeDtypeStruct((B, S, D), F32),
        compiler_params=_params("parallel", "parallel"),
        name="xattn",
    )(x, ya, yb, w_mix[:na], w_mix[na:], g.reshape(1, D), w_q, kv, w_o)


def kernel(x, mem, rel_bias, final_norm_g, ffn1_norm_g, ffn1_w_in, ffn1_w_out, mix_norm_g, w_mix_in, w_mix_out, rwkv_mu, rwkv_w0, rwkv_w_up, rwkv_a0, rwkv_a_up, rwkv_g_up, rwkv_k_k, rwkv_k_a, rwkv_r_k, rwkv_ln_g, rwkv_ln_b, xattn_norm_g, mem_norm_g, xattn_w_q, xattn_w_kv, xattn_w_o, ffn2_norm_g, ffn2_w_in, ffn2_w_out):
    B, S, D = x.shape
    M = mem.shape[1]
    depth = ffn1_w_in.shape[0]
    T = B * S
    bf = lambda w: w.astype(BF16)

    bias_strip = _rel_bias_strip(rel_bias, S // MOBA_BLOCK)
    mem2 = mem.reshape(B * M, D)
    x = x.reshape(T, D)
    for l in range(depth):
        x = _ffn(x, ffn1_norm_g[l], bf(ffn1_w_in[l]), bf(ffn1_w_out[l]))
        p_rwkv, qkv = _norm_proj(x, mix_norm_g[l], bf(w_mix_in[l]), (RWKV_PROJ, MOBA_PROJ),
                                 shift_mu=rwkv_mu[l], seq_len=S)
        y_rwkv = _rwkv(p_rwkv.reshape(B, S, RWKV_PROJ), rwkv_w0[l], rwkv_w_up[l],
                       rwkv_a0[l], rwkv_a_up[l], rwkv_g_up[l], rwkv_k_k[l], rwkv_k_a[l],
                       rwkv_r_k[l], rwkv_ln_g[l], rwkv_ln_b[l])
        y_moba = _moba(qkv.reshape(B, S, MOBA_PROJ), bias_strip)
        (kv,) = _norm_proj(mem2, mem_norm_g[l], bf(xattn_w_kv[l]), (2 * D,))
        x = _xattn(x.reshape(B, S, D), y_rwkv, y_moba, bf(w_mix_out[l]), xattn_norm_g[l],
                   bf(xattn_w_q[l]), kv.reshape(B, M, 2 * D), bf(xattn_w_o[l])).reshape(T, D)
        x = _ffn(x, ffn2_norm_g[l], bf(ffn2_w_in[l]), bf(ffn2_w_out[l]),
                 final_norm_g if l == depth - 1 else None)
    return x.reshape(B, S, D)
```

```python
import functools
import math

import jax
import jax.numpy as jnp
from jax import lax
from jax.experimental import pallas as pl
from jax.experimental.pallas import tpu as pltpu

F32 = jnp.float32
BF16 = jnp.bfloat16
HIGHEST = lax.Precision.HIGHEST

HEAD_DIM = 64
RWKV_WIDTH = 512
MOBA_WIDTH = 512
MOBA_HEADS = MOBA_WIDTH // HEAD_DIM
DECAY_LORA = 64
ICLR_LORA = 64
GATE_LORA = 128
RWKV_PROJ = 3 * RWKV_WIDTH + DECAY_LORA + ICLR_LORA + GATE_LORA
MOBA_PROJ = 3 * MOBA_WIDTH
LNX_EPS = 64e-5
MOBA_BLOCK = 256
MOBA_TOPK = 3
REL_BUCKETS = 32
REL_MAX_DISTANCE = 1024
XATTN_HEADS = 4
FFN_RES_WEIGHT = 0.5
NORM_EPS = 1e-6
LOG2E = math.log2(math.e)

LANES = 128
MXU_DIM = 256
HEADS_PER_TILE = LANES // HEAD_DIM
HEAD_SHIFT = HEAD_DIM.bit_length() - 1
RWKV_CHUNK = 64
CHUNK_SHIFT = RWKV_CHUNK.bit_length() - 1
SPLIT_LORA = (1, 1, 1)
SPLIT_SUM = (1, 1, 1)
SPLIT_CUMSUM = (1, 2, 2)
SPLIT_CHUNK = (1, 1, 1)
VMEM_LIMIT = 56 * 1024 * 1024


def _rms(x, g):
    ms = jnp.mean(x * x, axis=-1, keepdims=True)
    return x * lax.rsqrt(ms + NORM_EPS) * g


def _dot(a, b, precision=None):
    return jnp.dot(a, b, precision=precision, preferred_element_type=F32)


def _dot_nt(a, b, precision=None):
    return lax.dot_general(a, b, (((1,), (1,)), ((), ())), precision=precision,
                           preferred_element_type=F32)


def _dot_tn(a, b, precision=None):
    return lax.dot_general(a, b, (((0,), (0,)), ((), ())), precision=precision,
                           preferred_element_type=F32)


def _bf16_terms(x, n):
    terms = []
    for i in range(n):
        t = x.astype(BF16)
        terms.append(t)
        if i + 1 < n:
            x = x - t.astype(F32)
    return terms


def _mm(dot, a, b, split):
    na, nb, order = split
    at, bt = _bf16_terms(a, na), _bf16_terms(b, nb)
    acc = None
    for i in range(na):
        for j in range(nb):
            if i + j < order:
                t = dot(at[i], bt[j])
                acc = t if acc is None else acc + t
    return acc


def _params(*semantics):
    return pltpu.CompilerParams(dimension_semantics=semantics, vmem_limit_bytes=VMEM_LIMIT)


def _ffn_kernel(x_ref, g_ref, wi_ref, wo_ref, *rest, final_norm, sub):
    if final_norm:
        fg_ref, o_ref = rest
    else:
        (o_ref,) = rest
    F = wo_ref.shape[0]
    for r0 in range(0, x_ref.shape[0], sub):
        x = x_ref[r0:r0 + sub, :]
        h = _rms(x, g_ref[...]).astype(BF16)
        gate = _dot(h, wi_ref[:, :F])
        up = _dot(h, wi_ref[:, F:])
        act = (gate * jax.nn.sigmoid(gate) * up).astype(BF16)
        y = x + FFN_RES_WEIGHT * _dot(act, wo_ref[...])
        if final_norm:
            y = _rms(y, fg_ref[...])
        o_ref[r0:r0 + sub, :] = y


def _ffn(x, g, w_in, w_out, layer, final_g=None, *, tm=512, sub=256):
    T, D = x.shape
    F = w_out.shape[1]
    tm = min(tm, T)
    resident = lambda shape: pl.BlockSpec(shape, lambda i: (0, 0), pipeline_mode=pl.Buffered(1))
    of_layer = lambda shape: pl.BlockSpec((None,) + shape, lambda i: (layer, 0, 0),
                                          pipeline_mode=pl.Buffered(1))
    in_specs = [
        pl.BlockSpec((tm, D), lambda i: (i, 0)),
        resident((1, D)),
        of_layer((D, 2 * F)),
        of_layer((F, D)),
    ]
    args = [x, g.reshape(1, D), w_in, w_out]
    if final_g is not None:
        in_specs.append(resident((1, D)))
        args.append(final_g.reshape(1, D))
    return pl.pallas_call(
        functools.partial(_ffn_kernel, final_norm=final_g is not None, sub=min(sub, tm)),
        grid=(T // tm,),
        in_specs=in_specs,
        out_specs=pl.BlockSpec((tm, D), lambda i: (i, 0)),
        out_shape=jax.ShapeDtypeStruct((T, D), F32),
        compiler_params=_params("parallel"),
        name="ffn",
    )(*args)


def _norm_proj_kernel(x_ref, g_ref, w_ref, *rest, splits, tiles_per_seq, sub):
    if tiles_per_seq:
        mu_ref, *o_refs, prev_ref = rest

        @pl.when(pl.program_id(0) % tiles_per_seq == 0)
        def _():
            prev_ref[...] = jnp.zeros_like(prev_ref)

        before = prev_ref[...]
    else:
        o_refs = rest
    row = lax.broadcasted_iota(jnp.int32, (sub, 1), 0)
    for r0 in range(0, x_ref.shape[0], sub):
        h = _rms(x_ref[r0:r0 + sub, :], g_ref[...]).astype(BF16)
        off = 0
        for idx, (o_ref, n) in enumerate(zip(o_refs, splits)):
            y = _dot(h, w_ref[:, off:off + n])
            if tiles_per_seq and idx == 0:
                shifted = jnp.where(row == 0, before, pltpu.roll(y, 1, axis=0))
                before = y[sub - 1:sub, :]
                y = y + (shifted - y) * mu_ref[...]
            o_ref[r0:r0 + sub, :] = y
            off += n
    if tiles_per_seq:
        prev_ref[...] = before


def _norm_proj(x, g, w, layer, splits, *, shift_mu=None, seq_len=None, tm=512, sub=128):
    T, D = x.shape
    N = w.shape[2]
    tm = min(tm, T)
    assert sum(splits) == N
    in_specs = [
        pl.BlockSpec((tm, D), lambda i: (i, 0)),
        pl.BlockSpec((1, D), lambda i: (0, 0)),
        pl.BlockSpec((None, D, N), lambda i: (layer, 0, 0)),
    ]
    args = [x, g.reshape(1, D), w]
    scratch, tiles_per_seq = [], 0
    if shift_mu is not None:
        assert seq_len % tm == 0
        tiles_per_seq = seq_len // tm
        in_specs.append(pl.BlockSpec((1, splits[0]), lambda i: (0, 0)))
        args.append(shift_mu.reshape(1, splits[0]))
        scratch = [pltpu.VMEM((1, splits[0]), F32)]
    return pl.pallas_call(
        functools.partial(_norm_proj_kernel, splits=splits, tiles_per_seq=tiles_per_seq,
                          sub=min(sub, tm)),
        grid=(T // tm,),
        in_specs=in_specs,
        out_specs=[pl.BlockSpec((tm, n), lambda i: (i, 0)) for n in splits],
        out_shape=[jax.ShapeDtypeStruct((T, n), F32) for n in splits],
        scratch_shapes=scratch,
        compiler_params=_params("arbitrary" if tiles_per_seq else "parallel"),
        name="norm_proj",
    )(*args)


def _rwkv_kernel(p_ref, w0_ref, a0_ref, wwa_ref, gup_ref, kk_ref, ka_ref, rk_ref,
                 lng_ref, lnb_ref, o_ref, state_ref):
    C = RWKV_CHUNK
    W = RWKV_WIDTH
    NB, T, P = p_ref.shape
    R = NB * T
    s_idx = pl.program_id(1)

    @pl.when(s_idx == 0)
    def _():
        state_ref[...] = jnp.zeros_like(state_ref)

    p = p_ref[...].reshape(R, P)

    r = p[:, 0:W]
    k = p[:, W:2 * W]
    v = p[:, 2 * W:3 * W]
    lora = p[:, 3 * W:3 * W + LANES]
    g_lo = p[:, 3 * W + LANES:3 * W + 2 * LANES]

    lane = lax.broadcasted_iota(jnp.int32, (1, LANES), 1)
    first_head = lane < HEAD_DIM
    z = jnp.where(first_head, jnp.tanh(lora), lora)
    wa = _mm(_dot, z, wwa_ref[...], SPLIT_LORA)
    logw = -math.exp(-0.5) * jax.nn.sigmoid(w0_ref[...] + wa[:, :W])
    a = jax.nn.sigmoid(a0_ref[...] + wa[:, W:])
    g = _mm(_dot, jax.nn.sigmoid(g_lo), gup_ref[...], SPLIT_LORA)

    kk = k * kk_ref[...]
    k2 = k * (1.0 + (a - 1.0) * ka_ref[...])

    ri = lax.broadcasted_iota(jnp.int32, (LANES, LANES), 0)
    ci = lax.broadcasted_iota(jnp.int32, (LANES, LANES), 1)
    same_head = (ri >> HEAD_SHIFT) == (ci >> HEAD_SHIFT)
    ri2 = lax.broadcasted_iota(jnp.int32, (MXU_DIM, MXU_DIM), 0)
    ci2 = lax.broadcasted_iota(jnp.int32, (MXU_DIM, MXU_DIM), 1)
    head_ones = ((ri2 >> HEAD_SHIFT) == (ci2 >> HEAD_SHIFT)).astype(F32)

    def head_sum(t):
        return jnp.concatenate(
            [_mm(_dot, t[:, i * MXU_DIM:(i + 1) * MXU_DIM], head_ones, SPLIT_SUM)
             for i in range(W // MXU_DIM)],
            axis=1)

    kkn = kk / jnp.maximum(jnp.sqrt(head_sum(kk * kk)), 1e-12)
    b = kkn * a
    bonus = head_sum(r * k2 * rk_ref[...]) * v

    tr = lax.broadcasted_iota(jnp.int32, (MXU_DIM, MXU_DIM), 0)
    tc = lax.broadcasted_iota(jnp.int32, (MXU_DIM, MXU_DIM), 1)
    in_chunk_prefix = ((tc <= tr) & ((tc >> CHUNK_SHIFT) == (tr >> CHUNK_SHIFT))).astype(F32)
    logp = jnp.concatenate(
        [_mm(_dot, in_chunk_prefix, logw[i * MXU_DIM:(i + 1) * MXU_DIM], SPLIT_CUMSUM)
         for i in range(R // MXU_DIM)], axis=0)
    r_t = r * jnp.exp(logp)
    a_t = -kkn * jnp.exp(logp - logw)
    inv = jnp.exp(-logp)
    b_t = b * inv
    k_t = k2 * inv

    tok = lax.broadcasted_iota(jnp.int32, (C, LANES), 0)
    col_tok = lax.broadcasted_iota(jnp.int32, (C, LANES), 1) & (C - 1)
    strict = col_tok < tok
    incl = col_tok <= tok
    incl2 = jnp.concatenate([incl, incl], axis=1)
    eye = (col_tok == tok).astype(F32)

    def stack(t):
        zero = jnp.zeros_like(t)
        return jnp.concatenate([jnp.where(first_head, t, zero), jnp.where(first_head, zero, t)], axis=0)

    n_pairs = W // LANES
    n_chunks = T // C
    lanes = [slice(i * LANES, (i + 1) * LANES) for i in range(n_pairs)]
    units = [(q, i) for q in range(R // C) for i in range(n_pairs)]
    rows_of = lambda q: slice(q * C, (q + 1) * C)
    c16 = lambda t: t.astype(BF16)
    a16, r16, b16, k16, v16 = c16(a_t), c16(r_t), c16(b_t), c16(k_t), c16(v)

    ar = {(q, i): jnp.concatenate([a16[rows_of(q), lanes[i]], r16[rows_of(q), lanes[i]]], axis=0)
          for q, i in units}
    bk = {(q, i): jnp.concatenate([stack(b16[rows_of(q), lanes[i]]), stack(k16[rows_of(q), lanes[i]])],
                                  axis=0) for q, i in units}
    v_s = {(q, i): stack(v16[rows_of(q), lanes[i]]) for q, i in units}
    gram = {u_: _dot_nt(ar[u_], bk[u_]) for u_ in units}
    l_ab = {u_: jnp.where(strict, gram[u_][:C, :LANES], 0.0) for u_ in units}
    a_ak = {u_: c16(jnp.where(strict, gram[u_][:C, LANES:], 0.0)) for u_ in units}
    a_r = {u_: c16(jnp.where(incl2, gram[u_][C:], 0.0)) for u_ in units}

    t_inv = {u_: eye + l_ab[u_] for u_ in units}
    lp16 = {u_: c16(l_ab[u_]) for u_ in units}
    l_pow = {u_: _dot(lp16[u_], stack(lp16[u_])) for u_ in units}
    levels = int(math.log2(C)) - 1
    for level in range(levels - 1):
        lp16 = {u_: c16(l_pow[u_]) for u_ in units}
        both = {u_: _dot(jnp.concatenate([c16(t_inv[u_]), lp16[u_]], axis=0), stack(lp16[u_]))
                for u_ in units}
        t_inv = {u_: t_inv[u_] + both[u_][:C] for u_ in units}
        l_pow = {u_: both[u_][C:] for u_ in units}
    t16 = {u_: c16(t_inv[u_] + _dot(c16(t_inv[u_]), stack(c16(l_pow[u_])))) for u_ in units}
    av = {u_: _dot(a_ak[u_], v_s[u_]) for u_ in units}

    states = {(nb, i): state_ref[nb, i] for nb in range(NB) for i in range(n_pairs)}
    y_chunks = {}
    for c in range(n_chunks):
        now = [(nb, nb * n_chunks + c, i) for nb in range(NB) for i in range(n_pairs)]
        logp_end = {q: logp[(q + 1) * C - 1:(q + 1) * C, :] for _, q, _ in now}
        to_end = {q: jnp.exp(logp_end[q] - logp[rows_of(q)]) for q in logp_end}
        bk_e = {(q, i): c16(jnp.concatenate([b[rows_of(q), lanes[i]] * to_end[q][:, lanes[i]],
                                             k2[rows_of(q), lanes[i]] * to_end[q][:, lanes[i]]], axis=0))
                for _, q, i in now}
        from_state = {(q, i): _dot_nt(ar[q, i], c16(states[nb, i])) for nb, q, i in now}
        u16 = {(q, i): c16(_dot(t16[q, i], stack(c16(from_state[q, i][:C] + av[q, i]))))
               for _, q, i in now}
        outer = {(q, i): _dot_tn(jnp.concatenate([u16[q, i], v16[rows_of(q), lanes[i]]], axis=0), bk_e[q, i])
                 for _, q, i in now}
        for nb, q, i in now:
            y_chunks[q, i] = from_state[q, i][C:] + _dot(
                a_r[q, i], jnp.concatenate([stack(u16[q, i]), v_s[q, i]], axis=0))
            states[nb, i] = (states[nb, i] * jnp.exp(logp_end[q][:, lanes[i]])
                             + jnp.where(same_head, outer[q, i], 0.0))
    for nb in range(NB):
        for i in range(n_pairs):
            state_ref[nb, i] = states[nb, i]

    y = jnp.concatenate([jnp.concatenate([y_chunks[q, i] for i in range(n_pairs)], axis=1)
                         for q in range(R // C)], axis=0)
    mean = head_sum(y) * (1.0 / HEAD_DIM)
    yc = y - mean
    var = head_sum(yc * yc) * (1.0 / HEAD_DIM)
    y = yc * lax.rsqrt(var + LNX_EPS) * lng_ref[...] + lnb_ref[...]
    o_ref[...] = ((y + bonus) * g).reshape(NB, T, W)


def _rwkv(p, w0, w_up, a0, a_up, g_up, k_k, k_a, r_k, ln_g, ln_b, *, ts=256, nb=2):
    B, S, P = p.shape
    W = RWKV_WIDTH
    C = RWKV_CHUNK
    wwa = jnp.zeros((LANES, 2 * W), F32)
    wwa = wwa.at[:DECAY_LORA, :W].set(w_up).at[DECAY_LORA:, W:].set(a_up)
    vec = lambda t: t.reshape(1, -1)
    row_spec = lambda n: pl.BlockSpec((1, n), lambda b, s: (0, 0))
    ts = min(ts, S)
    nb = min(nb, B)
    assert ts % C == 0 and S % ts == 0 and B % nb == 0 and (nb * ts) % MXU_DIM == 0
    return pl.pallas_call(
        _rwkv_kernel,
        grid=(B // nb, S // ts),
        in_specs=[
            pl.BlockSpec((nb, ts, P), lambda b, s: (b, s, 0)),
            row_spec(W), row_spec(W),
            pl.BlockSpec((LANES, 2 * W), lambda b, s: (0, 0)),
            pl.BlockSpec((GATE_LORA, W), lambda b, s: (0, 0)),
            row_spec(W), row_spec(W), row_spec(W), row_spec(W), row_spec(W),
        ],
        out_specs=pl.BlockSpec((nb, ts, W), lambda b, s: (b, s, 0)),
        out_shape=jax.ShapeDtypeStruct((B, S, W), F32),
        scratch_shapes=[pltpu.VMEM((nb, W // LANES, LANES, LANES), F32)],
        compiler_params=_params("parallel", "arbitrary"),
        name="rwkv7",
    )(p, vec(w0), vec(a0), wwa, g_up, vec(k_k), vec(k_a), vec(r_k), vec(ln_g), vec(ln_b))


def _rel_bias_kernel(tab_ref, o_ref, *, n_blocks):
    blk = MOBA_BLOCK
    max_exact = REL_BUCKETS // 2

    def bucket_of(n):
        large = max_exact + math.floor(math.log(max(n, 1) / max_exact)
                                       / math.log(REL_MAX_DISTANCE / max_exact) * (REL_BUCKETS - max_exact))
        return n if n < max_exact else min(large, REL_BUCKETS - 1)

    def block(cb):
        kc = lax.broadcasted_iota(jnp.int32, (blk, blk), 0) + cb * blk
        qi = lax.broadcasted_iota(jnp.int32, (blk, blk), 1)
        dist = qi + (n_blocks - 1) * blk - kc
        n = jnp.maximum(dist, 0)
        nf = jnp.maximum(n, 1).astype(F32)
        large = max_exact + (jnp.log(nf / max_exact) / math.log(REL_MAX_DISTANCE / max_exact)
                             * (REL_BUCKETS - max_exact)).astype(jnp.int32)
        large = jnp.minimum(large, REL_BUCKETS - 1)
        bucket = jnp.where(n < max_exact, n, large)
        n_lo = max((n_blocks - 1 - cb) * blk - (blk - 1), 0)
        n_hi = max((n_blocks - 1 - cb) * blk + (blk - 1), 0)
        b_lo = max(bucket_of(n_lo) - 1, 0)
        b_hi = min(bucket_of(n_hi) + 1, REL_BUCKETS - 1)
        for h in range(MOBA_HEADS):
            tile = jnp.zeros((blk, blk), F32)
            for c in range(b_lo, b_hi + 1):
                tile = jnp.where(bucket == c, tab_ref[h, c], tile)
            o_ref[h] = jnp.where(dist < 0, -jnp.inf, tile * LOG2E)

    for cb in range(n_blocks):
        pl.when(pl.program_id(0) == cb)(functools.partial(block, cb))


def _rel_bias_strip(rel_bias, n_blocks):
    blk = MOBA_BLOCK
    return pl.pallas_call(
        functools.partial(_rel_bias_kernel, n_blocks=n_blocks),
        grid=(n_blocks,),
        in_specs=[pl.BlockSpec(memory_space=pltpu.SMEM)],
        out_specs=pl.BlockSpec((MOBA_HEADS, blk, blk), lambda c: (0, c, 0)),
        out_shape=jax.ShapeDtypeStruct((MOBA_HEADS, n_blocks * blk, blk), F32),
        compiler_params=_params("parallel"),
        name="rel_bias_strip",
    )(rel_bias)


def _moba_kernel(q_ref, k_ref, v_ref, bias_ref, o_ref, *, n_blocks):
    blk = MOBA_BLOCK
    lane = lax.broadcasted_iota(jnp.int32, (1, LANES), 1)
    vrow = lax.broadcasted_iota(jnp.int32, (LANES, 1), 0)
    q = q_ref[0] * (HEAD_DIM ** -0.5 * LOG2E)
    k = k_ref[0]
    k16 = k.astype(BF16)
    vt = v_ref[0].T
    vt16 = [jnp.where((vrow >> HEAD_SHIFT) == e, vt, 1.0).astype(BF16) for e in range(HEADS_PER_TILE)]
    kmean = jnp.mean(k.reshape(n_blocks, blk, LANES), axis=1)

    gates, q16 = [], []
    for e in range(HEADS_PER_TILE):
        qe = jnp.where((lane >> HEAD_SHIFT) == e, q, 0.0)
        gates.append(_dot_nt(kmean, qe, HIGHEST))
        q16.append(qe.astype(BF16))

    def scores(qb, e):
        return _dot_nt(k16[:(qb + 1) * blk], q16[e][qb * blk:(qb + 1) * blk])

    def weights(qb, e, s_t):
        rows = slice(qb * blk, (qb + 1) * blk)
        bias_lo = (n_blocks - 1 - qb) * blk
        g = [gates[e][j:j + 1, rows] for j in range(qb)]
        tiles = []
        for j in range(qb + 1):
            t = s_t[j * blk:(j + 1) * blk] + bias_ref[e, bias_lo + j * blk:bias_lo + (j + 1) * blk, :]
            if j < qb:
                rank = jnp.zeros((1, blk), jnp.int32)
                for jj in range(qb):
                    if jj != j:
                        ahead = (g[jj] >= g[j]) if jj < j else (g[jj] > g[j])
                        rank = rank + jnp.where(ahead, 1, 0)
                t = jnp.where(rank < MOBA_TOPK, t, -jnp.inf)
            tiles.append(t)
        m = tiles[0].max(axis=0, keepdims=True)
        for t in tiles[1:]:
            m = jnp.maximum(m, t.max(axis=0, keepdims=True))
        return jnp.concatenate([jnp.exp2(t - m).astype(BF16) for t in tiles], axis=0)

    def attend(qb, e, p16):
        pv = _dot(vt16[e][:, :(qb + 1) * blk], p16)
        den_row = (1 - e) * HEAD_DIM
        return pv / pv[den_row:den_row + 1]

    units = [(qb, e) for qb in range(n_blocks) for e in range(HEADS_PER_TILE)]
    s_t, p16, out_t = {}, {}, {}
    for step in range(len(units) + 2):
        if step < len(units):
            s_t[units[step]] = scores(*units[step])
        if 0 <= step - 1 < len(units):
            u = units[step - 1]
            p16[u] = weights(*u, s_t.pop(u))
        if 0 <= step - 2 < len(units):
            u = units[step - 2]
            out_t[u] = attend(*u, p16.pop(u))
            qb, e = u
            if e == HEADS_PER_TILE - 1:
                o_ref[0, qb * blk:(qb + 1) * blk, :] = jnp.where(
                    vrow < HEAD_DIM, out_t.pop((qb, 0)), out_t.pop((qb, 1))).T


def _moba(qkv, bias_strip):
    B, S, _ = qkv.shape
    blk = MOBA_BLOCK
    n_blocks = S // blk
    n_pairs = MOBA_WIDTH // LANES
    return pl.pallas_call(
        functools.partial(_moba_kernel, n_blocks=n_blocks),
        grid=(n_pairs, B),
        in_specs=[
            pl.BlockSpec((1, S, LANES), lambda p, b: (b, 0, p)),
            pl.BlockSpec((1, S, LANES), lambda p, b: (b, 0, n_pairs + p)),
            pl.BlockSpec((1, S, LANES), lambda p, b: (b, 0, 2 * n_pairs + p)),
            pl.BlockSpec((HEADS_PER_TILE, S, blk), lambda p, b: (p, 0, 0)),
        ],
        out_specs=pl.BlockSpec((1, S, LANES), lambda p, b: (b, 0, p)),
        out_shape=jax.ShapeDtypeStruct((B, S, MOBA_WIDTH), F32),
        compiler_params=_params("parallel", "parallel"),
        name="moba",
    )(qkv, qkv, qkv, bias_strip)


def _xattn_kernel(x_ref, ya_ref, yb_ref, wa_ref, wb_ref, g_ref, wq_ref, kv_ref, wo_ref, o_ref, *, sub):
    D = x_ref.shape[-1]
    dh = D // XATTN_HEADS
    k16 = [kv_ref[0, :, i * dh:(i + 1) * dh].astype(BF16) for i in range(XATTN_HEADS)]
    v16 = [kv_ref[0, :, D + i * dh:D + (i + 1) * dh].astype(BF16) for i in range(XATTN_HEADS)]

    def project(r0):
        rows = slice(r0, r0 + sub)
        x = (x_ref[0, rows, :] + _dot(ya_ref[0, rows, :].astype(BF16), wa_ref[...])
             + _dot(yb_ref[0, rows, :].astype(BF16), wb_ref[...]))
        h = _rms(x, g_ref[...]).astype(BF16)
        q = (_dot(h, wq_ref[...]) * (dh ** -0.5 * LOG2E)).astype(BF16)
        return x, q

    def attend(q):
        hs = range(XATTN_HEADS)
        s = [_dot_nt(q[:, i * dh:(i + 1) * dh], k16[i]) for i in hs]
        pr = [jnp.exp2(s[i] - jnp.max(s[i], axis=-1, keepdims=True)) for i in hs]
        pr = [(pr[i] / jnp.sum(pr[i], axis=-1, keepdims=True)).astype(BF16) for i in hs]
        return jnp.concatenate([_dot(pr[i], v16[i]) for i in hs], axis=1).astype(BF16)

    starts = list(range(0, x_ref.shape[1], sub))
    xq, att = {}, {}
    for step in range(len(starts) + 2):
        if step < len(starts):
            xq[starts[step]] = project(starts[step])
        if 0 <= step - 1 < len(starts):
            r0 = starts[step - 1]
            att[r0] = attend(xq[r0][1])
        if 0 <= step - 2 < len(starts):
            r0 = starts[step - 2]
            o_ref[0, r0:r0 + sub, :] = xq.pop(r0)[0] + _dot(att.pop(r0), wo_ref[...])


def _xattn(x, ya, yb, w_mix, g, w_q, kv, w_o, layer, *, tm=512, sub=512):
    B, S, D = x.shape
    M = kv.shape[1]
    na, nb = ya.shape[-1], yb.shape[-1]
    assert na == nb
    tm = min(tm, S)
    of_layer = lambda shape, blk=0: pl.BlockSpec((None,) + shape, lambda b, i: (layer, blk, 0))
    return pl.pallas_call(
        functools.partial(_xattn_kernel, sub=min(sub, tm)),
        grid=(B, S // tm),
        in_specs=[
            pl.BlockSpec((1, tm, D), lambda b, i: (b, i, 0)),
            pl.BlockSpec((1, tm, na), lambda b, i: (b, i, 0)),
            pl.BlockSpec((1, tm, nb), lambda b, i: (b, i, 0)),
            of_layer((na, D), 0),
            of_layer((nb, D), 1),
            pl.BlockSpec((1, D), lambda b, i: (0, 0)),
            of_layer((D, D)),
            pl.BlockSpec((1, M, 2 * D), lambda b, i: (b, 0, 0)),
            of_layer((D, D)),
        ],
        out_specs=pl.BlockSpec((1, tm, D), lambda b, i: (b, i, 0)),
        out_shape=jax.ShapeDtypeStruct((B, S, D), F32),
        compiler_params=_params("parallel", "parallel"),
        name="xattn",
    )(x, ya, yb, w_mix, w_mix, g.reshape(1, D), w_q, kv, w_o)


def kernel(x, mem, rel_bias, final_norm_g, ffn1_norm_g, ffn1_w_in, ffn1_w_out, mix_norm_g, w_mix_in, w_mix_out, rwkv_mu, rwkv_w0, rwkv_w_up, rwkv_a0, rwkv_a_up, rwkv_g_up, rwkv_k_k, rwkv_k_a, rwkv_r_k, rwkv_ln_g, rwkv_ln_b, xattn_norm_g, mem_norm_g, xattn_w_q, xattn_w_kv, xattn_w_o, ffn2_norm_g, ffn2_w_in, ffn2_w_out):
    B, S, D = x.shape
    M = mem.shape[1]
    depth = ffn1_w_in.shape[0]
    T = B * S
    ffn1_w_in, ffn1_w_out, ffn2_w_in, ffn2_w_out, w_mix_in, w_mix_out, xattn_w_q, xattn_w_kv, xattn_w_o = (
        w.astype(BF16) for w in (ffn1_w_in, ffn1_w_out, ffn2_w_in, ffn2_w_out, w_mix_in, w_mix_out,
                                 xattn_w_q, xattn_w_kv, xattn_w_o))

    bias_strip = _rel_bias_strip(rel_bias, S // MOBA_BLOCK)
    mem2 = mem.reshape(B * M, D)
    x = x.reshape(T, D)
    for l in range(depth):
        x = _ffn(x, ffn1_norm_g[l], ffn1_w_in, ffn1_w_out, l)
        p_rwkv, qkv = _norm_proj(x, mix_norm_g[l], w_mix_in, l, (RWKV_PROJ, MOBA_PROJ),
                                 shift_mu=rwkv_mu[l], seq_len=S)
        y_rwkv = _rwkv(p_rwkv.reshape(B, S, RWKV_PROJ), rwkv_w0[l], rwkv_w_up[l],
                       rwkv_a0[l], rwkv_a_up[l], rwkv_g_up[l], rwkv_k_k[l], rwkv_k_a[l],
                       rwkv_r_k[l], rwkv_ln_g[l], rwkv_ln_b[l])
        y_moba = _moba(qkv.reshape(B, S, MOBA_PROJ), bias_strip)
        (kv,) = _norm_proj(mem2, mem_norm_g[l], xattn_w_kv, l, (2 * D,))
        x = _xattn(x.reshape(B, S, D), y_rwkv, y_moba, w_mix_out, xattn_norm_g[l],
                   xattn_w_q, kv.reshape(B, M, 2 * D), xattn_w_o, l).reshape(T, D)
        x = _ffn(x, ffn2_norm_g[l], ffn2_w_in, ffn2_w_out, l,
                 final_norm_g if l == depth - 1 else None)
    return x.reshape(B, S, D)
```

```python
import functools
import math

import jax
import jax.numpy as jnp
from jax import lax
from jax.experimental import pallas as pl
from jax.experimental.pallas import tpu as pltpu

F32 = jnp.float32
BF16 = jnp.bfloat16
HIGHEST = lax.Precision.HIGHEST

HEAD_DIM = 64
RWKV_WIDTH = 512
MOBA_WIDTH = 512
MOBA_HEADS = MOBA_WIDTH // HEAD_DIM
DECAY_LORA = 64
ICLR_LORA = 64
GATE_LORA = 128
RWKV_PROJ = 3 * RWKV_WIDTH + DECAY_LORA + ICLR_LORA + GATE_LORA
MOBA_PROJ = 3 * MOBA_WIDTH
LNX_EPS = 64e-5
MOBA_BLOCK = 256
MOBA_TOPK = 3
REL_BUCKETS = 32
REL_MAX_DISTANCE = 1024
XATTN_HEADS = 4
FFN_RES_WEIGHT = 0.5
NORM_EPS = 1e-6
LOG2E = math.log2(math.e)

LANES = 128
MXU_DIM = 256
HEADS_PER_TILE = LANES // HEAD_DIM
HEAD_SHIFT = HEAD_DIM.bit_length() - 1
RWKV_CHUNK = 64
CHUNK_SHIFT = RWKV_CHUNK.bit_length() - 1
SPLIT_LORA = (1, 1, 1)
SPLIT_SUM = (1, 1, 1)
SPLIT_CUMSUM = (1, 2, 2)
SPLIT_CHUNK = (1, 1, 1)
VMEM_LIMIT = 56 * 1024 * 1024


def _rms(x, g):
    ms = jnp.mean(x * x, axis=-1, keepdims=True)
    return x * lax.rsqrt(ms + NORM_EPS) * g


def _sigmoid(x):
    return 0.5 * jnp.tanh(0.5 * x) + 0.5


def _dot(a, b, precision=None):
    return jnp.dot(a, b, precision=precision, preferred_element_type=F32)


def _dot_nt(a, b, precision=None):
    return lax.dot_general(a, b, (((1,), (1,)), ((), ())), precision=precision,
                           preferred_element_type=F32)


def _dot_tn(a, b, precision=None):
    return lax.dot_general(a, b, (((0,), (0,)), ((), ())), precision=precision,
                           preferred_element_type=F32)


def _bf16_terms(x, n):
    terms = []
    for i in range(n):
        t = x.astype(BF16)
        terms.append(t)
        if i + 1 < n:
            x = x - t.astype(F32)
    return terms


def _mm(dot, a, b, split):
    na, nb, order = split
    at, bt = _bf16_terms(a, na), _bf16_terms(b, nb)
    acc = None
    for i in range(na):
        for j in range(nb):
            if i + j < order:
                t = dot(at[i], bt[j])
                acc = t if acc is None else acc + t
    return acc


def _params(*semantics):
    return pltpu.CompilerParams(dimension_semantics=semantics, vmem_limit_bytes=VMEM_LIMIT)


def _ffn_kernel(x_ref, g_ref, wi_ref, wo_ref, *rest, final_norm, sub, n_side):
    rest = list(rest)
    fg_ref = rest.pop(0) if final_norm else None
    side_in, o_ref, side_out = rest[:n_side], rest[n_side], rest[n_side + 1:]
    F = wo_ref.shape[0]
    for r0 in range(0, x_ref.shape[0], sub):
        x = x_ref[r0:r0 + sub, :]
        h = _rms(x, g_ref[...]).astype(BF16)
        gate = _dot(h, wi_ref[:, :F])
        up = _dot(h, wi_ref[:, F:])
        act = (gate * _sigmoid(gate) * up).astype(BF16)
        y = x + FFN_RES_WEIGHT * _dot(act, wo_ref[...])
        if final_norm:
            y = _rms(y, fg_ref[...])
        o_ref[r0:r0 + sub, :] = y
    for w_ref, w16_ref in zip(side_in, side_out):
        w16_ref[...] = w_ref[...].astype(BF16)


def _ffn(x, g, w_in, w_out, layer, final_g=None, side=(), *, tm=512, sub=256):
    T, D = x.shape
    F = w_out.shape[1]
    tm = min(tm, T)
    steps = T // tm
    resident = lambda shape: pl.BlockSpec(shape, lambda i: (0, 0), pipeline_mode=pl.Buffered(1))
    of_layer = lambda shape: pl.BlockSpec((None,) + shape, lambda i: (layer, 0, 0),
                                          pipeline_mode=pl.Buffered(1))
    in_specs = [
        pl.BlockSpec((tm, D), lambda i: (i, 0)),
        resident((1, D)),
        of_layer((D, 2 * F)),
        of_layer((F, D)),
    ]
    args = [x, g.reshape(1, D), w_in, w_out]
    if final_g is not None:
        in_specs.append(resident((1, D)))
        args.append(final_g.reshape(1, D))
    out_specs = [pl.BlockSpec((tm, D), lambda i: (i, 0))]
    out_shape = [jax.ShapeDtypeStruct((T, D), F32)]
    for w, w_layer in side:
        n_layers, rows, cols = w.shape
        assert rows % steps == 0
        slab = (rows // steps, cols)
        in_specs.append(pl.BlockSpec((None, None) + slab, lambda i, w_layer=w_layer: (w_layer, i, 0, 0)))
        args.append(w.reshape((n_layers, steps) + slab))
        out_specs.append(pl.BlockSpec((None,) + slab, lambda i: (i, 0, 0)))
        out_shape.append(jax.ShapeDtypeStruct((steps,) + slab, BF16))
    y, *side16 = pl.pallas_call(
        functools.partial(_ffn_kernel, final_norm=final_g is not None, sub=min(sub, tm), n_side=len(side)),
        grid=(steps,),
        in_specs=in_specs,
        out_specs=out_specs,
        out_shape=out_shape,
        compiler_params=_params("parallel"),
        name="ffn",
    )(*args)
    return y, [w16.reshape((1,) + w.shape[1:]) for w16, (w, _) in zip(side16, side)]


def _norm_proj_kernel(x_ref, g_ref, w_ref, *rest, splits, tiles_per_seq, sub):
    if tiles_per_seq:
        mu_ref, *o_refs, prev_ref = rest

        @pl.when(pl.program_id(0) % tiles_per_seq == 0)
        def _():
            prev_ref[...] = jnp.zeros_like(prev_ref)

        before = prev_ref[...]
    else:
        o_refs = rest
    row = lax.broadcasted_iota(jnp.int32, (sub, 1), 0)
    for r0 in range(0, x_ref.shape[0], sub):
        h = _rms(x_ref[r0:r0 + sub, :], g_ref[...]).astype(BF16)
        off = 0
        for idx, (o_ref, n) in enumerate(zip(o_refs, splits)):
            y = _dot(h, w_ref[:, off:off + n])
            if tiles_per_seq and idx == 0:
                shifted = jnp.where(row == 0, before, pltpu.roll(y, 1, axis=0))
                before = y[sub - 1:sub, :]
                y = y + (shifted - y) * mu_ref[...]
            o_ref[r0:r0 + sub, :] = y
            off += n
    if tiles_per_seq:
        prev_ref[...] = before


def _norm_proj(x, g, w, layer, splits, *, shift_mu=None, seq_len=None, tm=512, sub=128):
    T, D = x.shape
    N = w.shape[2]
    tm = min(tm, T)
    assert sum(splits) == N
    in_specs = [
        pl.BlockSpec((tm, D), lambda i: (i, 0)),
        pl.BlockSpec((1, D), lambda i: (0, 0)),
        pl.BlockSpec((None, D, N), lambda i: (layer, 0, 0)),
    ]
    args = [x, g.reshape(1, D), w]
    scratch, tiles_per_seq = [], 0
    if shift_mu is not None:
        assert seq_len % tm == 0
        tiles_per_seq = seq_len // tm
        in_specs.append(pl.BlockSpec((1, splits[0]), lambda i: (0, 0)))
        args.append(shift_mu.reshape(1, splits[0]))
        scratch = [pltpu.VMEM((1, splits[0]), F32)]
    return pl.pallas_call(
        functools.partial(_norm_proj_kernel, splits=splits, tiles_per_seq=tiles_per_seq,
                          sub=min(sub, tm)),
        grid=(T // tm,),
        in_specs=in_specs,
        out_specs=[pl.BlockSpec((tm, n), lambda i: (i, 0)) for n in splits],
        out_shape=[jax.ShapeDtypeStruct((T, n), F32) for n in splits],
        scratch_shapes=scratch,
        compiler_params=_params("arbitrary" if tiles_per_seq else "parallel"),
        name="norm_proj",
    )(*args)


def _rwkv_kernel(p_ref, w0_ref, a0_ref, wwa_ref, gup_ref, kk_ref, ka_ref, rk_ref,
                 lng_ref, lnb_ref, o_ref, state_ref):
    C = RWKV_CHUNK
    W = RWKV_WIDTH
    NB, T, P = p_ref.shape
    R = NB * T
    s_idx = pl.program_id(1)

    @pl.when(s_idx == 0)
    def _():
        state_ref[...] = jnp.zeros_like(state_ref)

    p = p_ref[...].reshape(R, P)

    r = p[:, 0:W]
    k = p[:, W:2 * W]
    v = p[:, 2 * W:3 * W]
    lora = p[:, 3 * W:3 * W + LANES]
    g_lo = p[:, 3 * W + LANES:3 * W + 2 * LANES]

    lane = lax.broadcasted_iota(jnp.int32, (1, LANES), 1)
    first_head = lane < HEAD_DIM
    z = jnp.where(first_head, jnp.tanh(lora), lora)
    wa = _mm(_dot, z, wwa_ref[...], SPLIT_LORA)
    logw = -math.exp(-0.5) * _sigmoid(w0_ref[...] + wa[:, :W])
    a = _sigmoid(a0_ref[...] + wa[:, W:])
    g = _mm(_dot, _sigmoid(g_lo), gup_ref[...], SPLIT_LORA)

    kk = k * kk_ref[...]
    k2 = k * (1.0 + (a - 1.0) * ka_ref[...])

    ri = lax.broadcasted_iota(jnp.int32, (LANES, LANES), 0)
    ci = lax.broadcasted_iota(jnp.int32, (LANES, LANES), 1)
    same_head = (ri >> HEAD_SHIFT) == (ci >> HEAD_SHIFT)
    ri2 = lax.broadcasted_iota(jnp.int32, (MXU_DIM, MXU_DIM), 0)
    ci2 = lax.broadcasted_iota(jnp.int32, (MXU_DIM, MXU_DIM), 1)
    head_ones = ((ri2 >> HEAD_SHIFT) == (ci2 >> HEAD_SHIFT)).astype(F32)

    def head_sum(t):
        return jnp.concatenate(
            [_mm(_dot, t[:, i * MXU_DIM:(i + 1) * MXU_DIM], head_ones, SPLIT_SUM)
             for i in range(W // MXU_DIM)],
            axis=1)

    kkn = kk * lax.rsqrt(jnp.maximum(head_sum(kk * kk), 1e-24))
    b = kkn * a
    bonus = head_sum(r * k2 * rk_ref[...]) * v

    tr = lax.broadcasted_iota(jnp.int32, (MXU_DIM, MXU_DIM), 0)
    tc = lax.broadcasted_iota(jnp.int32, (MXU_DIM, MXU_DIM), 1)
    in_chunk_prefix = ((tc <= tr) & ((tc >> CHUNK_SHIFT) == (tr >> CHUNK_SHIFT))).astype(F32)
    logp = jnp.concatenate(
        [_mm(_dot, in_chunk_prefix, logw[i * MXU_DIM:(i + 1) * MXU_DIM], SPLIT_CUMSUM)
         for i in range(R // MXU_DIM)], axis=0)
    r_t = r * jnp.exp(logp)
    a_t = -kkn * jnp.exp(logp - logw)
    inv = jnp.exp(-logp)
    b_t = b * inv
    k_t = k2 * inv

    tok = lax.broadcasted_iota(jnp.int32, (C, LANES), 0)
    col_tok = lax.broadcasted_iota(jnp.int32, (C, LANES), 1) & (C - 1)
    strict = col_tok < tok
    incl = col_tok <= tok
    incl2 = jnp.concatenate([incl, incl], axis=1)
    eye = (col_tok == tok).astype(F32)

    def stack(t):
        zero = jnp.zeros_like(t)
        return jnp.concatenate([jnp.where(first_head, t, zero), jnp.where(first_head, zero, t)], axis=0)

    n_pairs = W // LANES
    n_chunks = T // C
    lanes = [slice(i * LANES, (i + 1) * LANES) for i in range(n_pairs)]
    units = [(q, i) for q in range(R // C) for i in range(n_pairs)]
    rows_of = lambda q: slice(q * C, (q + 1) * C)
    c16 = lambda t: t.astype(BF16)
    a16, r16, b16, k16, v16 = c16(a_t), c16(r_t), c16(b_t), c16(k_t), c16(v)

    ar = {(q, i): jnp.concatenate([a16[rows_of(q), lanes[i]], r16[rows_of(q), lanes[i]]], axis=0)
          for q, i in units}
    bk = {(q, i): jnp.concatenate([stack(b16[rows_of(q), lanes[i]]), stack(k16[rows_of(q), lanes[i]])],
                                  axis=0) for q, i in units}
    v_s = {(q, i): stack(v16[rows_of(q), lanes[i]]) for q, i in units}
    gram = {u_: _dot_nt(ar[u_], bk[u_]) for u_ in units}
    l_ab = {u_: jnp.where(strict, gram[u_][:C, :LANES], 0.0) for u_ in units}
    a_ak = {u_: c16(jnp.where(strict, gram[u_][:C, LANES:], 0.0)) for u_ in units}
    a_r = {u_: c16(jnp.where(incl2, gram[u_][C:], 0.0)) for u_ in units}

    t_inv = {u_: eye + l_ab[u_] for u_ in units}
    lp16 = {u_: c16(l_ab[u_]) for u_ in units}
    l_pow = {u_: _dot(lp16[u_], stack(lp16[u_])) for u_ in units}
    levels = int(math.log2(C)) - 1
    for level in range(levels - 1):
        lp16 = {u_: c16(l_pow[u_]) for u_ in units}
        both = {u_: _dot(jnp.concatenate([c16(t_inv[u_]), lp16[u_]], axis=0), stack(lp16[u_]))
                for u_ in units}
        t_inv = {u_: t_inv[u_] + both[u_][:C] for u_ in units}
        l_pow = {u_: both[u_][C:] for u_ in units}
    t16 = {u_: c16(t_inv[u_] + _dot(c16(t_inv[u_]), stack(c16(l_pow[u_])))) for u_ in units}
    av = {u_: _dot(a_ak[u_], v_s[u_]) for u_ in units}

    states = {(nb, i): state_ref[nb, i] for nb in range(NB) for i in range(n_pairs)}
    y_chunks = {}
    for c in range(n_chunks):
        now = [(nb, nb * n_chunks + c, i) for nb in range(NB) for i in range(n_pairs)]
        logp_end = {q: logp[(q + 1) * C - 1:(q + 1) * C, :] for _, q, _ in now}
        to_end = {q: jnp.exp(logp_end[q] - logp[rows_of(q)]) for q in logp_end}
        bk_e = {(q, i): c16(jnp.concatenate([b[rows_of(q), lanes[i]] * to_end[q][:, lanes[i]],
                                             k2[rows_of(q), lanes[i]] * to_end[q][:, lanes[i]]], axis=0))
                for _, q, i in now}
        from_state = {(q, i): _dot_nt(ar[q, i], c16(states[nb, i])) for nb, q, i in now}
        u16 = {(q, i): c16(_dot(t16[q, i], stack(c16(from_state[q, i][:C] + av[q, i]))))
               for _, q, i in now}
        outer = {(q, i): _dot_tn(jnp.concatenate([u16[q, i], v16[rows_of(q), lanes[i]]], axis=0), bk_e[q, i])
                 for _, q, i in now}
        for nb, q, i in now:
            y_chunks[q, i] = from_state[q, i][C:] + _dot(
                a_r[q, i], jnp.concatenate([stack(u16[q, i]), v_s[q, i]], axis=0))
            states[nb, i] = (states[nb, i] * jnp.exp(logp_end[q][:, lanes[i]])
                             + jnp.where(same_head, outer[q, i], 0.0))
    for nb in range(NB):
        for i in range(n_pairs):
            state_ref[nb, i] = states[nb, i]

    y = jnp.concatenate([jnp.concatenate([y_chunks[q, i] for i in range(n_pairs)], axis=1)
                         for q in range(R // C)], axis=0)
    mean = head_sum(y) * (1.0 / HEAD_DIM)
    yc = y - mean
    var = head_sum(yc * yc) * (1.0 / HEAD_DIM)
    y = yc * lax.rsqrt(var + LNX_EPS) * lng_ref[...] + lnb_ref[...]
    o_ref[...] = ((y + bonus) * g).reshape(NB, T, W)


def _rwkv(p, w0, w_up, a0, a_up, g_up, k_k, k_a, r_k, ln_g, ln_b, *, ts=256, nb=2):
    B, S, P = p.shape
    W = RWKV_WIDTH
    C = RWKV_CHUNK
    wwa = jnp.zeros((LANES, 2 * W), F32)
    wwa = wwa.at[:DECAY_LORA, :W].set(w_up).at[DECAY_LORA:, W:].set(a_up)
    vec = lambda t: t.reshape(1, -1)
    row_spec = lambda n: pl.BlockSpec((1, n), lambda b, s: (0, 0))
    ts = min(ts, S)
    nb = min(nb, B)
    assert ts % C == 0 and S % ts == 0 and B % nb == 0 and (nb * ts) % MXU_DIM == 0
    return pl.pallas_call(
        _rwkv_kernel,
        grid=(B // nb, S // ts),
        in_specs=[
            pl.BlockSpec((nb, ts, P), lambda b, s: (b, s, 0)),
            row_spec(W), row_spec(W),
            pl.BlockSpec((LANES, 2 * W), lambda b, s: (0, 0)),
            pl.BlockSpec((GATE_LORA, W), lambda b, s: (0, 0)),
            row_spec(W), row_spec(W), row_spec(W), row_spec(W), row_spec(W),
        ],
        out_specs=pl.BlockSpec((nb, ts, W), lambda b, s: (b, s, 0)),
        out_shape=jax.ShapeDtypeStruct((B, S, W), F32),
        scratch_shapes=[pltpu.VMEM((nb, W // LANES, LANES, LANES), F32)],
        compiler_params=_params("parallel", "arbitrary"),
        name="rwkv7",
    )(p, vec(w0), vec(a0), wwa, g_up, vec(k_k), vec(k_a), vec(r_k), vec(ln_g), vec(ln_b))


def _rel_bias_kernel(tab_ref, o_ref, *, n_blocks):
    blk = MOBA_BLOCK
    max_exact = REL_BUCKETS // 2

    def bucket_of(n):
        large = max_exact + math.floor(math.log(max(n, 1) / max_exact)
                                       / math.log(REL_MAX_DISTANCE / max_exact) * (REL_BUCKETS - max_exact))
        return n if n < max_exact else min(large, REL_BUCKETS - 1)

    def block(cb):
        kc = lax.broadcasted_iota(jnp.int32, (blk, blk), 0) + cb * blk
        qi = lax.broadcasted_iota(jnp.int32, (blk, blk), 1)
        dist = qi + (n_blocks - 1) * blk - kc
        n = jnp.maximum(dist, 0)
        nf = jnp.maximum(n, 1).astype(F32)
        large = max_exact + (jnp.log(nf / max_exact) / math.log(REL_MAX_DISTANCE / max_exact)
                             * (REL_BUCKETS - max_exact)).astype(jnp.int32)
        large = jnp.minimum(large, REL_BUCKETS - 1)
        bucket = jnp.where(n < max_exact, n, large)
        n_lo = max((n_blocks - 1 - cb) * blk - (blk - 1), 0)
        n_hi = max((n_blocks - 1 - cb) * blk + (blk - 1), 0)
        b_lo = max(bucket_of(n_lo) - 1, 0)
        b_hi = min(bucket_of(n_hi) + 1, REL_BUCKETS - 1)
        for h in range(MOBA_HEADS):
            tile = jnp.zeros((blk, blk), F32)
            for c in range(b_lo, b_hi + 1):
                tile = jnp.where(bucket == c, tab_ref[h, c], tile)
            o_ref[h] = jnp.where(dist < 0, -jnp.inf, tile * LOG2E)

    for cb in range(n_blocks):
        pl.when(pl.program_id(0) == cb)(functools.partial(block, cb))


def _rel_bias_strip(rel_bias, n_blocks):
    blk = MOBA_BLOCK
    return pl.pallas_call(
        functools.partial(_rel_bias_kernel, n_blocks=n_blocks),
        grid=(n_blocks,),
        in_specs=[pl.BlockSpec(memory_space=pltpu.SMEM)],
        out_specs=pl.BlockSpec((MOBA_HEADS, blk, blk), lambda c: (0, c, 0)),
        out_shape=jax.ShapeDtypeStruct((MOBA_HEADS, n_blocks * blk, blk), F32),
        compiler_params=_params("parallel"),
        name="rel_bias_strip",
    )(rel_bias)


def _moba_kernel(q_ref, k_ref, v_ref, bias_ref, o_ref, *, n_blocks):
    blk = MOBA_BLOCK
    lane = lax.broadcasted_iota(jnp.int32, (1, LANES), 1)
    vrow = lax.broadcasted_iota(jnp.int32, (LANES, 1), 0)
    q = q_ref[0] * (HEAD_DIM ** -0.5 * LOG2E)
    k = k_ref[0]
    k16 = k.astype(BF16)
    vt = v_ref[0].T
    vt16 = [jnp.where((vrow >> HEAD_SHIFT) == e, vt, 1.0).astype(BF16) for e in range(HEADS_PER_TILE)]
    kmean = jnp.mean(k.reshape(n_blocks, blk, LANES), axis=1)

    gates, q16 = [], []
    for e in range(HEADS_PER_TILE):
        qe = jnp.where((lane >> HEAD_SHIFT) == e, q, 0.0)
        gates.append(_dot_nt(kmean, qe, HIGHEST))
        q16.append(qe.astype(BF16))

    def scores(qb, e):
        return _dot_nt(k16[:(qb + 1) * blk], q16[e][qb * blk:(qb + 1) * blk])

    def weights(qb, e, s_t):
        rows = slice(qb * blk, (qb + 1) * blk)
        bias_lo = (n_blocks - 1 - qb) * blk
        g = [gates[e][j:j + 1, rows] for j in range(qb)]
        tiles = []
        for j in range(qb + 1):
            t = s_t[j * blk:(j + 1) * blk] + bias_ref[e, bias_lo + j * blk:bias_lo + (j + 1) * blk, :]
            if j < qb:
                rank = jnp.zeros((1, blk), jnp.int32)
                for jj in range(qb):
                    if jj != j:
                        ahead = (g[jj] >= g[j]) if jj < j else (g[jj] > g[j])
                        rank = rank + jnp.where(ahead, 1, 0)
                t = jnp.where(rank < MOBA_TOPK, t, -jnp.inf)
            tiles.append(t)
        m = tiles[0].max(axis=0, keepdims=True)
        for t in tiles[1:]:
            m = jnp.maximum(m, t.max(axis=0, keepdims=True))
        return jnp.concatenate([jnp.exp2(t - m).astype(BF16) for t in tiles], axis=0)

    def attend(qb, e, p16):
        pv = _dot(vt16[e][:, :(qb + 1) * blk], p16)
        den_row = (1 - e) * HEAD_DIM
        return pv / pv[den_row:den_row + 1]

    units = [(qb, e) for qb in range(n_blocks) for e in range(HEADS_PER_TILE)]
    s_t, p16, out_t = {}, {}, {}
    for step in range(len(units) + 2):
        if step < len(units):
            s_t[units[step]] = scores(*units[step])
        if 0 <= step - 1 < len(units):
            u = units[step - 1]
            p16[u] = weights(*u, s_t.pop(u))
        if 0 <= step - 2 < len(units):
            u = units[step - 2]
            out_t[u] = attend(*u, p16.pop(u))
            qb, e = u
            if e == HEADS_PER_TILE - 1:
                o_ref[0, qb * blk:(qb + 1) * blk, :] = jnp.where(
                    vrow < HEAD_DIM, out_t.pop((qb, 0)), out_t.pop((qb, 1))).T


def _moba(qkv, bias_strip):
    B, S, _ = qkv.shape
    blk = MOBA_BLOCK
    n_blocks = S // blk
    n_pairs = MOBA_WIDTH // LANES
    return pl.pallas_call(
        functools.partial(_moba_kernel, n_blocks=n_blocks),
        grid=(n_pairs, B),
        in_specs=[
            pl.BlockSpec((1, S, LANES), lambda p, b: (b, 0, p)),
            pl.BlockSpec((1, S, LANES), lambda p, b: (b, 0, n_pairs + p)),
            pl.BlockSpec((1, S, LANES), lambda p, b: (b, 0, 2 * n_pairs + p)),
            pl.BlockSpec((HEADS_PER_TILE, S, blk), lambda p, b: (p, 0, 0)),
        ],
        out_specs=pl.BlockSpec((1, S, LANES), lambda p, b: (b, 0, p)),
        out_shape=jax.ShapeDtypeStruct((B, S, MOBA_WIDTH), F32),
        compiler_params=_params("parallel", "parallel"),
        name="moba",
    )(qkv, qkv, qkv, bias_strip)


def _xattn_kernel(x_ref, ya_ref, yb_ref, wa_ref, wb_ref, g_ref, wq_ref, kv_ref, wo_ref, o_ref, *, sub):
    D = x_ref.shape[-1]
    dh = D // XATTN_HEADS
    k16 = [kv_ref[0, :, i * dh:(i + 1) * dh].astype(BF16) for i in range(XATTN_HEADS)]
    v16 = [kv_ref[0, :, D + i * dh:D + (i + 1) * dh].astype(BF16) for i in range(XATTN_HEADS)]

    def project(r0):
        rows = slice(r0, r0 + sub)
        x = (x_ref[0, rows, :] + _dot(ya_ref[0, rows, :].astype(BF16), wa_ref[...])
             + _dot(yb_ref[0, rows, :].astype(BF16), wb_ref[...]))
        h = _rms(x, g_ref[...]).astype(BF16)
        q = (_dot(h, wq_ref[...]) * (dh ** -0.5 * LOG2E)).astype(BF16)
        return x, q

    def attend(q):
        hs = range(XATTN_HEADS)
        s = [_dot_nt(q[:, i * dh:(i + 1) * dh], k16[i]) for i in hs]
        pr = [jnp.exp2(s[i] - jnp.max(s[i], axis=-1, keepdims=True)) for i in hs]
        pr = [(pr[i] / jnp.sum(pr[i], axis=-1, keepdims=True)).astype(BF16) for i in hs]
        return jnp.concatenate([_dot(pr[i], v16[i]) for i in hs], axis=1).astype(BF16)

    starts = list(range(0, x_ref.shape[1], sub))
    xq, att = {}, {}
    for step in range(len(starts) + 2):
        if step < len(starts):
            xq[starts[step]] = project(starts[step])
        if 0 <= step - 1 < len(starts):
            r0 = starts[step - 1]
            att[r0] = attend(xq[r0][1])
        if 0 <= step - 2 < len(starts):
            r0 = starts[step - 2]
            o_ref[0, r0:r0 + sub, :] = xq.pop(r0)[0] + _dot(att.pop(r0), wo_ref[...])


def _xattn(x, ya, yb, w_mix, g, w_q, kv, w_o, layer, *, tm=512, sub=512):
    B, S, D = x.shape
    M = kv.shape[1]
    na, nb = ya.shape[-1], yb.shape[-1]
    assert na == nb
    tm = min(tm, S)
    of_layer = lambda shape, blk=0: pl.BlockSpec((None,) + shape, lambda b, i: (layer, blk, 0))
    return pl.pallas_call(
        functools.partial(_xattn_kernel, sub=min(sub, tm)),
        grid=(B, S // tm),
        in_specs=[
            pl.BlockSpec((1, tm, D), lambda b, i: (b, i, 0)),
            pl.BlockSpec((1, tm, na), lambda b, i: (b, i, 0)),
            pl.BlockSpec((1, tm, nb), lambda b, i: (b, i, 0)),
            of_layer((na, D), 0),
            of_layer((nb, D), 1),
            pl.BlockSpec((1, D), lambda b, i: (0, 0)),
            of_layer((D, D)),
            pl.BlockSpec((1, M, 2 * D), lambda b, i: (b, 0, 0)),
            of_layer((D, D)),
        ],
        out_specs=pl.BlockSpec((1, tm, D), lambda b, i: (b, i, 0)),
        out_shape=jax.ShapeDtypeStruct((B, S, D), F32),
        compiler_params=_params("parallel", "parallel"),
        name="xattn",
    )(x, ya, yb, w_mix, w_mix, g.reshape(1, D), w_q, kv, w_o)


def kernel(x, mem, rel_bias, final_norm_g, ffn1_norm_g, ffn1_w_in, ffn1_w_out, mix_norm_g, w_mix_in, w_mix_out, rwkv_mu, rwkv_w0, rwkv_w_up, rwkv_a0, rwkv_a_up, rwkv_g_up, rwkv_k_k, rwkv_k_a, rwkv_r_k, rwkv_ln_g, rwkv_ln_b, xattn_norm_g, mem_norm_g, xattn_w_q, xattn_w_kv, xattn_w_o, ffn2_norm_g, ffn2_w_in, ffn2_w_out):
    B, S, D = x.shape
    M = mem.shape[1]
    depth = ffn1_w_in.shape[0]
    T = B * S
    ffn1_w_in, ffn1_w_out = ffn1_w_in.astype(BF16), ffn1_w_out.astype(BF16)

    bias_strip = _rel_bias_strip(rel_bias, S // MOBA_BLOCK)
    mem2 = mem.reshape(B * M, D)
    x = x.reshape(T, D)
    for l in range(depth):
        later = (w_mix_in, xattn_w_kv, w_mix_out, xattn_w_q, xattn_w_o, ffn2_w_in, ffn2_w_out)
        x, (mix_in16, kv16, mix_out16, q16, o16, ffn2_in16, ffn2_out16) = _ffn(
            x, ffn1_norm_g[l], ffn1_w_in, ffn1_w_out, l, side=[(w, l) for w in later])
        p_rwkv, qkv = _norm_proj(x, mix_norm_g[l], mix_in16, 0, (RWKV_PROJ, MOBA_PROJ),
                                 shift_mu=rwkv_mu[l], seq_len=S)
        y_rwkv = _rwkv(p_rwkv.reshape(B, S, RWKV_PROJ), rwkv_w0[l], rwkv_w_up[l],
                       rwkv_a0[l], rwkv_a_up[l], rwkv_g_up[l], rwkv_k_k[l], rwkv_k_a[l],
                       rwkv_r_k[l], rwkv_ln_g[l], rwkv_ln_b[l])
        y_moba = _moba(qkv.reshape(B, S, MOBA_PROJ), bias_strip)
        (kv,) = _norm_proj(mem2, mem_norm_g[l], kv16, 0, (2 * D,))
        x = _xattn(x.reshape(B, S, D), y_rwkv, y_moba, mix_out16, xattn_norm_g[l],
                   q16, kv.reshape(B, M, 2 * D), o16, 0).reshape(T, D)
        x, _ = _ffn(x, ffn2_norm_g[l], ffn2_in16, ffn2_out16, 0,
                    final_norm_g if l == depth - 1 else None)
    return x.reshape(B, S, D)
```

```python
import functools
import math

import jax
import jax.numpy as jnp
from jax import lax
from jax.experimental import pallas as pl
from jax.experimental.pallas import tpu as pltpu

F32 = jnp.float32
BF16 = jnp.bfloat16
HIGHEST = lax.Precision.HIGHEST

HEAD_DIM = 64
RWKV_WIDTH = 512
MOBA_WIDTH = 512
MOBA_HEADS = MOBA_WIDTH // HEAD_DIM
DECAY_LORA = 64
ICLR_LORA = 64
GATE_LORA = 128
RWKV_PROJ = 3 * RWKV_WIDTH + DECAY_LORA + ICLR_LORA + GATE_LORA
MOBA_PROJ = 3 * MOBA_WIDTH
LNX_EPS = 64e-5
MOBA_BLOCK = 256
MOBA_TOPK = 3
REL_BUCKETS = 32
REL_MAX_DISTANCE = 1024
XATTN_HEADS = 4
FFN_RES_WEIGHT = 0.5
NORM_EPS = 1e-6
LOG2E = math.log2(math.e)

LANES = 128
MXU_DIM = 256
HEADS_PER_TILE = LANES // HEAD_DIM
HEAD_SHIFT = HEAD_DIM.bit_length() - 1
RWKV_CHUNK = 64
CHUNK_SHIFT = RWKV_CHUNK.bit_length() - 1
RWKV_ROWS_IN_STEP = 2
MIXER_STAGE_RATIO = (3, 2)
SPLIT_LORA = (1, 1, 1)
SPLIT_SUM = (1, 1, 1)
SPLIT_CUMSUM = (1, 2, 2)
SPLIT_CHUNK = (1, 1, 1)
VMEM_LIMIT = 56 * 1024 * 1024


def _rms(x, g):
    ms = jnp.mean(x * x, axis=-1, keepdims=True)
    return x * lax.rsqrt(ms + NORM_EPS) * g


def _sigmoid(x):
    return 0.5 * jnp.tanh(0.5 * x) + 0.5


def _dot(a, b, precision=None):
    return jnp.dot(a, b, precision=precision, preferred_element_type=F32)


def _dot_nt(a, b, precision=None):
    return lax.dot_general(a, b, (((1,), (1,)), ((), ())), precision=precision,
                           preferred_element_type=F32)


def _dot_tn(a, b, precision=None):
    return lax.dot_general(a, b, (((0,), (0,)), ((), ())), precision=precision,
                           preferred_element_type=F32)


def _bf16_terms(x, n):
    terms = []
    for i in range(n):
        t = x.astype(BF16)
        terms.append(t)
        if i + 1 < n:
            x = x - t.astype(F32)
    return terms


def _mm(dot, a, b, split):
    na, nb, order = split
    at, bt = _bf16_terms(a, na), _bf16_terms(b, nb)
    acc = None
    for i in range(na):
        for j in range(nb):
            if i + j < order:
                t = dot(at[i], bt[j])
                acc = t if acc is None else acc + t
    return acc


def _params(*semantics):
    return pltpu.CompilerParams(dimension_semantics=semantics, vmem_limit_bytes=VMEM_LIMIT)


def _ffn_kernel(x_ref, g_ref, wi_ref, wo_ref, *rest, final_norm, sub, n_side):
    rest = list(rest)
    fg_ref = rest.pop(0) if final_norm else None
    side_in, o_ref, side_out = rest[:n_side], rest[n_side], rest[n_side + 1:]
    F = wo_ref.shape[0]
    for r0 in range(0, x_ref.shape[0], sub):
        x = x_ref[r0:r0 + sub, :]
        h = _rms(x, g_ref[...]).astype(BF16)
        gate = _dot(h, wi_ref[:, :F])
        up = _dot(h, wi_ref[:, F:])
        act = (gate * _sigmoid(gate) * up).astype(BF16)
        y = x + FFN_RES_WEIGHT * _dot(act, wo_ref[...])
        if final_norm:
            y = _rms(y, fg_ref[...])
        o_ref[r0:r0 + sub, :] = y
    for w_ref, w16_ref in zip(side_in, side_out):
        w16_ref[...] = w_ref[...].astype(BF16)


def _ffn(x, g, w_in, w_out, layer, final_g=None, side=(), *, tm=512, sub=256):
    T, D = x.shape
    F = w_out.shape[1]
    tm = min(tm, T)
    steps = T // tm
    resident = lambda shape: pl.BlockSpec(shape, lambda i: (0, 0), pipeline_mode=pl.Buffered(1))
    of_layer = lambda shape: pl.BlockSpec((None,) + shape, lambda i: (layer, 0, 0),
                                          pipeline_mode=pl.Buffered(1))
    in_specs = [
        pl.BlockSpec((tm, D), lambda i: (i, 0)),
        resident((1, D)),
        of_layer((D, 2 * F)),
        of_layer((F, D)),
    ]
    args = [x, g.reshape(1, D), w_in, w_out]
    if final_g is not None:
        in_specs.append(resident((1, D)))
        args.append(final_g.reshape(1, D))
    out_specs = [pl.BlockSpec((tm, D), lambda i: (i, 0))]
    out_shape = [jax.ShapeDtypeStruct((T, D), F32)]
    for w, w_layer in side:
        n_layers, rows, cols = w.shape
        assert rows % steps == 0
        slab = (rows // steps, cols)
        in_specs.append(pl.BlockSpec((None, None) + slab, lambda i, w_layer=w_layer: (w_layer, i, 0, 0)))
        args.append(w.reshape((n_layers, steps) + slab))
        out_specs.append(pl.BlockSpec((None,) + slab, lambda i: (i, 0, 0)))
        out_shape.append(jax.ShapeDtypeStruct((steps,) + slab, BF16))
    y, *side16 = pl.pallas_call(
        functools.partial(_ffn_kernel, final_norm=final_g is not None, sub=min(sub, tm), n_side=len(side)),
        grid=(steps,),
        in_specs=in_specs,
        out_specs=out_specs,
        out_shape=out_shape,
        compiler_params=_params("parallel"),
        name="ffn",
    )(*args)
    return y, [w16.reshape((1,) + w.shape[1:]) for w16, (w, _) in zip(side16, side)]


def _norm_proj_kernel(x_ref, g_ref, w_ref, *rest, splits, tiles_per_seq, sub):
    if tiles_per_seq:
        mu_ref, *o_refs, prev_ref = rest

        @pl.when(pl.program_id(0) % tiles_per_seq == 0)
        def _():
            prev_ref[...] = jnp.zeros_like(prev_ref)

        before = prev_ref[...]
    else:
        o_refs = rest
    row = lax.broadcasted_iota(jnp.int32, (sub, 1), 0)
    for r0 in range(0, x_ref.shape[0], sub):
        h = _rms(x_ref[r0:r0 + sub, :], g_ref[...]).astype(BF16)
        off = 0
        for idx, (o_ref, n) in enumerate(zip(o_refs, splits)):
            y = _dot(h, w_ref[:, off:off + n])
            if tiles_per_seq and idx == 0:
                shifted = jnp.where(row == 0, before, pltpu.roll(y, 1, axis=0))
                before = y[sub - 1:sub, :]
                y = y + (shifted - y) * mu_ref[...]
            o_ref[r0:r0 + sub, :] = y
            off += n
    if tiles_per_seq:
        prev_ref[...] = before


def _norm_proj(x, g, w, layer, splits, *, shift_mu=None, seq_len=None, tm=512, sub=128):
    T, D = x.shape
    N = w.shape[2]
    tm = min(tm, T)
    assert sum(splits) == N
    in_specs = [
        pl.BlockSpec((tm, D), lambda i: (i, 0)),
        pl.BlockSpec((1, D), lambda i: (0, 0)),
        pl.BlockSpec((None, D, N), lambda i: (layer, 0, 0)),
    ]
    args = [x, g.reshape(1, D), w]
    scratch, tiles_per_seq = [], 0
    if shift_mu is not None:
        assert seq_len % tm == 0
        tiles_per_seq = seq_len // tm
        in_specs.append(pl.BlockSpec((1, splits[0]), lambda i: (0, 0)))
        args.append(shift_mu.reshape(1, splits[0]))
        scratch = [pltpu.VMEM((1, splits[0]), F32)]
    return pl.pallas_call(
        functools.partial(_norm_proj_kernel, splits=splits, tiles_per_seq=tiles_per_seq,
                          sub=min(sub, tm)),
        grid=(T // tm,),
        in_specs=in_specs,
        out_specs=[pl.BlockSpec((tm, n), lambda i: (i, 0)) for n in splits],
        out_shape=[jax.ShapeDtypeStruct((T, n), F32) for n in splits],
        scratch_shapes=scratch,
        compiler_params=_params("arbitrary" if tiles_per_seq else "parallel"),
        name="norm_proj",
    )(*args)


def _rwkv_stages(p_ref, w0_ref, a0_ref, wwa_ref, gup_ref, kk_ref, ka_ref, rk_ref,
                 lng_ref, lnb_ref, o_ref, state_ref):
    C = RWKV_CHUNK
    W = RWKV_WIDTH
    NB, T, P = p_ref.shape
    n_pairs = W // LANES
    n_chunks = T // C

    @pl.when(pl.program_id(1) == 0)
    def _():
        state_ref[...] = jnp.zeros_like(state_ref)

    lane = lax.broadcasted_iota(jnp.int32, (1, LANES), 1)
    first_head = lane < HEAD_DIM
    ri = lax.broadcasted_iota(jnp.int32, (LANES, LANES), 0)
    ci = lax.broadcasted_iota(jnp.int32, (LANES, LANES), 1)
    same_head = (ri >> HEAD_SHIFT) == (ci >> HEAD_SHIFT)
    ri2 = lax.broadcasted_iota(jnp.int32, (MXU_DIM, MXU_DIM), 0)
    ci2 = lax.broadcasted_iota(jnp.int32, (MXU_DIM, MXU_DIM), 1)
    head_ones = ((ri2 >> HEAD_SHIFT) == (ci2 >> HEAD_SHIFT)).astype(F32)
    in_chunk_prefix = ((ci2 <= ri2) & ((ci2 >> CHUNK_SHIFT) == (ri2 >> CHUNK_SHIFT))).astype(F32)
    tok = lax.broadcasted_iota(jnp.int32, (C, LANES), 0)
    col_tok = lax.broadcasted_iota(jnp.int32, (C, LANES), 1) & (C - 1)
    strict = col_tok < tok
    incl = col_tok <= tok
    incl2 = jnp.concatenate([incl, incl], axis=1)
    eye = (col_tok == tok).astype(F32)
    lanes = [slice(i * LANES, (i + 1) * LANES) for i in range(n_pairs)]
    rows_of = lambda c: slice(c * C, (c + 1) * C)
    units = [(c, i) for c in range(n_chunks) for i in range(n_pairs)]
    c16 = lambda t: t.astype(BF16)

    def head_sum(t):
        return jnp.concatenate(
            [_mm(_dot, t[:, i * MXU_DIM:(i + 1) * MXU_DIM], head_ones, SPLIT_SUM)
             for i in range(W // MXU_DIM)],
            axis=1)

    def stack(t):
        zero = jnp.zeros_like(t)
        return jnp.concatenate([jnp.where(first_head, t, zero), jnp.where(first_head, zero, t)], axis=0)

    def tokenwise(nb, out):
        p = p_ref[nb]
        r, k, v = p[:, 0:W], p[:, W:2 * W], p[:, 2 * W:3 * W]
        lora = p[:, 3 * W:3 * W + LANES]
        g_lo = p[:, 3 * W + LANES:3 * W + 2 * LANES]
        z = jnp.where(first_head, jnp.tanh(lora), lora)
        wa = _mm(_dot, z, wwa_ref[...], SPLIT_LORA)
        g = _mm(_dot, _sigmoid(g_lo), gup_ref[...], SPLIT_LORA)
        yield
        logw = -math.exp(-0.5) * _sigmoid(w0_ref[...] + wa[:, :W])
        a = _sigmoid(a0_ref[...] + wa[:, W:])
        kk = k * kk_ref[...]
        k2 = k * (1.0 + (a - 1.0) * ka_ref[...])
        yield
        kkn = kk * lax.rsqrt(jnp.maximum(head_sum(kk * kk), 1e-24))
        b = kkn * a
        yield
        bonus = head_sum(r * k2 * rk_ref[...]) * v
        logp = jnp.concatenate(
            [_mm(_dot, in_chunk_prefix, logw[i * MXU_DIM:(i + 1) * MXU_DIM], SPLIT_CUMSUM)
             for i in range(T // MXU_DIM)], axis=0)
        yield
        inv = jnp.exp(-logp)
        out.update(a16=c16(-kkn * jnp.exp(logp - logw)), r16=c16(r * jnp.exp(logp)), v16=c16(v))
        yield
        out.update(b16=c16(b * inv), k16=c16(k2 * inv), b=b, k2=k2, logp=logp, bonus=bonus, g=g)

    def chunkwise(tw, out):
        a16, r16, b16, k16, v16 = tw["a16"], tw["r16"], tw["b16"], tw["k16"], tw["v16"]
        ar = {(c, i): jnp.concatenate([a16[rows_of(c), lanes[i]], r16[rows_of(c), lanes[i]]], axis=0)
              for c, i in units}
        bk = {(c, i): jnp.concatenate([stack(b16[rows_of(c), lanes[i]]), stack(k16[rows_of(c), lanes[i]])],
                                      axis=0) for c, i in units}
        v_s = {(c, i): stack(v16[rows_of(c), lanes[i]]) for c, i in units}
        yield
        gram = {u_: _dot_nt(ar[u_], bk[u_]) for u_ in units}
        yield
        l_ab = {u_: jnp.where(strict, gram[u_][:C, :LANES], 0.0) for u_ in units}
        a_ak = {u_: c16(jnp.where(strict, gram[u_][:C, LANES:], 0.0)) for u_ in units}
        a_r = {u_: c16(jnp.where(incl2, gram[u_][C:], 0.0)) for u_ in units}
        t_inv = {u_: eye + l_ab[u_] for u_ in units}
        lp16 = {u_: c16(l_ab[u_]) for u_ in units}
        yield
        l_pow = {u_: _dot(lp16[u_], stack(lp16[u_])) for u_ in units}
        av = {u_: _dot(a_ak[u_], v_s[u_]) for u_ in units}
        yield
        levels = int(math.log2(C)) - 1
        for level in range(levels - 1):
            lp16 = {u_: c16(l_pow[u_]) for u_ in units}
            both = {u_: _dot(jnp.concatenate([c16(t_inv[u_]), lp16[u_]], axis=0), stack(lp16[u_]))
                    for u_ in units}
            yield
            t_inv = {u_: t_inv[u_] + both[u_][:C] for u_ in units}
            l_pow = {u_: both[u_][C:] for u_ in units}
        last = {u_: _dot(c16(t_inv[u_]), stack(c16(l_pow[u_]))) for u_ in units}
        yield
        out.update(ar=ar, v_s=v_s, a_r=a_r, av=av, t16={u_: c16(t_inv[u_] + last[u_]) for u_ in units})

    def recurrent(nb, tw, cw, out):
        b, k2, logp, v16 = tw["b"], tw["k2"], tw["logp"], tw["v16"]
        ar, v_s, a_r, av, t16 = cw["ar"], cw["v_s"], cw["a_r"], cw["av"], cw["t16"]
        states = [state_ref[nb, i] for i in range(n_pairs)]
        y_chunks = []
        for c in range(n_chunks):
            logp_end = logp[(c + 1) * C - 1:(c + 1) * C, :]
            to_end = jnp.exp(logp_end - logp[rows_of(c)])
            bk_e = [c16(jnp.concatenate([b[rows_of(c), sl] * to_end[:, sl], k2[rows_of(c), sl] * to_end[:, sl]],
                                        axis=0)) for sl in lanes]
            from_state = [_dot_nt(ar[c, i], c16(states[i])) for i in range(n_pairs)]
            yield
            u16 = [c16(_dot(t16[c, i], stack(c16(from_state[i][:C] + av[c, i])))) for i in range(n_pairs)]
            yield
            outer = [_dot_tn(jnp.concatenate([u16[i], v16[rows_of(c), lanes[i]]], axis=0), bk_e[i])
                     for i in range(n_pairs)]
            y_chunks.append(jnp.concatenate(
                [from_state[i][C:] + _dot(a_r[c, i], jnp.concatenate([stack(u16[i]), v_s[c, i]], axis=0))
                 for i in range(n_pairs)], axis=1))
            yield
            decay_end = jnp.exp(logp_end)
            states = [states[i] * decay_end[:, lanes[i]] + jnp.where(same_head, outer[i], 0.0)
                      for i in range(n_pairs)]
        for i in range(n_pairs):
            state_ref[nb, i] = states[i]
        out.update(y=jnp.concatenate(y_chunks, axis=0))

    def finish(nb, tw, rc):
        y = rc["y"]
        mean = head_sum(y) * (1.0 / HEAD_DIM)
        yield
        yc = y - mean
        var = head_sum(yc * yc) * (1.0 / HEAD_DIM)
        yield
        y = yc * lax.rsqrt(var + LNX_EPS) * lng_ref[...] + lnb_ref[...]
        o_ref[nb] = (y + tw["bonus"]) * tw["g"]

    def in_turn(gens):
        while gens:
            gens = [gen for gen in gens if next(gen, gens) is not gens]
            yield

    tw = [dict() for _ in range(NB)]
    cw = [dict() for _ in range(NB)]
    rc = [dict() for _ in range(NB)]
    groups = [range(g0, min(g0 + RWKV_ROWS_IN_STEP, NB)) for g0 in range(0, NB, RWKV_ROWS_IN_STEP)]
    phases = [
        lambda rows: in_turn([tokenwise(nb, tw[nb]) for nb in rows]),
        lambda rows: in_turn([chunkwise(tw[nb], cw[nb]) for nb in rows]),
        lambda rows: in_turn([recurrent(nb, tw[nb], cw[nb], rc[nb]) for nb in rows]),
        lambda rows: in_turn([finish(nb, tw[nb], rc[nb]) for nb in rows]),
    ]
    for wave in range(len(groups) + len(phases) - 1):
        yield from in_turn([phases[wave - g](rows) for g, rows in enumerate(groups)
                            if 0 <= wave - g < len(phases)])


def _rwkv_kernel(*refs):
    for _ in _rwkv_stages(*refs):
        pass


def _rwkv_call_parts(p, w0, w_up, a0, a_up, g_up, k_k, k_a, r_k, ln_g, ln_b, ts, nb):
    B, S, P = p.shape
    W = RWKV_WIDTH
    wwa = jnp.zeros((LANES, 2 * W), F32)
    wwa = wwa.at[:DECAY_LORA, :W].set(w_up).at[DECAY_LORA:, W:].set(a_up)
    vec = lambda t: t.reshape(1, -1)
    row_spec = lambda n: pl.BlockSpec((1, n), lambda b, s: (0, 0))
    assert ts % MXU_DIM == 0 and S % ts == 0 and B % nb == 0
    args = (p, vec(w0), vec(a0), wwa, g_up, vec(k_k), vec(k_a), vec(r_k), vec(ln_g), vec(ln_b))
    in_specs = [
        pl.BlockSpec((nb, ts, P), lambda b, s: (b, s, 0)),
        row_spec(W), row_spec(W),
        pl.BlockSpec((LANES, 2 * W), lambda b, s: (0, 0)),
        pl.BlockSpec((GATE_LORA, W), lambda b, s: (0, 0)),
        row_spec(W), row_spec(W), row_spec(W), row_spec(W), row_spec(W),
    ]
    out_spec = pl.BlockSpec((nb, ts, W), lambda b, s: (b, s, 0))
    out_shape = jax.ShapeDtypeStruct((B, S, W), F32)
    scratch = [pltpu.VMEM((nb, W // LANES, LANES, LANES), F32)]
    return (B // nb, S // ts), args, in_specs, out_spec, out_shape, scratch


def _rwkv(p, *weights, ts=256, nb=2):
    B, S, _ = p.shape
    grid, args, in_specs, out_spec, out_shape, scratch = _rwkv_call_parts(p, *weights, min(ts, S), min(nb, B))
    return pl.pallas_call(
        _rwkv_kernel,
        grid=grid,
        in_specs=in_specs,
        out_specs=out_spec,
        out_shape=out_shape,
        scratch_shapes=scratch,
        compiler_params=_params("parallel", "arbitrary"),
        name="rwkv7",
    )(*args)


def _rel_bias_kernel(tab_ref, o_ref, *, n_blocks):
    blk = MOBA_BLOCK
    max_exact = REL_BUCKETS // 2

    def bucket_of(n):
        large = max_exact + math.floor(math.log(max(n, 1) / max_exact)
                                       / math.log(REL_MAX_DISTANCE / max_exact) * (REL_BUCKETS - max_exact))
        return n if n < max_exact else min(large, REL_BUCKETS - 1)

    def block(cb):
        kc = lax.broadcasted_iota(jnp.int32, (blk, blk), 0) + cb * blk
        qi = lax.broadcasted_iota(jnp.int32, (blk, blk), 1)
        dist = qi + (n_blocks - 1) * blk - kc
        n = jnp.maximum(dist, 0)
        nf = jnp.maximum(n, 1).astype(F32)
        large = max_exact + (jnp.log(nf / max_exact) / math.log(REL_MAX_DISTANCE / max_exact)
                             * (REL_BUCKETS - max_exact)).astype(jnp.int32)
        large = jnp.minimum(large, REL_BUCKETS - 1)
        bucket = jnp.where(n < max_exact, n, large)
        n_lo = max((n_blocks - 1 - cb) * blk - (blk - 1), 0)
        n_hi = max((n_blocks - 1 - cb) * blk + (blk - 1), 0)
        b_lo = max(bucket_of(n_lo) - 1, 0)
        b_hi = min(bucket_of(n_hi) + 1, REL_BUCKETS - 1)
        for h in range(MOBA_HEADS):
            tile = jnp.zeros((blk, blk), F32)
            for c in range(b_lo, b_hi + 1):
                tile = jnp.where(bucket == c, tab_ref[h, c], tile)
            o_ref[h] = jnp.where(dist < 0, -jnp.inf, tile * LOG2E)

    for cb in range(n_blocks):
        pl.when(pl.program_id(0) == cb)(functools.partial(block, cb))


def _rel_bias_strip(rel_bias, n_blocks):
    blk = MOBA_BLOCK
    return pl.pallas_call(
        functools.partial(_rel_bias_kernel, n_blocks=n_blocks),
        grid=(n_blocks,),
        in_specs=[pl.BlockSpec(memory_space=pltpu.SMEM)],
        out_specs=pl.BlockSpec((MOBA_HEADS, blk, blk), lambda c: (0, c, 0)),
        out_shape=jax.ShapeDtypeStruct((MOBA_HEADS, n_blocks * blk, blk), F32),
        compiler_params=_params("parallel"),
        name="rel_bias_strip",
    )(rel_bias)


def _moba_stages(q_ref, k_ref, v_ref, bias_ref, o_ref, *, n_blocks):
    blk = MOBA_BLOCK
    lane = lax.broadcasted_iota(jnp.int32, (1, LANES), 1)
    vrow = lax.broadcasted_iota(jnp.int32, (LANES, 1), 0)
    q = q_ref[0] * (HEAD_DIM ** -0.5 * LOG2E)
    k = k_ref[0]
    k16 = k.astype(BF16)
    vt = v_ref[0].T
    vt16 = [jnp.where((vrow >> HEAD_SHIFT) == e, vt, 1.0).astype(BF16) for e in range(HEADS_PER_TILE)]
    kmean = jnp.mean(k.reshape(n_blocks, blk, LANES), axis=1)

    gates, q16 = [], []
    for e in range(HEADS_PER_TILE):
        qe = jnp.where((lane >> HEAD_SHIFT) == e, q, 0.0)
        gates.append(_dot_nt(kmean, qe, HIGHEST))
        q16.append(qe.astype(BF16))

    def scores(qb, e):
        return _dot_nt(k16[:(qb + 1) * blk], q16[e][qb * blk:(qb + 1) * blk])

    def weights(qb, e, s_t):
        rows = slice(qb * blk, (qb + 1) * blk)
        bias_lo = (n_blocks - 1 - qb) * blk
        g = [gates[e][j:j + 1, rows] for j in range(qb)]
        tiles = []
        for j in range(qb + 1):
            t = s_t[j * blk:(j + 1) * blk] + bias_ref[e, bias_lo + j * blk:bias_lo + (j + 1) * blk, :]
            if j < qb:
                rank = jnp.zeros((1, blk), jnp.int32)
                for jj in range(qb):
                    if jj != j:
                        ahead = (g[jj] >= g[j]) if jj < j else (g[jj] > g[j])
                        rank = rank + jnp.where(ahead, 1, 0)
                t = jnp.where(rank < MOBA_TOPK, t, -jnp.inf)
            tiles.append(t)
        m = tiles[0].max(axis=0, keepdims=True)
        for t in tiles[1:]:
            m = jnp.maximum(m, t.max(axis=0, keepdims=True))
        return jnp.concatenate([jnp.exp2(t - m).astype(BF16) for t in tiles], axis=0)

    def attend(qb, e, p16):
        pv = _dot(vt16[e][:, :(qb + 1) * blk], p16)
        den_row = (1 - e) * HEAD_DIM
        return pv / pv[den_row:den_row + 1]

    units = [(qb, e) for qb in range(n_blocks) for e in range(HEADS_PER_TILE)]
    s_t, p16, out_t = {}, {}, {}
    for step in range(len(units) + 2):
        if step < len(units):
            s_t[units[step]] = scores(*units[step])
        if 0 <= step - 1 < len(units):
            u = units[step - 1]
            p16[u] = weights(*u, s_t.pop(u))
        if 0 <= step - 2 < len(units):
            u = units[step - 2]
            out_t[u] = attend(*u, p16.pop(u))
            qb, e = u
            if e == HEADS_PER_TILE - 1:
                o_ref[0, qb * blk:(qb + 1) * blk, :] = jnp.where(
                    vrow < HEAD_DIM, out_t.pop((qb, 0)), out_t.pop((qb, 1))).T
        yield


def _moba_kernel(*refs, n_blocks):
    for _ in _moba_stages(*refs, n_blocks=n_blocks):
        pass


def _moba(qkv, bias_strip):
    B, S, _ = qkv.shape
    blk = MOBA_BLOCK
    n_blocks = S // blk
    n_pairs = MOBA_WIDTH // LANES
    return pl.pallas_call(
        functools.partial(_moba_kernel, n_blocks=n_blocks),
        grid=(n_pairs, B),
        in_specs=[
            pl.BlockSpec((1, S, LANES), lambda p, b: (b, 0, p)),
            pl.BlockSpec((1, S, LANES), lambda p, b: (b, 0, n_pairs + p)),
            pl.BlockSpec((1, S, LANES), lambda p, b: (b, 0, 2 * n_pairs + p)),
            pl.BlockSpec((HEADS_PER_TILE, S, blk), lambda p, b: (p, 0, 0)),
        ],
        out_specs=pl.BlockSpec((1, S, LANES), lambda p, b: (b, 0, p)),
        out_shape=jax.ShapeDtypeStruct((B, S, MOBA_WIDTH), F32),
        compiler_params=_params("parallel", "parallel"),
        name="moba",
    )(qkv, qkv, qkv, bias_strip)


N_RWKV_INPUTS = 10
N_MOBA_INPUTS = 4


def _mixers_kernel(*refs, n_blocks):
    rwkv_in, refs = refs[:N_RWKV_INPUTS], refs[N_RWKV_INPUTS:]
    moba_in, (y_rwkv_ref, y_moba_ref, state_ref) = refs[:N_MOBA_INPUTS], refs[N_MOBA_INPUTS:]
    jobs = [(_rwkv_stages(*rwkv_in, y_rwkv_ref, state_ref), MIXER_STAGE_RATIO[0]),
            (_moba_stages(*moba_in, y_moba_ref, n_blocks=n_blocks), MIXER_STAGE_RATIO[1])]
    while jobs:
        jobs = [(gen, n) for gen, n in jobs if all(next(gen, jobs) is not jobs for _ in range(n))]


def _mixers(p, qkv, bias_strip, *rwkv_weights, ts=256, nb=2):
    B, S, _ = qkv.shape
    blk = MOBA_BLOCK
    n_blocks = S // blk
    n_pairs = MOBA_WIDTH // LANES
    ts, nb = min(ts, S), min(nb, B)
    grid, args, in_specs, out_spec, out_shape, scratch = _rwkv_call_parts(p, *rwkv_weights, ts, nb)
    if grid[0] * grid[1] != n_pairs * B:
        return _rwkv(p, *rwkv_weights, ts=ts, nb=nb), _moba(qkv, bias_strip)
    item = lambda g, s: g * grid[1] + s
    row = lambda g, s: item(g, s) % B
    pair = lambda g, s: item(g, s) // B
    in_specs = in_specs + [
        pl.BlockSpec((1, S, LANES), lambda g, s: (row(g, s), 0, pair(g, s))),
        pl.BlockSpec((1, S, LANES), lambda g, s: (row(g, s), 0, n_pairs + pair(g, s))),
        pl.BlockSpec((1, S, LANES), lambda g, s: (row(g, s), 0, 2 * n_pairs + pair(g, s))),
        pl.BlockSpec((HEADS_PER_TILE, S, blk), lambda g, s: (pair(g, s), 0, 0)),
    ]
    assert len(args) == N_RWKV_INPUTS
    return pl.pallas_call(
        functools.partial(_mixers_kernel, n_blocks=n_blocks),
        grid=grid,
        in_specs=in_specs,
        out_specs=[out_spec, pl.BlockSpec((1, S, LANES), lambda g, s: (row(g, s), 0, pair(g, s)))],
        out_shape=[out_shape, jax.ShapeDtypeStruct((B, S, MOBA_WIDTH), F32)],
        scratch_shapes=scratch,
        compiler_params=_params("parallel", "arbitrary"),
        name="mixers",
    )(*args, qkv, qkv, qkv, bias_strip)


def _xattn_kernel(x_ref, ya_ref, yb_ref, wa_ref, wb_ref, g_ref, wq_ref, kv_ref, wo_ref, o_ref, *, sub):
    D = x_ref.shape[-1]
    dh = D // XATTN_HEADS
    k16 = [kv_ref[0, :, i * dh:(i + 1) * dh].astype(BF16) for i in range(XATTN_HEADS)]
    v16 = [kv_ref[0, :, D + i * dh:D + (i + 1) * dh].astype(BF16) for i in range(XATTN_HEADS)]

    def project(r0):
        rows = slice(r0, r0 + sub)
        x = (x_ref[0, rows, :] + _dot(ya_ref[0, rows, :].astype(BF16), wa_ref[...])
             + _dot(yb_ref[0, rows, :].astype(BF16), wb_ref[...]))
        h = _rms(x, g_ref[...]).astype(BF16)
        q = (_dot(h, wq_ref[...]) * (dh ** -0.5 * LOG2E)).astype(BF16)
        return x, q

    def attend(q):
        hs = range(XATTN_HEADS)
        s = [_dot_nt(q[:, i * dh:(i + 1) * dh], k16[i]) for i in hs]
        pr = [jnp.exp2(s[i] - jnp.max(s[i], axis=-1, keepdims=True)) for i in hs]
        pr = [(pr[i] / jnp.sum(pr[i], axis=-1, keepdims=True)).astype(BF16) for i in hs]
        return jnp.concatenate([_dot(pr[i], v16[i]) for i in hs], axis=1).astype(BF16)

    starts = list(range(0, x_ref.shape[1], sub))
    xq, att = {}, {}
    for step in range(len(starts) + 2):
        if step < len(starts):
            xq[starts[step]] = project(starts[step])
        if 0 <= step - 1 < len(starts):
            r0 = starts[step - 1]
            att[r0] = attend(xq[r0][1])
        if 0 <= step - 2 < len(starts):
            r0 = starts[step - 2]
            o_ref[0, r0:r0 + sub, :] = xq.pop(r0)[0] + _dot(att.pop(r0), wo_ref[...])


def _xattn(x, ya, yb, w_mix, g, w_q, kv, w_o, layer, *, tm=512, sub=512):
    B, S, D = x.shape
    M = kv.shape[1]
    na, nb = ya.shape[-1], yb.shape[-1]
    assert na == nb
    tm = min(tm, S)
    of_layer = lambda shape, blk=0: pl.BlockSpec((None,) + shape, lambda b, i: (layer, blk, 0))
    return pl.pallas_call(
        functools.partial(_xattn_kernel, sub=min(sub, tm)),
        grid=(B, S // tm),
        in_specs=[
            pl.BlockSpec((1, tm, D), lambda b, i: (b, i, 0)),
            pl.BlockSpec((1, tm, na), lambda b, i: (b, i, 0)),
            pl.BlockSpec((1, tm, nb), lambda b, i: (b, i, 0)),
            of_layer((na, D), 0),
            of_layer((nb, D), 1),
            pl.BlockSpec((1, D), lambda b, i: (0, 0)),
            of_layer((D, D)),
            pl.BlockSpec((1, M, 2 * D), lambda b, i: (b, 0, 0)),
            of_layer((D, D)),
        ],
        out_specs=pl.BlockSpec((1, tm, D), lambda b, i: (b, i, 0)),
        out_shape=jax.ShapeDtypeStruct((B, S, D), F32),
        compiler_params=_params("parallel", "parallel"),
        name="xattn",
    )(x, ya, yb, w_mix, w_mix, g.reshape(1, D), w_q, kv, w_o)


def kernel(x, mem, rel_bias, final_norm_g, ffn1_norm_g, ffn1_w_in, ffn1_w_out, mix_norm_g, w_mix_in, w_mix_out, rwkv_mu, rwkv_w0, rwkv_w_up, rwkv_a0, rwkv_a_up, rwkv_g_up, rwkv_k_k, rwkv_k_a, rwkv_r_k, rwkv_ln_g, rwkv_ln_b, xattn_norm_g, mem_norm_g, xattn_w_q, xattn_w_kv, xattn_w_o, ffn2_norm_g, ffn2_w_in, ffn2_w_out):
    B, S, D = x.shape
    M = mem.shape[1]
    depth = ffn1_w_in.shape[0]
    T = B * S
    ffn1_w_in, ffn1_w_out = ffn1_w_in.astype(BF16), ffn1_w_out.astype(BF16)

    bias_strip = _rel_bias_strip(rel_bias, S // MOBA_BLOCK)
    mem2 = mem.reshape(B * M, D)
    x = x.reshape(T, D)
    for l in range(depth):
        later = (w_mix_in, xattn_w_kv, w_mix_out, xattn_w_q, xattn_w_o, ffn2_w_in, ffn2_w_out)
        x, (mix_in16, kv16, mix_out16, q16, o16, ffn2_in16, ffn2_out16) = _ffn(
            x, ffn1_norm_g[l], ffn1_w_in, ffn1_w_out, l, side=[(w, l) for w in later])
        p_rwkv, qkv = _norm_proj(x, mix_norm_g[l], mix_in16, 0, (RWKV_PROJ, MOBA_PROJ),
                                 shift_mu=rwkv_mu[l], seq_len=S)
        y_rwkv, y_moba = _mixers(p_rwkv.reshape(B, S, RWKV_PROJ), qkv.reshape(B, S, MOBA_PROJ), bias_strip,
                                 rwkv_w0[l], rwkv_w_up[l], rwkv_a0[l], rwkv_a_up[l], rwkv_g_up[l],
                                 rwkv_k_k[l], rwkv_k_a[l], rwkv_r_k[l], rwkv_ln_g[l], rwkv_ln_b[l])
        (kv,) = _norm_proj(mem2, mem_norm_g[l], kv16, 0, (2 * D,))
        x = _xattn(x.reshape(B, S, D), y_rwkv, y_moba, mix_out16, xattn_norm_g[l],
                   q16, kv.reshape(B, M, 2 * D), o16, 0).reshape(T, D)
        x, _ = _ffn(x, ffn2_norm_g[l], ffn2_in16, ffn2_out16, 0,
                    final_norm_g if l == depth - 1 else None)
    return x.reshape(B, S, D)
```

```python
import functools
import math

import jax
import jax.numpy as jnp
from jax import lax
from jax.experimental import pallas as pl
from jax.experimental.pallas import tpu as pltpu

F32 = jnp.float32
BF16 = jnp.bfloat16
HIGHEST = lax.Precision.HIGHEST

HEAD_DIM = 64
RWKV_WIDTH = 512
MOBA_WIDTH = 512
MOBA_HEADS = MOBA_WIDTH // HEAD_DIM
DECAY_LORA = 64
ICLR_LORA = 64
GATE_LORA = 128
RWKV_PROJ = 3 * RWKV_WIDTH + DECAY_LORA + ICLR_LORA + GATE_LORA
MOBA_PROJ = 3 * MOBA_WIDTH
LNX_EPS = 64e-5
MOBA_BLOCK = 256
MOBA_TOPK = 3
REL_BUCKETS = 32
REL_MAX_DISTANCE = 1024
XATTN_HEADS = 4
FFN_RES_WEIGHT = 0.5
NORM_EPS = 1e-6
LOG2E = math.log2(math.e)

LANES = 128
MXU_DIM = 256
HEADS_PER_TILE = LANES // HEAD_DIM
HEAD_SHIFT = HEAD_DIM.bit_length() - 1
RWKV_CHUNK = 64
CHUNK_SHIFT = RWKV_CHUNK.bit_length() - 1
RWKV_ROWS_IN_STEP = 2
MIXER_STAGE_RATIO = (3, 2)
SPLIT_LORA = (1, 1, 1)
SPLIT_SUM = (1, 1, 1)
SPLIT_CUMSUM = (1, 2, 2)
SPLIT_CHUNK = (1, 1, 1)
VMEM_LIMIT = 56 * 1024 * 1024


def _rms(x, g):
    ms = jnp.mean(x * x, axis=-1, keepdims=True)
    return x * lax.rsqrt(ms + NORM_EPS) * g


def _sigmoid(x):
    return 0.5 * jnp.tanh(0.5 * x) + 0.5


def _dot(a, b, precision=None):
    return jnp.dot(a, b, precision=precision, preferred_element_type=F32)


def _dot_nt(a, b, precision=None):
    return lax.dot_general(a, b, (((1,), (1,)), ((), ())), precision=precision,
                           preferred_element_type=F32)


def _dot_tn(a, b, precision=None):
    return lax.dot_general(a, b, (((0,), (0,)), ((), ())), precision=precision,
                           preferred_element_type=F32)


def _bf16_terms(x, n):
    terms = []
    for i in range(n):
        t = x.astype(BF16)
        terms.append(t)
        if i + 1 < n:
            x = x - t.astype(F32)
    return terms


def _mm(dot, a, b, split):
    na, nb, order = split
    at, bt = _bf16_terms(a, na), _bf16_terms(b, nb)
    acc = None
    for i in range(na):
        for j in range(nb):
            if i + j < order:
                t = dot(at[i], bt[j])
                acc = t if acc is None else acc + t
    return acc


def _params(*semantics):
    return pltpu.CompilerParams(dimension_semantics=semantics, vmem_limit_bytes=VMEM_LIMIT)


def _ffn_kernel(x_ref, g_ref, wi_ref, wo_ref, *rest, final_norm, sub, n_side):
    rest = list(rest)
    fg_ref = rest.pop(0) if final_norm else None
    side_in, o_ref, side_out = rest[:n_side], rest[n_side], rest[n_side + 1:]
    F = wo_ref.shape[0]
    for r0 in range(0, x_ref.shape[0], sub):
        x = x_ref[r0:r0 + sub, :]
        h = _rms(x, g_ref[...]).astype(BF16)
        gate = _dot(h, wi_ref[:, :F])
        up = _dot(h, wi_ref[:, F:])
        act = (gate * _sigmoid(gate) * up).astype(BF16)
        y = x + FFN_RES_WEIGHT * _dot(act, wo_ref[...])
        if final_norm:
            y = _rms(y, fg_ref[...])
        o_ref[r0:r0 + sub, :] = y
    for w_ref, w16_ref in zip(side_in, side_out):
        w16_ref[...] = w_ref[...].astype(BF16)


def _ffn(x, g, w_in, w_out, layer, final_g=None, side=(), *, tm=512, sub=256):
    T, D = x.shape
    F = w_out.shape[1]
    tm = min(tm, T)
    steps = T // tm
    resident = lambda shape: pl.BlockSpec(shape, lambda i: (0, 0), pipeline_mode=pl.Buffered(1))
    of_layer = lambda shape: pl.BlockSpec((None,) + shape, lambda i: (layer, 0, 0),
                                          pipeline_mode=pl.Buffered(1))
    in_specs = [
        pl.BlockSpec((tm, D), lambda i: (i, 0)),
        resident((1, D)),
        of_layer((D, 2 * F)),
        of_layer((F, D)),
    ]
    args = [x, g.reshape(1, D), w_in, w_out]
    if final_g is not None:
        in_specs.append(resident((1, D)))
        args.append(final_g.reshape(1, D))
    out_specs = [pl.BlockSpec((tm, D), lambda i: (i, 0))]
    out_shape = [jax.ShapeDtypeStruct((T, D), F32)]
    for w, w_layer in side:
        n_layers, rows, cols = w.shape
        assert rows % steps == 0
        slab = (rows // steps, cols)
        in_specs.append(pl.BlockSpec((None, None) + slab, lambda i, w_layer=w_layer: (w_layer, i, 0, 0)))
        args.append(w.reshape((n_layers, steps) + slab))
        out_specs.append(pl.BlockSpec((None,) + slab, lambda i: (i, 0, 0)))
        out_shape.append(jax.ShapeDtypeStruct((steps,) + slab, BF16))
    y, *side16 = pl.pallas_call(
        functools.partial(_ffn_kernel, final_norm=final_g is not None, sub=min(sub, tm), n_side=len(side)),
        grid=(steps,),
        in_specs=in_specs,
        out_specs=out_specs,
        out_shape=out_shape,
        compiler_params=_params("parallel"),
        name="ffn",
    )(*args)
    return y, [w16.reshape((1,) + w.shape[1:]) for w16, (w, _) in zip(side16, side)]


def _norm_proj_kernel(x_ref, g_ref, w_ref, *rest, splits, tiles_per_seq, sub):
    if tiles_per_seq:
        mu_ref, *o_refs, prev_ref = rest

        @pl.when(pl.program_id(0) % tiles_per_seq == 0)
        def _():
            prev_ref[...] = jnp.zeros_like(prev_ref)

        before = prev_ref[...]
    else:
        o_refs = rest
    row = lax.broadcasted_iota(jnp.int32, (sub, 1), 0)
    for r0 in range(0, x_ref.shape[0], sub):
        h = _rms(x_ref[r0:r0 + sub, :], g_ref[...]).astype(BF16)
        off = 0
        for idx, (o_ref, n) in enumerate(zip(o_refs, splits)):
            y = _dot(h, w_ref[:, off:off + n])
            if tiles_per_seq and idx == 0:
                shifted = jnp.where(row == 0, before, pltpu.roll(y, 1, axis=0))
                before = y[sub - 1:sub, :]
                y = y + (shifted - y) * mu_ref[...]
            o_ref[r0:r0 + sub, :] = y
            off += n
    if tiles_per_seq:
        prev_ref[...] = before


def _norm_proj(x, g, w, layer, splits, *, shift_mu=None, seq_len=None, tm=512, sub=128):
    T, D = x.shape
    N = w.shape[2]
    tm = min(tm, T)
    assert sum(splits) == N
    in_specs = [
        pl.BlockSpec((tm, D), lambda i: (i, 0)),
        pl.BlockSpec((1, D), lambda i: (0, 0)),
        pl.BlockSpec((None, D, N), lambda i: (layer, 0, 0)),
    ]
    args = [x, g.reshape(1, D), w]
    scratch, tiles_per_seq = [], 0
    if shift_mu is not None:
        assert seq_len % tm == 0
        tiles_per_seq = seq_len // tm
        in_specs.append(pl.BlockSpec((1, splits[0]), lambda i: (0, 0)))
        args.append(shift_mu.reshape(1, splits[0]))
        scratch = [pltpu.VMEM((1, splits[0]), F32)]
    return pl.pallas_call(
        functools.partial(_norm_proj_kernel, splits=splits, tiles_per_seq=tiles_per_seq,
                          sub=min(sub, tm)),
        grid=(T // tm,),
        in_specs=in_specs,
        out_specs=[pl.BlockSpec((tm, n), lambda i: (i, 0)) for n in splits],
        out_shape=[jax.ShapeDtypeStruct((T, n), F32) for n in splits],
        scratch_shapes=scratch,
        compiler_params=_params("arbitrary" if tiles_per_seq else "parallel"),
        name="norm_proj",
    )(*args)


def _rwkv_stages(p_ref, w0_ref, a0_ref, wwa_ref, gup_ref, kk_ref, ka_ref, rk_ref,
                 lng_ref, lnb_ref, o_ref, state_ref):
    C = RWKV_CHUNK
    W = RWKV_WIDTH
    NB, T, P = p_ref.shape
    n_pairs = W // LANES
    n_chunks = T // C

    @pl.when(pl.program_id(1) == 0)
    def _():
        state_ref[...] = jnp.zeros_like(state_ref)

    lane = lax.broadcasted_iota(jnp.int32, (1, LANES), 1)
    first_head = lane < HEAD_DIM
    ri = lax.broadcasted_iota(jnp.int32, (LANES, LANES), 0)
    ci = lax.broadcasted_iota(jnp.int32, (LANES, LANES), 1)
    same_head = (ri >> HEAD_SHIFT) == (ci >> HEAD_SHIFT)
    ri2 = lax.broadcasted_iota(jnp.int32, (MXU_DIM, MXU_DIM), 0)
    ci2 = lax.broadcasted_iota(jnp.int32, (MXU_DIM, MXU_DIM), 1)
    head_ones = ((ri2 >> HEAD_SHIFT) == (ci2 >> HEAD_SHIFT)).astype(F32)
    in_chunk_prefix = ((ci2 <= ri2) & ((ci2 >> CHUNK_SHIFT) == (ri2 >> CHUNK_SHIFT))).astype(F32)
    tok = lax.broadcasted_iota(jnp.int32, (C, LANES), 0)
    col_tok = lax.broadcasted_iota(jnp.int32, (C, LANES), 1) & (C - 1)
    strict = col_tok < tok
    incl = col_tok <= tok
    incl2 = jnp.concatenate([incl, incl], axis=1)
    eye = (col_tok == tok).astype(F32)
    lanes = [slice(i * LANES, (i + 1) * LANES) for i in range(n_pairs)]
    rows_of = lambda c: slice(c * C, (c + 1) * C)
    units = [(c, i) for c in range(n_chunks) for i in range(n_pairs)]
    c16 = lambda t: t.astype(BF16)

    def head_sum(t):
        return jnp.concatenate(
            [_mm(_dot, t[:, i * MXU_DIM:(i + 1) * MXU_DIM], head_ones, SPLIT_SUM)
             for i in range(W // MXU_DIM)],
            axis=1)

    def stack(t):
        zero = jnp.zeros_like(t)
        return jnp.concatenate([jnp.where(first_head, t, zero), jnp.where(first_head, zero, t)], axis=0)

    def tokenwise(nb, out):
        p = p_ref[nb]
        r, k, v = p[:, 0:W], p[:, W:2 * W], p[:, 2 * W:3 * W]
        lora = p[:, 3 * W:3 * W + LANES]
        g_lo = p[:, 3 * W + LANES:3 * W + 2 * LANES]
        z = jnp.where(first_head, jnp.tanh(lora), lora)
        wa = _mm(_dot, z, wwa_ref[...], SPLIT_LORA)
        g = _mm(_dot, _sigmoid(g_lo), gup_ref[...], SPLIT_LORA)
        yield
        logw = -math.exp(-0.5) * _sigmoid(w0_ref[...] + wa[:, :W])
        a = _sigmoid(a0_ref[...] + wa[:, W:])
        kk = k * kk_ref[...]
        k2 = k * (1.0 + (a - 1.0) * ka_ref[...])
        yield
        kkn = kk * lax.rsqrt(jnp.maximum(head_sum(kk * kk), 1e-24))
        b = kkn * a
        yield
        bonus = head_sum(r * k2 * rk_ref[...]) * v
        logp = jnp.concatenate(
            [_mm(_dot, in_chunk_prefix, logw[i * MXU_DIM:(i + 1) * MXU_DIM], SPLIT_CUMSUM)
             for i in range(T // MXU_DIM)], axis=0)
        yield
        inv = jnp.exp(-logp)
        out.update(a16=c16(-kkn * jnp.exp(logp - logw)), r16=c16(r * jnp.exp(logp)), v16=c16(v))
        yield
        out.update(b16=c16(b * inv), k16=c16(k2 * inv), b=b, k2=k2, logp=logp, bonus=bonus, g=g)

    def chunkwise(tw, out):
        a16, r16, b16, k16, v16 = tw["a16"], tw["r16"], tw["b16"], tw["k16"], tw["v16"]
        ar = {(c, i): jnp.concatenate([a16[rows_of(c), lanes[i]], r16[rows_of(c), lanes[i]]], axis=0)
              for c, i in units}
        bk = {(c, i): jnp.concatenate([stack(b16[rows_of(c), lanes[i]]), stack(k16[rows_of(c), lanes[i]])],
                                      axis=0) for c, i in units}
        v_s = {(c, i): stack(v16[rows_of(c), lanes[i]]) for c, i in units}
        yield
        gram = {u_: _dot_nt(ar[u_], bk[u_]) for u_ in units}
        yield
        l_ab = {u_: jnp.where(strict, gram[u_][:C, :LANES], 0.0) for u_ in units}
        a_ak = {u_: c16(jnp.where(strict, gram[u_][:C, LANES:], 0.0)) for u_ in units}
        a_r = {u_: c16(jnp.where(incl2, gram[u_][C:], 0.0)) for u_ in units}
        t_inv = {u_: eye + l_ab[u_] for u_ in units}
        lp16 = {u_: c16(l_ab[u_]) for u_ in units}
        yield
        l_pow = {u_: _dot(lp16[u_], stack(lp16[u_])) for u_ in units}
        av = {u_: _dot(a_ak[u_], v_s[u_]) for u_ in units}
        yield
        levels = int(math.log2(C)) - 1
        for level in range(levels - 1):
            lp16 = {u_: c16(l_pow[u_]) for u_ in units}
            both = {u_: _dot(jnp.concatenate([c16(t_inv[u_]), lp16[u_]], axis=0), stack(lp16[u_]))
                    for u_ in units}
            yield
            t_inv = {u_: t_inv[u_] + both[u_][:C] for u_ in units}
            l_pow = {u_: both[u_][C:] for u_ in units}
        last = {u_: _dot(c16(t_inv[u_]), stack(c16(l_pow[u_]))) for u_ in units}
        yield
        out.update(ar=ar, v_s=v_s, a_r=a_r, av=av, t16={u_: c16(t_inv[u_] + last[u_]) for u_ in units})

    def recurrent(nb, tw, cw, out):
        b, k2, logp, v16 = tw["b"], tw["k2"], tw["logp"], tw["v16"]
        ar, v_s, a_r, av, t16 = cw["ar"], cw["v_s"], cw["a_r"], cw["av"], cw["t16"]
        states = [state_ref[nb, i] for i in range(n_pairs)]
        y_chunks = []
        for c in range(n_chunks):
            logp_end = logp[(c + 1) * C - 1:(c + 1) * C, :]
            to_end = jnp.exp(logp_end - logp[rows_of(c)])
            bk_e = [c16(jnp.concatenate([b[rows_of(c), sl] * to_end[:, sl], k2[rows_of(c), sl] * to_end[:, sl]],
                                        axis=0)) for sl in lanes]
            from_state = [_dot_nt(ar[c, i], c16(states[i])) for i in range(n_pairs)]
            yield
            u16 = [c16(_dot(t16[c, i], stack(c16(from_state[i][:C] + av[c, i])))) for i in range(n_pairs)]
            yield
            outer = [_dot_tn(jnp.concatenate([u16[i], v16[rows_of(c), lanes[i]]], axis=0), bk_e[i])
                     for i in range(n_pairs)]
            y_chunks.append(jnp.concatenate(
                [from_state[i][C:] + _dot(a_r[c, i], jnp.concatenate([stack(u16[i]), v_s[c, i]], axis=0))
                 for i in range(n_pairs)], axis=1))
            yield
            decay_end = jnp.exp(logp_end)
            states = [states[i] * decay_end[:, lanes[i]] + jnp.where(same_head, outer[i], 0.0)
                      for i in range(n_pairs)]
        for i in range(n_pairs):
            state_ref[nb, i] = states[i]
        out.update(y=jnp.concatenate(y_chunks, axis=0))

    def finish(nb, tw, rc):
        y = rc["y"]
        mean = head_sum(y) * (1.0 / HEAD_DIM)
        yield
        yc = y - mean
        var = head_sum(yc * yc) * (1.0 / HEAD_DIM)
        yield
        y = yc * lax.rsqrt(var + LNX_EPS) * lng_ref[...] + lnb_ref[...]
        o_ref[nb] = (y + tw["bonus"]) * tw["g"]

    def in_turn(gens):
        while gens:
            gens = [gen for gen in gens if next(gen, gens) is not gens]
            yield

    tw = [dict() for _ in range(NB)]
    cw = [dict() for _ in range(NB)]
    rc = [dict() for _ in range(NB)]
    groups = [range(g0, min(g0 + RWKV_ROWS_IN_STEP, NB)) for g0 in range(0, NB, RWKV_ROWS_IN_STEP)]
    phases = [
        lambda rows: in_turn([tokenwise(nb, tw[nb]) for nb in rows]),
        lambda rows: in_turn([chunkwise(tw[nb], cw[nb]) for nb in rows]),
        lambda rows: in_turn([recurrent(nb, tw[nb], cw[nb], rc[nb]) for nb in rows]),
        lambda rows: in_turn([finish(nb, tw[nb], rc[nb]) for nb in rows]),
    ]
    for wave in range(len(groups) + len(phases) - 1):
        yield from in_turn([phases[wave - g](rows) for g, rows in enumerate(groups)
                            if 0 <= wave - g < len(phases)])


def _rwkv_kernel(*refs):
    for _ in _rwkv_stages(*refs):
        pass


def _rwkv_call_parts(p, w0, w_up, a0, a_up, g_up, k_k, k_a, r_k, ln_g, ln_b, ts, nb):
    B, S, P = p.shape
    W = RWKV_WIDTH
    wwa = jnp.zeros((LANES, 2 * W), F32)
    wwa = wwa.at[:DECAY_LORA, :W].set(w_up).at[DECAY_LORA:, W:].set(a_up)
    vec = lambda t: t.reshape(1, -1)
    row_spec = lambda n: pl.BlockSpec((1, n), lambda b, s: (0, 0))
    assert ts % MXU_DIM == 0 and S % ts == 0 and B % nb == 0
    args = (p, vec(w0), vec(a0), wwa, g_up, vec(k_k), vec(k_a), vec(r_k), vec(ln_g), vec(ln_b))
    in_specs = [
        pl.BlockSpec((nb, ts, P), lambda b, s: (b, s, 0)),
        row_spec(W), row_spec(W),
        pl.BlockSpec((LANES, 2 * W), lambda b, s: (0, 0)),
        pl.BlockSpec((GATE_LORA, W), lambda b, s: (0, 0)),
        row_spec(W), row_spec(W), row_spec(W), row_spec(W), row_spec(W),
    ]
    out_spec = pl.BlockSpec((nb, ts, W), lambda b, s: (b, s, 0))
    out_shape = jax.ShapeDtypeStruct((B, S, W), F32)
    scratch = [pltpu.VMEM((nb, W // LANES, LANES, LANES), F32)]
    return (B // nb, S // ts), args, in_specs, out_spec, out_shape, scratch


def _rwkv(p, *weights, ts=256, nb=2):
    B, S, _ = p.shape
    grid, args, in_specs, out_spec, out_shape, scratch = _rwkv_call_parts(p, *weights, min(ts, S), min(nb, B))
    return pl.pallas_call(
        _rwkv_kernel,
        grid=grid,
        in_specs=in_specs,
        out_specs=out_spec,
        out_shape=out_shape,
        scratch_shapes=scratch,
        compiler_params=_params("parallel", "arbitrary"),
        name="rwkv7",
    )(*args)


def _rel_bias_kernel(tab_ref, o_ref, *, n_blocks):
    blk = MOBA_BLOCK
    max_exact = REL_BUCKETS // 2

    def bucket_of(n):
        large = max_exact + math.floor(math.log(max(n, 1) / max_exact)
                                       / math.log(REL_MAX_DISTANCE / max_exact) * (REL_BUCKETS - max_exact))
        return n if n < max_exact else min(large, REL_BUCKETS - 1)

    def block(cb):
        kc = lax.broadcasted_iota(jnp.int32, (blk, blk), 0) + cb * blk
        qi = lax.broadcasted_iota(jnp.int32, (blk, blk), 1)
        dist = qi + (n_blocks - 1) * blk - kc
        n = jnp.maximum(dist, 0)
        nf = jnp.maximum(n, 1).astype(F32)
        large = max_exact + (jnp.log(nf / max_exact) / math.log(REL_MAX_DISTANCE / max_exact)
                             * (REL_BUCKETS - max_exact)).astype(jnp.int32)
        large = jnp.minimum(large, REL_BUCKETS - 1)
        bucket = jnp.where(n < max_exact, n, large)
        n_lo = max((n_blocks - 1 - cb) * blk - (blk - 1), 0)
        n_hi = max((n_blocks - 1 - cb) * blk + (blk - 1), 0)
        b_lo = max(bucket_of(n_lo) - 1, 0)
        b_hi = min(bucket_of(n_hi) + 1, REL_BUCKETS - 1)
        for h in range(MOBA_HEADS):
            tile = jnp.zeros((blk, blk), F32)
            for c in range(b_lo, b_hi + 1):
                tile = jnp.where(bucket == c, tab_ref[h, c], tile)
            o_ref[h] = jnp.where(dist < 0, -jnp.inf, tile * LOG2E)

    for cb in range(n_blocks):
        pl.when(pl.program_id(0) == cb)(functools.partial(block, cb))


def _rel_bias_strip(rel_bias, n_blocks):
    blk = MOBA_BLOCK
    return pl.pallas_call(
        functools.partial(_rel_bias_kernel, n_blocks=n_blocks),
        grid=(n_blocks,),
        in_specs=[pl.BlockSpec(memory_space=pltpu.SMEM)],
        out_specs=pl.BlockSpec((MOBA_HEADS, blk, blk), lambda c: (0, c, 0)),
        out_shape=jax.ShapeDtypeStruct((MOBA_HEADS, n_blocks * blk, blk), F32),
        compiler_params=_params("parallel"),
        name="rel_bias_strip",
    )(rel_bias)


def _moba_stages(q_ref, k_ref, v_ref, bias_ref, o_ref, *, n_blocks):
    blk = MOBA_BLOCK
    lane = lax.broadcasted_iota(jnp.int32, (1, LANES), 1)
    vrow = lax.broadcasted_iota(jnp.int32, (LANES, 1), 0)
    q = q_ref[0] * (HEAD_DIM ** -0.5 * LOG2E)
    k = k_ref[0]
    k16 = k.astype(BF16)
    vt = v_ref[0].T
    vt16 = [jnp.where((vrow >> HEAD_SHIFT) == e, vt, 1.0).astype(BF16) for e in range(HEADS_PER_TILE)]
    kmean = jnp.mean(k.reshape(n_blocks, blk, LANES), axis=1)

    gates, q16 = [], []
    for e in range(HEADS_PER_TILE):
        qe = jnp.where((lane >> HEAD_SHIFT) == e, q, 0.0)
        gates.append(_dot_nt(kmean, qe, HIGHEST))
        q16.append(qe.astype(BF16))

    def scores(qb, e):
        return _dot_nt(k16[:(qb + 1) * blk], q16[e][qb * blk:(qb + 1) * blk])

    def weights(qb, e, s_t):
        rows = slice(qb * blk, (qb + 1) * blk)
        bias_lo = (n_blocks - 1 - qb) * blk
        g = [gates[e][j:j + 1, rows] for j in range(qb)]
        tiles = []
        for j in range(qb + 1):
            t = s_t[j * blk:(j + 1) * blk] + bias_ref[e, bias_lo + j * blk:bias_lo + (j + 1) * blk, :]
            if j < qb:
                rank = jnp.zeros((1, blk), jnp.int32)
                for jj in range(qb):
                    if jj != j:
                        ahead = (g[jj] >= g[j]) if jj < j else (g[jj] > g[j])
                        rank = rank + jnp.where(ahead, 1, 0)
                t = jnp.where(rank < MOBA_TOPK, t, -jnp.inf)
            tiles.append(t)
        m = tiles[0].max(axis=0, keepdims=True)
        for t in tiles[1:]:
            m = jnp.maximum(m, t.max(axis=0, keepdims=True))
        return jnp.concatenate([jnp.exp2(t - m).astype(BF16) for t in tiles], axis=0)

    def attend(qb, e, p16):
        pv = _dot(vt16[e][:, :(qb + 1) * blk], p16)
        den_row = (1 - e) * HEAD_DIM
        return pv / pv[den_row:den_row + 1]

    units = [(qb, e) for qb in range(n_blocks) for e in range(HEADS_PER_TILE)]
    s_t, p16, out_t = {}, {}, {}
    for step in range(len(units) + 2):
        if step < len(units):
            s_t[units[step]] = scores(*units[step])
        if 0 <= step - 1 < len(units):
            u = units[step - 1]
            p16[u] = weights(*u, s_t.pop(u))
        if 0 <= step - 2 < len(units):
            u = units[step - 2]
            out_t[u] = attend(*u, p16.pop(u))
            qb, e = u
            if e == HEADS_PER_TILE - 1:
                o_ref[0, qb * blk:(qb + 1) * blk, :] = jnp.where(
                    vrow < HEAD_DIM, out_t.pop((qb, 0)), out_t.pop((qb, 1))).T
        yield


def _moba_kernel(*refs, n_blocks):
    for _ in _moba_stages(*refs, n_blocks=n_blocks):
        pass


def _moba(qkv, bias_strip):
    B, S, _ = qkv.shape
    blk = MOBA_BLOCK
    n_blocks = S // blk
    n_pairs = MOBA_WIDTH // LANES
    return pl.pallas_call(
        functools.partial(_moba_kernel, n_blocks=n_blocks),
        grid=(n_pairs, B),
        in_specs=[
            pl.BlockSpec((1, S, LANES), lambda p, b: (b, 0, p)),
            pl.BlockSpec((1, S, LANES), lambda p, b: (b, 0, n_pairs + p)),
            pl.BlockSpec((1, S, LANES), lambda p, b: (b, 0, 2 * n_pairs + p)),
            pl.BlockSpec((HEADS_PER_TILE, S, blk), lambda p, b: (p, 0, 0)),
        ],
        out_specs=pl.BlockSpec((1, S, LANES), lambda p, b: (b, 0, p)),
        out_shape=jax.ShapeDtypeStruct((B, S, MOBA_WIDTH), F32),
        compiler_params=_params("parallel", "parallel"),
        name="moba",
    )(qkv, qkv, qkv, bias_strip)


N_RWKV_INPUTS = 10
N_MOBA_INPUTS = 4


def _mixers_kernel(*refs, n_blocks):
    rwkv_in, refs = refs[:N_RWKV_INPUTS], refs[N_RWKV_INPUTS:]
    moba_in, (y_rwkv_ref, y_moba_ref, state_ref) = refs[:N_MOBA_INPUTS], refs[N_MOBA_INPUTS:]
    jobs = [(_rwkv_stages(*rwkv_in, y_rwkv_ref, state_ref), MIXER_STAGE_RATIO[0]),
            (_moba_stages(*moba_in, y_moba_ref, n_blocks=n_blocks), MIXER_STAGE_RATIO[1])]
    while jobs:
        jobs = [(gen, n) for gen, n in jobs if all(next(gen, jobs) is not jobs for _ in range(n))]


def _mixers(p, qkv, bias_strip, *rwkv_weights, ts=256, nb=2):
    B, S, _ = qkv.shape
    blk = MOBA_BLOCK
    n_blocks = S // blk
    n_pairs = MOBA_WIDTH // LANES
    ts, nb = min(ts, S), min(nb, B)
    grid, args, in_specs, out_spec, out_shape, scratch = _rwkv_call_parts(p, *rwkv_weights, ts, nb)
    if grid[0] * grid[1] != n_pairs * B:
        return _rwkv(p, *rwkv_weights, ts=ts, nb=nb), _moba(qkv, bias_strip)
    item = lambda g, s: g * grid[1] + s
    row = lambda g, s: item(g, s) % B
    pair = lambda g, s: item(g, s) // B
    in_specs = in_specs + [
        pl.BlockSpec((1, S, LANES), lambda g, s: (row(g, s), 0, pair(g, s))),
        pl.BlockSpec((1, S, LANES), lambda g, s: (row(g, s), 0, n_pairs + pair(g, s))),
        pl.BlockSpec((1, S, LANES), lambda g, s: (row(g, s), 0, 2 * n_pairs + pair(g, s))),
        pl.BlockSpec((HEADS_PER_TILE, S, blk), lambda g, s: (pair(g, s), 0, 0)),
    ]
    assert len(args) == N_RWKV_INPUTS
    return pl.pallas_call(
        functools.partial(_mixers_kernel, n_blocks=n_blocks),
        grid=grid,
        in_specs=in_specs,
        out_specs=[out_spec, pl.BlockSpec((1, S, LANES), lambda g, s: (row(g, s), 0, pair(g, s)))],
        out_shape=[out_shape, jax.ShapeDtypeStruct((B, S, MOBA_WIDTH), F32)],
        scratch_shapes=scratch,
        compiler_params=_params("parallel", "arbitrary"),
        name="mixers",
    )(*args, qkv, qkv, qkv, bias_strip)


def _xattn_kernel(x_ref, ya_ref, yb_ref, wa_ref, wb_ref, g_ref, wq_ref, mem_ref, mg_ref, wkv_ref, wo_ref,
                  o_ref, kv_ref, *, sub):
    D = x_ref.shape[-1]
    dh = D // XATTN_HEADS

    @pl.when(pl.program_id(1) == 0)
    def _():
        kv_ref[...] = _dot(_rms(mem_ref[0], mg_ref[...]).astype(BF16), wkv_ref[...]).astype(BF16)

    k16 = [kv_ref[:, i * dh:(i + 1) * dh] for i in range(XATTN_HEADS)]
    v16 = [kv_ref[:, D + i * dh:D + (i + 1) * dh] for i in range(XATTN_HEADS)]

    def project(r0):
        rows = slice(r0, r0 + sub)
        x = (x_ref[0, rows, :] + _dot(ya_ref[0, rows, :].astype(BF16), wa_ref[...])
             + _dot(yb_ref[0, rows, :].astype(BF16), wb_ref[...]))
        h = _rms(x, g_ref[...]).astype(BF16)
        q = (_dot(h, wq_ref[...]) * (dh ** -0.5 * LOG2E)).astype(BF16)
        return x, q

    def attend(q):
        hs = range(XATTN_HEADS)
        s = [_dot_nt(q[:, i * dh:(i + 1) * dh], k16[i]) for i in hs]
        pr = [jnp.exp2(s[i] - jnp.max(s[i], axis=-1, keepdims=True)) for i in hs]
        pr = [(pr[i] / jnp.sum(pr[i], axis=-1, keepdims=True)).astype(BF16) for i in hs]
        return jnp.concatenate([_dot(pr[i], v16[i]) for i in hs], axis=1).astype(BF16)

    starts = list(range(0, x_ref.shape[1], sub))
    xq, att = {}, {}
    for step in range(len(starts) + 2):
        if step < len(starts):
            xq[starts[step]] = project(starts[step])
        if 0 <= step - 1 < len(starts):
            r0 = starts[step - 1]
            att[r0] = attend(xq[r0][1])
        if 0 <= step - 2 < len(starts):
            r0 = starts[step - 2]
            o_ref[0, r0:r0 + sub, :] = xq.pop(r0)[0] + _dot(att.pop(r0), wo_ref[...])


def _xattn(x, ya, yb, w_mix, g, w_q, mem, mem_g, w_kv, w_o, layer, *, tm=512, sub=512):
    B, S, D = x.shape
    M = mem.shape[1]
    na, nb = ya.shape[-1], yb.shape[-1]
    assert na == nb
    tm = min(tm, S)
    of_layer = lambda shape, blk=0: pl.BlockSpec((None,) + shape, lambda b, i: (layer, blk, 0))
    return pl.pallas_call(
        functools.partial(_xattn_kernel, sub=min(sub, tm)),
        grid=(B, S // tm),
        in_specs=[
            pl.BlockSpec((1, tm, D), lambda b, i: (b, i, 0)),
            pl.BlockSpec((1, tm, na), lambda b, i: (b, i, 0)),
            pl.BlockSpec((1, tm, nb), lambda b, i: (b, i, 0)),
            of_layer((na, D), 0),
            of_layer((nb, D), 1),
            pl.BlockSpec((1, D), lambda b, i: (0, 0)),
            of_layer((D, D)),
            pl.BlockSpec((1, M, D), lambda b, i: (b, 0, 0)),
            pl.BlockSpec((1, D), lambda b, i: (0, 0)),
            of_layer((D, 2 * D)),
            of_layer((D, D)),
        ],
        out_specs=pl.BlockSpec((1, tm, D), lambda b, i: (b, i, 0)),
        out_shape=jax.ShapeDtypeStruct((B, S, D), F32),
        scratch_shapes=[pltpu.VMEM((M, 2 * D), BF16)],
        compiler_params=_params("parallel", "arbitrary"),
        name="xattn",
    )(x, ya, yb, w_mix, w_mix, g.reshape(1, D), w_q, mem, mem_g.reshape(1, D), w_kv, w_o)


def kernel(x, mem, rel_bias, final_norm_g, ffn1_norm_g, ffn1_w_in, ffn1_w_out, mix_norm_g, w_mix_in, w_mix_out, rwkv_mu, rwkv_w0, rwkv_w_up, rwkv_a0, rwkv_a_up, rwkv_g_up, rwkv_k_k, rwkv_k_a, rwkv_r_k, rwkv_ln_g, rwkv_ln_b, xattn_norm_g, mem_norm_g, xattn_w_q, xattn_w_kv, xattn_w_o, ffn2_norm_g, ffn2_w_in, ffn2_w_out):
    B, S, D = x.shape
    M = mem.shape[1]
    depth = ffn1_w_in.shape[0]
    T = B * S
    ffn1_in16, ffn1_out16 = ffn1_w_in[:1].astype(BF16), ffn1_w_out[:1].astype(BF16)

    bias_strip = _rel_bias_strip(rel_bias, S // MOBA_BLOCK)
    x = x.reshape(T, D)
    for l in range(depth):
        later = (w_mix_in, xattn_w_kv, w_mix_out, xattn_w_q, xattn_w_o, ffn2_w_in, ffn2_w_out)
        x, (mix_in16, kv16, mix_out16, q16, o16, ffn2_in16, ffn2_out16) = _ffn(
            x, ffn1_norm_g[l], ffn1_in16, ffn1_out16, 0, side=[(w, l) for w in later])
        p_rwkv, qkv = _norm_proj(x, mix_norm_g[l], mix_in16, 0, (RWKV_PROJ, MOBA_PROJ),
                                 shift_mu=rwkv_mu[l], seq_len=S)
        y_rwkv, y_moba = _mixers(p_rwkv.reshape(B, S, RWKV_PROJ), qkv.reshape(B, S, MOBA_PROJ), bias_strip,
                                 rwkv_w0[l], rwkv_w_up[l], rwkv_a0[l], rwkv_a_up[l], rwkv_g_up[l],
                                 rwkv_k_k[l], rwkv_k_a[l], rwkv_r_k[l], rwkv_ln_g[l], rwkv_ln_b[l])
        x = _xattn(x.reshape(B, S, D), y_rwkv, y_moba, mix_out16, xattn_norm_g[l],
                   q16, mem, mem_norm_g[l], kv16, o16, 0).reshape(T, D)
        last = l == depth - 1
        x, next_ffn1 = _ffn(x, ffn2_norm_g[l], ffn2_in16, ffn2_out16, 0, final_norm_g if last else None,
                            side=[] if last else [(ffn1_w_in, l + 1), (ffn1_w_out, l + 1)])
        if not last:
            ffn1_in16, ffn1_out16 = next_ffn1
    return x.reshape(B, S, D)
```

```python
import functools
import math

import jax
import jax.numpy as jnp
from jax import lax
from jax.experimental import pallas as pl
from jax.experimental.pallas import tpu as pltpu

F32 = jnp.float32
BF16 = jnp.bfloat16

HEAD_DIM = 64
RWKV_WIDTH = 512
MOBA_WIDTH = 512
MOBA_HEADS = MOBA_WIDTH // HEAD_DIM
DECAY_LORA = 64
ICLR_LORA = 64
GATE_LORA = 128
RWKV_PROJ = 3 * RWKV_WIDTH + DECAY_LORA + ICLR_LORA + GATE_LORA
MOBA_PROJ = 3 * MOBA_WIDTH
LNX_EPS = 64e-5
MOBA_BLOCK = 256
MOBA_TOPK = 3
REL_BUCKETS = 32
REL_MAX_DISTANCE = 1024
XATTN_HEADS = 4
FFN_RES_WEIGHT = 0.5
NORM_EPS = 1e-6
LOG2E = math.log2(math.e)

LANES = 128
MXU_DIM = 256
HEADS_PER_TILE = LANES // HEAD_DIM
HEAD_SHIFT = HEAD_DIM.bit_length() - 1
RWKV_CHUNK = 64
CHUNK_SHIFT = RWKV_CHUNK.bit_length() - 1
RWKV_ROWS_IN_STEP = 2
MIXER_STAGE_RATIO = (3, 2)
SPLIT_LORA = (1, 1, 1)
SPLIT_SUM = (1, 1, 1)
SPLIT_CUMSUM = (1, 2, 2)
SPLIT_CHUNK = (1, 1, 1)
VMEM_LIMIT = 56 * 1024 * 1024


def _rms(x, g):
    ms = jnp.mean(x * x, axis=-1, keepdims=True)
    return x * lax.rsqrt(ms + NORM_EPS) * g


def _sigmoid(x):
    return 0.5 * jnp.tanh(0.5 * x) + 0.5


def _dot(a, b, precision=None):
    return jnp.dot(a, b, precision=precision, preferred_element_type=F32)


def _dot_nt(a, b, precision=None):
    return lax.dot_general(a, b, (((1,), (1,)), ((), ())), precision=precision,
                           preferred_element_type=F32)


def _dot_tn(a, b, precision=None):
    return lax.dot_general(a, b, (((0,), (0,)), ((), ())), precision=precision,
                           preferred_element_type=F32)


def _bf16_terms(x, n):
    terms = []
    for i in range(n):
        t = x.astype(BF16)
        terms.append(t)
        if i + 1 < n:
            x = x - t.astype(F32)
    return terms


def _mm(dot, a, b, split):
    na, nb, order = split
    at, bt = _bf16_terms(a, na), _bf16_terms(b, nb)
    acc = None
    for i in range(na):
        for j in range(nb):
            if i + j < order:
                t = dot(at[i], bt[j])
                acc = t if acc is None else acc + t
    return acc


def _params(*semantics):
    return pltpu.CompilerParams(dimension_semantics=semantics, vmem_limit_bytes=VMEM_LIMIT)


def _ffn_kernel(x_ref, g_ref, wi_ref, wo_ref, *rest, final_norm, sub, n_side):
    rest = list(rest)
    fg_ref = rest.pop(0) if final_norm else None
    side_in, o_ref, side_out = rest[:n_side], rest[n_side], rest[n_side + 1:]
    F = wo_ref.shape[0]
    for r0 in range(0, x_ref.shape[0], sub):
        x = x_ref[r0:r0 + sub, :]
        h = _rms(x, g_ref[...]).astype(BF16)
        gate = _dot(h, wi_ref[:, :F])
        up = _dot(h, wi_ref[:, F:])
        act = (gate * _sigmoid(gate) * up).astype(BF16)
        y = x + FFN_RES_WEIGHT * _dot(act, wo_ref[...])
        if final_norm:
            y = _rms(y, fg_ref[...])
        o_ref[r0:r0 + sub, :] = y
    for w_ref, w16_ref in zip(side_in, side_out):
        w16_ref[...] = w_ref[...].astype(BF16)


def _ffn(x, g, w_in, w_out, layer, final_g=None, side=(), *, tm=512, sub=256):
    T, D = x.shape
    F = w_out.shape[1]
    tm = min(tm, T)
    steps = T // tm
    resident = lambda shape: pl.BlockSpec(shape, lambda i: (0, 0), pipeline_mode=pl.Buffered(1))
    of_layer = lambda shape: pl.BlockSpec((None,) + shape, lambda i: (layer, 0, 0),
                                          pipeline_mode=pl.Buffered(1))
    in_specs = [
        pl.BlockSpec((tm, D), lambda i: (i, 0)),
        resident((1, D)),
        of_layer((D, 2 * F)),
        of_layer((F, D)),
    ]
    args = [x, g.reshape(1, D), w_in, w_out]
    if final_g is not None:
        in_specs.append(resident((1, D)))
        args.append(final_g.reshape(1, D))
    out_specs = [pl.BlockSpec((tm, D), lambda i: (i, 0))]
    out_shape = [jax.ShapeDtypeStruct((T, D), F32)]
    for w, w_layer in side:
        n_layers, rows, cols = w.shape
        assert rows % steps == 0
        slab = (rows // steps, cols)
        in_specs.append(pl.BlockSpec((None, None) + slab, lambda i, w_layer=w_layer: (w_layer, i, 0, 0)))
        args.append(w.reshape((n_layers, steps) + slab))
        out_specs.append(pl.BlockSpec((None,) + slab, lambda i: (i, 0, 0)))
        out_shape.append(jax.ShapeDtypeStruct((steps,) + slab, BF16))
    y, *side16 = pl.pallas_call(
        functools.partial(_ffn_kernel, final_norm=final_g is not None, sub=min(sub, tm), n_side=len(side)),
        grid=(steps,),
        in_specs=in_specs,
        out_specs=out_specs,
        out_shape=out_shape,
        compiler_params=_params("parallel"),
        name="ffn",
    )(*args)
    return y, [w16.reshape((1,) + w.shape[1:]) for w16, (w, _) in zip(side16, side)]


def _norm_proj_kernel(x_ref, g_ref, w_ref, *rest, splits, tiles_per_seq, sub):
    if tiles_per_seq:
        mu_ref, *o_refs, prev_ref = rest

        @pl.when(pl.program_id(0) % tiles_per_seq == 0)
        def _():
            prev_ref[...] = jnp.zeros_like(prev_ref)

        before = prev_ref[...]
    else:
        o_refs = rest
    row = lax.broadcasted_iota(jnp.int32, (sub, 1), 0)
    for r0 in range(0, x_ref.shape[0], sub):
        h = _rms(x_ref[r0:r0 + sub, :], g_ref[...]).astype(BF16)
        off = 0
        for idx, (o_ref, n) in enumerate(zip(o_refs, splits)):
            y = _dot(h, w_ref[:, off:off + n])
            if tiles_per_seq and idx == 0:
                shifted = jnp.where(row == 0, before, pltpu.roll(y, 1, axis=0))
                before = y[sub - 1:sub, :]
                y = y + (shifted - y) * mu_ref[...]
            o_ref[r0:r0 + sub, :] = y
            off += n
    if tiles_per_seq:
        prev_ref[...] = before


def _norm_proj(x, g, w, layer, splits, *, shift_mu=None, seq_len=None, tm=1024, sub=128):
    T, D = x.shape
    N = w.shape[2]
    tm = min(tm, T)
    assert sum(splits) == N
    in_specs = [
        pl.BlockSpec((tm, D), lambda i: (i, 0)),
        pl.BlockSpec((1, D), lambda i: (0, 0)),
        pl.BlockSpec((None, D, N), lambda i: (layer, 0, 0)),
    ]
    args = [x, g.reshape(1, D), w]
    scratch, tiles_per_seq = [], 0
    if shift_mu is not None:
        assert seq_len % tm == 0
        tiles_per_seq = seq_len // tm
        in_specs.append(pl.BlockSpec((1, splits[0]), lambda i: (0, 0)))
        args.append(shift_mu.reshape(1, splits[0]))
        scratch = [pltpu.VMEM((1, splits[0]), F32)]
    return pl.pallas_call(
        functools.partial(_norm_proj_kernel, splits=splits, tiles_per_seq=tiles_per_seq,
                          sub=min(sub, tm)),
        grid=(T // tm,),
        in_specs=in_specs,
        out_specs=[pl.BlockSpec((tm, n), lambda i: (i, 0)) for n in splits],
        out_shape=[jax.ShapeDtypeStruct((T, n), F32) for n in splits],
        scratch_shapes=scratch,
        compiler_params=_params("arbitrary" if tiles_per_seq else "parallel"),
        name="norm_proj",
    )(*args)


def _rwkv_stages(p_ref, w0_ref, a0_ref, wwa_ref, gup_ref, kk_ref, ka_ref, rk_ref,
                 lng_ref, lnb_ref, o_ref, state_ref):
    C = RWKV_CHUNK
    W = RWKV_WIDTH
    NB, T, P = p_ref.shape
    n_pairs = W // LANES
    n_chunks = T // C

    @pl.when(pl.program_id(1) == 0)
    def _():
        state_ref[...] = jnp.zeros_like(state_ref)

    lane = lax.broadcasted_iota(jnp.int32, (1, LANES), 1)
    first_head = lane < HEAD_DIM
    ri = lax.broadcasted_iota(jnp.int32, (LANES, LANES), 0)
    ci = lax.broadcasted_iota(jnp.int32, (LANES, LANES), 1)
    same_head = (ri >> HEAD_SHIFT) == (ci >> HEAD_SHIFT)
    ri2 = lax.broadcasted_iota(jnp.int32, (MXU_DIM, MXU_DIM), 0)
    ci2 = lax.broadcasted_iota(jnp.int32, (MXU_DIM, MXU_DIM), 1)
    head_ones = ((ri2 >> HEAD_SHIFT) == (ci2 >> HEAD_SHIFT)).astype(F32)
    in_chunk_prefix = ((ci2 <= ri2) & ((ci2 >> CHUNK_SHIFT) == (ri2 >> CHUNK_SHIFT))).astype(F32)
    tok = lax.broadcasted_iota(jnp.int32, (C, LANES), 0)
    col_tok = lax.broadcasted_iota(jnp.int32, (C, LANES), 1) & (C - 1)
    strict = col_tok < tok
    incl = col_tok <= tok
    incl2 = jnp.concatenate([incl, incl], axis=1)
    eye = (col_tok == tok).astype(F32)
    lanes = [slice(i * LANES, (i + 1) * LANES) for i in range(n_pairs)]
    rows_of = lambda c: slice(c * C, (c + 1) * C)
    units = [(c, i) for c in range(n_chunks) for i in range(n_pairs)]
    c16 = lambda t: t.astype(BF16)

    def head_sum(t):
        return jnp.concatenate(
            [_mm(_dot, t[:, i * MXU_DIM:(i + 1) * MXU_DIM], head_ones, SPLIT_SUM)
             for i in range(W // MXU_DIM)],
            axis=1)

    def stack(t):
        zero = jnp.zeros_like(t)
        return jnp.concatenate([jnp.where(first_head, t, zero), jnp.where(first_head, zero, t)], axis=0)

    def tokenwise(nb, out):
        p = p_ref[nb]
        r, k, v = p[:, 0:W], p[:, W:2 * W], p[:, 2 * W:3 * W]
        lora = p[:, 3 * W:3 * W + LANES]
        g_lo = p[:, 3 * W + LANES:3 * W + 2 * LANES]
        z = jnp.where(first_head, jnp.tanh(lora), lora)
        wa = _mm(_dot, z, wwa_ref[...], SPLIT_LORA)
        g = _mm(_dot, _sigmoid(g_lo), gup_ref[...], SPLIT_LORA)
        yield
        logw = -math.exp(-0.5) * _sigmoid(w0_ref[...] + wa[:, :W])
        a = _sigmoid(a0_ref[...] + wa[:, W:])
        kk = k * kk_ref[...]
        k2 = k * (1.0 + (a - 1.0) * ka_ref[...])
        yield
        kkn = kk * lax.rsqrt(jnp.maximum(head_sum(kk * kk), 1e-24))
        b = kkn * a
        yield
        bonus = head_sum(r * k2 * rk_ref[...]) * v
        logp = jnp.concatenate(
            [_mm(_dot, in_chunk_prefix, logw[i * MXU_DIM:(i + 1) * MXU_DIM], SPLIT_CUMSUM)
             for i in range(T // MXU_DIM)], axis=0)
        yield
        inv = jnp.exp(-logp)
        out.update(a16=c16(-kkn * jnp.exp(logp - logw)), r16=c16(r * jnp.exp(logp)), v16=c16(v))
        yield
        out.update(b16=c16(b * inv), k16=c16(k2 * inv), b=b, k2=k2, logp=logp, bonus=bonus, g=g)

    def chunkwise(tw, out):
        a16, r16, b16, k16, v16 = tw["a16"], tw["r16"], tw["b16"], tw["k16"], tw["v16"]
        ar = {(c, i): jnp.concatenate([a16[rows_of(c), lanes[i]], r16[rows_of(c), lanes[i]]], axis=0)
              for c, i in units}
        bk = {(c, i): jnp.concatenate([stack(b16[rows_of(c), lanes[i]]), stack(k16[rows_of(c), lanes[i]])],
                                      axis=0) for c, i in units}
        v_s = {(c, i): stack(v16[rows_of(c), lanes[i]]) for c, i in units}
        yield
        gram = {u_: _dot_nt(ar[u_], bk[u_]) for u_ in units}
        yield
        l_ab = {u_: jnp.where(strict, gram[u_][:C, :LANES], 0.0) for u_ in units}
        a_ak = {u_: c16(jnp.where(strict, gram[u_][:C, LANES:], 0.0)) for u_ in units}
        a_r = {u_: c16(jnp.where(incl2, gram[u_][C:], 0.0)) for u_ in units}
        t_inv = {u_: eye + l_ab[u_] for u_ in units}
        lp16 = {u_: c16(l_ab[u_]) for u_ in units}
        yield
        l_pow = {u_: _dot(lp16[u_], stack(lp16[u_])) for u_ in units}
        av = {u_: _dot(a_ak[u_], v_s[u_]) for u_ in units}
        yield
        levels = int(math.log2(C)) - 1
        for level in range(levels - 1):
            lp16 = {u_: c16(l_pow[u_]) for u_ in units}
            both = {u_: _dot(jnp.concatenate([c16(t_inv[u_]), lp16[u_]], axis=0), stack(lp16[u_]))
                    for u_ in units}
            yield
            t_inv = {u_: t_inv[u_] + both[u_][:C] for u_ in units}
            l_pow = {u_: both[u_][C:] for u_ in units}
        last = {u_: _dot(c16(t_inv[u_]), stack(c16(l_pow[u_]))) for u_ in units}
        yield
        out.update(ar=ar, v_s=v_s, a_r=a_r, av=av, t16={u_: c16(t_inv[u_] + last[u_]) for u_ in units})

    def recurrent(nb, tw, cw, out):
        b, k2, logp, v16 = tw["b"], tw["k2"], tw["logp"], tw["v16"]
        ar, v_s, a_r, av, t16 = cw["ar"], cw["v_s"], cw["a_r"], cw["av"], cw["t16"]
        states = [state_ref[nb, i] for i in range(n_pairs)]
        y_chunks = []
        for c in range(n_chunks):
            logp_end = logp[(c + 1) * C - 1:(c + 1) * C, :]
            to_end = jnp.exp(logp_end - logp[rows_of(c)])
            bk_e = [c16(jnp.concatenate([b[rows_of(c), sl] * to_end[:, sl], k2[rows_of(c), sl] * to_end[:, sl]],
                                        axis=0)) for sl in lanes]
            from_state = [_dot_nt(ar[c, i], c16(states[i])) for i in range(n_pairs)]
            yield
            u16 = [c16(_dot(t16[c, i], stack(c16(from_state[i][:C] + av[c, i])))) for i in range(n_pairs)]
            yield
            outer = [_dot_tn(jnp.concatenate([u16[i], v16[rows_of(c), lanes[i]]], axis=0), bk_e[i])
                     for i in range(n_pairs)]
            y_chunks.append(jnp.concatenate(
                [from_state[i][C:] + _dot(a_r[c, i], jnp.concatenate([stack(u16[i]), v_s[c, i]], axis=0))
                 for i in range(n_pairs)], axis=1))
            yield
            decay_end = jnp.exp(logp_end)
            states = [states[i] * decay_end[:, lanes[i]] + jnp.where(same_head, outer[i], 0.0)
                      for i in range(n_pairs)]
        for i in range(n_pairs):
            state_ref[nb, i] = states[i]
        out.update(y=jnp.concatenate(y_chunks, axis=0))

    def finish(nb, tw, rc):
        y = rc["y"]
        mean = head_sum(y) * (1.0 / HEAD_DIM)
        yield
        yc = y - mean
        var = head_sum(yc * yc) * (1.0 / HEAD_DIM)
        yield
        y = yc * lax.rsqrt(var + LNX_EPS) * lng_ref[...] + lnb_ref[...]
        o_ref[nb] = (y + tw["bonus"]) * tw["g"]

    def in_turn(gens):
        while gens:
            gens = [gen for gen in gens if next(gen, gens) is not gens]
            yield

    tw = [dict() for _ in range(NB)]
    cw = [dict() for _ in range(NB)]
    rc = [dict() for _ in range(NB)]
    groups = [range(g0, min(g0 + RWKV_ROWS_IN_STEP, NB)) for g0 in range(0, NB, RWKV_ROWS_IN_STEP)]
    phases = [
        lambda rows: in_turn([tokenwise(nb, tw[nb]) for nb in rows]),
        lambda rows: in_turn([chunkwise(tw[nb], cw[nb]) for nb in rows]),
        lambda rows: in_turn([recurrent(nb, tw[nb], cw[nb], rc[nb]) for nb in rows]),
        lambda rows: in_turn([finish(nb, tw[nb], rc[nb]) for nb in rows]),
    ]
    for wave in range(len(groups) + len(phases) - 1):
        yield from in_turn([phases[wave - g](rows) for g, rows in enumerate(groups)
                            if 0 <= wave - g < len(phases)])


def _rwkv_kernel(*refs):
    for _ in _rwkv_stages(*refs):
        pass


def _rwkv_call_parts(p, w0, w_up, a0, a_up, g_up, k_k, k_a, r_k, ln_g, ln_b, ts, nb):
    B, S, P = p.shape
    W = RWKV_WIDTH
    wwa = jnp.zeros((LANES, 2 * W), F32)
    wwa = wwa.at[:DECAY_LORA, :W].set(w_up).at[DECAY_LORA:, W:].set(a_up)
    vec = lambda t: t.reshape(1, -1)
    row_spec = lambda n: pl.BlockSpec((1, n), lambda b, s: (0, 0))
    assert ts % MXU_DIM == 0 and S % ts == 0 and B % nb == 0
    args = (p, vec(w0), vec(a0), wwa, g_up, vec(k_k), vec(k_a), vec(r_k), vec(ln_g), vec(ln_b))
    in_specs = [
        pl.BlockSpec((nb, ts, P), lambda b, s: (b, s, 0)),
        row_spec(W), row_spec(W),
        pl.BlockSpec((LANES, 2 * W), lambda b, s: (0, 0)),
        pl.BlockSpec((GATE_LORA, W), lambda b, s: (0, 0)),
        row_spec(W), row_spec(W), row_spec(W), row_spec(W), row_spec(W),
    ]
    out_spec = pl.BlockSpec((nb, ts, W), lambda b, s: (b, s, 0))
    out_shape = jax.ShapeDtypeStruct((B, S, W), F32)
    scratch = [pltpu.VMEM((nb, W // LANES, LANES, LANES), F32)]
    return (B // nb, S // ts), args, in_specs, out_spec, out_shape, scratch


def _rwkv(p, *weights, ts=256, nb=2):
    B, S, _ = p.shape
    grid, args, in_specs, out_spec, out_shape, scratch = _rwkv_call_parts(p, *weights, min(ts, S), min(nb, B))
    return pl.pallas_call(
        _rwkv_kernel,
        grid=grid,
        in_specs=in_specs,
        out_specs=out_spec,
        out_shape=out_shape,
        scratch_shapes=scratch,
        compiler_params=_params("parallel", "arbitrary"),
        name="rwkv7",
    )(*args)


def _rel_bias_kernel(tab_ref, o_ref, *, n_blocks):
    blk = MOBA_BLOCK
    max_exact = REL_BUCKETS // 2

    def bucket_of(n):
        large = max_exact + math.floor(math.log(max(n, 1) / max_exact)
                                       / math.log(REL_MAX_DISTANCE / max_exact) * (REL_BUCKETS - max_exact))
        return n if n < max_exact else min(large, REL_BUCKETS - 1)

    def block(cb):
        kc = lax.broadcasted_iota(jnp.int32, (blk, blk), 0) + cb * blk
        qi = lax.broadcasted_iota(jnp.int32, (blk, blk), 1)
        dist = qi + (n_blocks - 1) * blk - kc
        n = jnp.maximum(dist, 0)
        nf = jnp.maximum(n, 1).astype(F32)
        large = max_exact + (jnp.log(nf / max_exact) / math.log(REL_MAX_DISTANCE / max_exact)
                             * (REL_BUCKETS - max_exact)).astype(jnp.int32)
        large = jnp.minimum(large, REL_BUCKETS - 1)
        bucket = jnp.where(n < max_exact, n, large)
        n_lo = max((n_blocks - 1 - cb) * blk - (blk - 1), 0)
        n_hi = max((n_blocks - 1 - cb) * blk + (blk - 1), 0)
        b_lo = max(bucket_of(n_lo) - 1, 0)
        b_hi = min(bucket_of(n_hi) + 1, REL_BUCKETS - 1)
        for h in range(MOBA_HEADS):
            tile = jnp.zeros((blk, blk), F32)
            for c in range(b_lo, b_hi + 1):
                tile = jnp.where(bucket == c, tab_ref[h, c], tile)
            o_ref[h] = jnp.where(dist < 0, -jnp.inf, tile * LOG2E)

    for cb in range(n_blocks):
        pl.when(pl.program_id(0) == cb)(functools.partial(block, cb))


def _rel_bias_strip(rel_bias, n_blocks):
    blk = MOBA_BLOCK
    return pl.pallas_call(
        functools.partial(_rel_bias_kernel, n_blocks=n_blocks),
        grid=(n_blocks,),
        in_specs=[pl.BlockSpec(memory_space=pltpu.SMEM)],
        out_specs=pl.BlockSpec((MOBA_HEADS, blk, blk), lambda c: (0, c, 0)),
        out_shape=jax.ShapeDtypeStruct((MOBA_HEADS, n_blocks * blk, blk), F32),
        compiler_params=_params("parallel"),
        name="rel_bias_strip",
    )(rel_bias)


def _moba_stages(q_ref, k_ref, v_ref, bias_ref, o_ref, *, n_blocks):
    blk = MOBA_BLOCK
    lane = lax.broadcasted_iota(jnp.int32, (1, LANES), 1)
    vrow = lax.broadcasted_iota(jnp.int32, (LANES, 1), 0)
    q = q_ref[0] * (HEAD_DIM ** -0.5 * LOG2E)
    k = k_ref[0]
    k16 = k.astype(BF16)
    vt = v_ref[0].T
    vt16 = [jnp.where((vrow >> HEAD_SHIFT) == e, vt, 1.0).astype(BF16) for e in range(HEADS_PER_TILE)]
    kmean = jnp.mean(k.reshape(n_blocks, blk, LANES), axis=1)

    q_hi, q_lo = _bf16_terms(q, 2)
    m_hi, m_lo = _bf16_terms(kmean, 2)
    gates, q16 = [], []
    for e in range(HEADS_PER_TILE):
        head_lanes = (lane >> HEAD_SHIFT) == e
        zero = jnp.zeros_like(q_hi)
        qe_hi, qe_lo = jnp.where(head_lanes, q_hi, zero), jnp.where(head_lanes, q_lo, zero)
        gates.append(_dot_nt(m_hi, qe_hi) + _dot_nt(m_hi, qe_lo) + _dot_nt(m_lo, qe_hi))
        q16.append(qe_hi)

    def scores(qb, e):
        return _dot_nt(k16[:(qb + 1) * blk], q16[e][qb * blk:(qb + 1) * blk])

    def weights(qb, e, s_t):
        rows = slice(qb * blk, (qb + 1) * blk)
        bias_lo = (n_blocks - 1 - qb) * blk
        g = [gates[e][j:j + 1, rows] for j in range(qb)]
        tiles = []
        for j in range(qb + 1):
            t = s_t[j * blk:(j + 1) * blk] + bias_ref[e, bias_lo + j * blk:bias_lo + (j + 1) * blk, :]
            if j < qb:
                rank = jnp.zeros((1, blk), jnp.int32)
                for jj in range(qb):
                    if jj != j:
                        ahead = (g[jj] >= g[j]) if jj < j else (g[jj] > g[j])
                        rank = rank + jnp.where(ahead, 1, 0)
                t = jnp.where(rank < MOBA_TOPK, t, -jnp.inf)
            tiles.append(t)
        m = tiles[0].max(axis=0, keepdims=True)
        for t in tiles[1:]:
            m = jnp.maximum(m, t.max(axis=0, keepdims=True))
        return jnp.concatenate([jnp.exp2(t - m).astype(BF16) for t in tiles], axis=0)

    def attend(qb, e, p16):
        pv = _dot(vt16[e][:, :(qb + 1) * blk], p16)
        den_row = (1 - e) * HEAD_DIM
        return pv / pv[den_row:den_row + 1]

    units = [(qb, e) for qb in range(n_blocks) for e in range(HEADS_PER_TILE)]
    s_t, p16, out_t = {}, {}, {}
    for step in range(len(units) + 2):
        if step < len(units):
            s_t[units[step]] = scores(*units[step])
        if 0 <= step - 1 < len(units):
            u = units[step - 1]
            p16[u] = weights(*u, s_t.pop(u))
        if 0 <= step - 2 < len(units):
            u = units[step - 2]
            out_t[u] = attend(*u, p16.pop(u))
            qb, e = u
            if e == HEADS_PER_TILE - 1:
                o_ref[0, qb * blk:(qb + 1) * blk, :] = jnp.where(
                    vrow < HEAD_DIM, out_t.pop((qb, 0)), out_t.pop((qb, 1))).T
        yield


def _moba_kernel(*refs, n_blocks):
    for _ in _moba_stages(*refs, n_blocks=n_blocks):
        pass


def _moba(qkv, bias_strip):
    B, S, _ = qkv.shape
    blk = MOBA_BLOCK
    n_blocks = S // blk
    n_pairs = MOBA_WIDTH // LANES
    return pl.pallas_call(
        functools.partial(_moba_kernel, n_blocks=n_blocks),
        grid=(n_pairs, B),
        in_specs=[
            pl.BlockSpec((1, S, LANES), lambda p, b: (b, 0, p)),
            pl.BlockSpec((1, S, LANES), lambda p, b: (b, 0, n_pairs + p)),
            pl.BlockSpec((1, S, LANES), lambda p, b: (b, 0, 2 * n_pairs + p)),
            pl.BlockSpec((HEADS_PER_TILE, S, blk), lambda p, b: (p, 0, 0)),
        ],
        out_specs=pl.BlockSpec((1, S, LANES), lambda p, b: (b, 0, p)),
        out_shape=jax.ShapeDtypeStruct((B, S, MOBA_WIDTH), F32),
        compiler_params=_params("parallel", "parallel"),
        name="moba",
    )(qkv, qkv, qkv, bias_strip)


N_RWKV_INPUTS = 10
N_MOBA_INPUTS = 4


def _mixers_kernel(*refs, n_blocks):
    rwkv_in, refs = refs[:N_RWKV_INPUTS], refs[N_RWKV_INPUTS:]
    moba_in, (y_rwkv_ref, y_moba_ref, state_ref) = refs[:N_MOBA_INPUTS], refs[N_MOBA_INPUTS:]
    jobs = [(_rwkv_stages(*rwkv_in, y_rwkv_ref, state_ref), MIXER_STAGE_RATIO[0]),
            (_moba_stages(*moba_in, y_moba_ref, n_blocks=n_blocks), MIXER_STAGE_RATIO[1])]
    while jobs:
        jobs = [(gen, n) for gen, n in jobs if all(next(gen, jobs) is not jobs for _ in range(n))]


def _mixers(p, qkv, bias_strip, *rwkv_weights, ts=256, nb=2):
    B, S, _ = qkv.shape
    blk = MOBA_BLOCK
    n_blocks = S // blk
    n_pairs = MOBA_WIDTH // LANES
    ts, nb = min(ts, S), min(nb, B)
    grid, args, in_specs, out_spec, out_shape, scratch = _rwkv_call_parts(p, *rwkv_weights, ts, nb)
    if grid[0] * grid[1] != n_pairs * B:
        return _rwkv(p, *rwkv_weights, ts=ts, nb=nb), _moba(qkv, bias_strip)
    item = lambda g, s: g * grid[1] + s
    row = lambda g, s: item(g, s) % B
    pair = lambda g, s: item(g, s) // B
    in_specs = in_specs + [
        pl.BlockSpec((1, S, LANES), lambda g, s: (row(g, s), 0, pair(g, s))),
        pl.BlockSpec((1, S, LANES), lambda g, s: (row(g, s), 0, n_pairs + pair(g, s))),
        pl.BlockSpec((1, S, LANES), lambda g, s: (row(g, s), 0, 2 * n_pairs + pair(g, s))),
        pl.BlockSpec((HEADS_PER_TILE, S, blk), lambda g, s: (pair(g, s), 0, 0)),
    ]
    assert len(args) == N_RWKV_INPUTS
    return pl.pallas_call(
        functools.partial(_mixers_kernel, n_blocks=n_blocks),
        grid=grid,
        in_specs=in_specs,
        out_specs=[out_spec, pl.BlockSpec((1, S, LANES), lambda g, s: (row(g, s), 0, pair(g, s)))],
        out_shape=[out_shape, jax.ShapeDtypeStruct((B, S, MOBA_WIDTH), F32)],
        scratch_shapes=scratch,
        compiler_params=_params("parallel", "arbitrary"),
        name="mixers",
    )(*args, qkv, qkv, qkv, bias_strip)


def _xattn_kernel(x_ref, ya_ref, yb_ref, wa_ref, wb_ref, g_ref, wq_ref, mem_ref, mg_ref, wkv_ref, wo_ref,
                  o_ref, kv_ref, *, sub):
    D = x_ref.shape[-1]
    dh = D // XATTN_HEADS

    @pl.when(pl.program_id(1) == 0)
    def _():
        kv_ref[...] = _dot(_rms(mem_ref[0], mg_ref[...]).astype(BF16), wkv_ref[...]).astype(BF16)

    k16 = [kv_ref[:, i * dh:(i + 1) * dh] for i in range(XATTN_HEADS)]
    v16 = [kv_ref[:, D + i * dh:D + (i + 1) * dh] for i in range(XATTN_HEADS)]

    def project(r0):
        rows = slice(r0, r0 + sub)
        x = (x_ref[0, rows, :] + _dot(ya_ref[0, rows, :].astype(BF16), wa_ref[...])
             + _dot(yb_ref[0, rows, :].astype(BF16), wb_ref[...]))
        h = _rms(x, g_ref[...]).astype(BF16)
        q = (_dot(h, wq_ref[...]) * (dh ** -0.5 * LOG2E)).astype(BF16)
        return x, q

    def attend(q):
        hs = range(XATTN_HEADS)
        s = [_dot_nt(q[:, i * dh:(i + 1) * dh], k16[i]) for i in hs]
        pr = [jnp.exp2(s[i] - jnp.max(s[i], axis=-1, keepdims=True)) for i in hs]
        pr = [(pr[i] / jnp.sum(pr[i], axis=-1, keepdims=True)).astype(BF16) for i in hs]
        return jnp.concatenate([_dot(pr[i], v16[i]) for i in hs], axis=1).astype(BF16)

    starts = list(range(0, x_ref.shape[1], sub))
    xq, att = {}, {}
    for step in range(len(starts) + 2):
        if step < len(starts):
            xq[starts[step]] = project(starts[step])
        if 0 <= step - 1 < len(starts):
            r0 = starts[step - 1]
            att[r0] = attend(xq[r0][1])
        if 0 <= step - 2 < len(starts):
            r0 = starts[step - 2]
            o_ref[0, r0:r0 + sub, :] = xq.pop(r0)[0] + _dot(att.pop(r0), wo_ref[...])


def _xattn(x, ya, yb, w_mix, g, w_q, mem, mem_g, w_kv, w_o, layer, *, tm=512, sub=512):
    B, S, D = x.shape
    M = mem.shape[1]
    na, nb = ya.shape[-1], yb.shape[-1]
    assert na == nb
    tm = min(tm, S)
    of_layer = lambda shape, blk=0: pl.BlockSpec((None,) + shape, lambda b, i: (layer, blk, 0))
    return pl.pallas_call(
        functools.partial(_xattn_kernel, sub=min(sub, tm)),
        grid=(B, S // tm),
        in_specs=[
            pl.BlockSpec((1, tm, D), lambda b, i: (b, i, 0)),
            pl.BlockSpec((1, tm, na), lambda b, i: (b, i, 0)),
            pl.BlockSpec((1, tm, nb), lambda b, i: (b, i, 0)),
            of_layer((na, D), 0),
            of_layer((nb, D), 1),
            pl.BlockSpec((1, D), lambda b, i: (0, 0)),
            of_layer((D, D)),
            pl.BlockSpec((1, M, D), lambda b, i: (b, 0, 0)),
            pl.BlockSpec((1, D), lambda b, i: (0, 0)),
            of_layer((D, 2 * D)),
            of_layer((D, D)),
        ],
        out_specs=pl.BlockSpec((1, tm, D), lambda b, i: (b, i, 0)),
        out_shape=jax.ShapeDtypeStruct((B, S, D), F32),
        scratch_shapes=[pltpu.VMEM((M, 2 * D), BF16)],
        compiler_params=_params("parallel", "arbitrary"),
        name="xattn",
    )(x, ya, yb, w_mix, w_mix, g.reshape(1, D), w_q, mem, mem_g.reshape(1, D), w_kv, w_o)


def kernel(x, mem, rel_bias, final_norm_g, ffn1_norm_g, ffn1_w_in, ffn1_w_out, mix_norm_g, w_mix_in, w_mix_out, rwkv_mu, rwkv_w0, rwkv_w_up, rwkv_a0, rwkv_a_up, rwkv_g_up, rwkv_k_k, rwkv_k_a, rwkv_r_k, rwkv_ln_g, rwkv_ln_b, xattn_norm_g, mem_norm_g, xattn_w_q, xattn_w_kv, xattn_w_o, ffn2_norm_g, ffn2_w_in, ffn2_w_out):
    B, S, D = x.shape
    M = mem.shape[1]
    depth = ffn1_w_in.shape[0]
    T = B * S
    ffn1_in16, ffn1_out16 = ffn1_w_in[:1].astype(BF16), ffn1_w_out[:1].astype(BF16)

    bias_strip = _rel_bias_strip(rel_bias, S // MOBA_BLOCK)
    x = x.reshape(T, D)
    for l in range(depth):
        later = (w_mix_in, xattn_w_kv, w_mix_out, xattn_w_q, xattn_w_o, ffn2_w_in, ffn2_w_out)
        x, (mix_in16, kv16, mix_out16, q16, o16, ffn2_in16, ffn2_out16) = _ffn(
            x, ffn1_norm_g[l], ffn1_in16, ffn1_out16, 0, side=[(w, l) for w in later])
        p_rwkv, qkv = _norm_proj(x, mix_norm_g[l], mix_in16, 0, (RWKV_PROJ, MOBA_PROJ),
                                 shift_mu=rwkv_mu[l], seq_len=S)
        y_rwkv, y_moba = _mixers(p_rwkv.reshape(B, S, RWKV_PROJ), qkv.reshape(B, S, MOBA_PROJ), bias_strip,
                                 rwkv_w0[l], rwkv_w_up[l], rwkv_a0[l], rwkv_a_up[l], rwkv_g_up[l],
                                 rwkv_k_k[l], rwkv_k_a[l], rwkv_r_k[l], rwkv_ln_g[l], rwkv_ln_b[l])
        x = _xattn(x.reshape(B, S, D), y_rwkv, y_moba, mix_out16, xattn_norm_g[l],
                   q16, mem, mem_norm_g[l], kv16, o16, 0).reshape(T, D)
        last = l == depth - 1
        x, next_ffn1 = _ffn(x, ffn2_norm_g[l], ffn2_in16, ffn2_out16, 0, final_norm_g if last else None,
                            side=[] if last else [(ffn1_w_in, l + 1), (ffn1_w_out, l + 1)])
        if not last:
            ffn1_in16, ffn1_out16 = next_ffn1
    return x.reshape(B, S, D)
```

```python
import functools
import math

import jax
import jax.numpy as jnp
from jax import lax
from jax.experimental import pallas as pl
from jax.experimental.pallas import tpu as pltpu

F32 = jnp.float32
BF16 = jnp.bfloat16

HEAD_DIM = 64
RWKV_WIDTH = 512
MOBA_WIDTH = 512
MOBA_HEADS = MOBA_WIDTH // HEAD_DIM
DECAY_LORA = 64
ICLR_LORA = 64
GATE_LORA = 128
RWKV_PROJ = 3 * RWKV_WIDTH + DECAY_LORA + ICLR_LORA + GATE_LORA
MOBA_PROJ = 3 * MOBA_WIDTH
LNX_EPS = 64e-5
MOBA_BLOCK = 256
MOBA_TOPK = 3
REL_BUCKETS = 32
REL_MAX_DISTANCE = 1024
XATTN_HEADS = 4
FFN_RES_WEIGHT = 0.5
NORM_EPS = 1e-6
LOG2E = math.log2(math.e)

LANES = 128
MXU_DIM = 256
HEADS_PER_TILE = LANES // HEAD_DIM
HEAD_SHIFT = HEAD_DIM.bit_length() - 1
RWKV_CHUNK = 64
CHUNK_SHIFT = RWKV_CHUNK.bit_length() - 1
RWKV_ROWS_IN_STEP = 2
MIXER_STAGE_RATIO = (3, 2)
SPLIT_LORA = (1, 1, 1)
SPLIT_SUM = (1, 1, 1)
SPLIT_CUMSUM = (1, 2, 2)
VMEM_LIMIT = 56 * 1024 * 1024


def _rms(x, g):
    ms = jnp.mean(x * x, axis=-1, keepdims=True)
    return x * lax.rsqrt(ms + NORM_EPS) * g


def _sigmoid(x):
    return 0.5 * jnp.tanh(0.5 * x) + 0.5


def _dot(a, b, precision=None):
    return jnp.dot(a, b, precision=precision, preferred_element_type=F32)


def _dot_nt(a, b, precision=None):
    return lax.dot_general(a, b, (((1,), (1,)), ((), ())), precision=precision,
                           preferred_element_type=F32)


def _dot_tn(a, b, precision=None):
    return lax.dot_general(a, b, (((0,), (0,)), ((), ())), precision=precision,
                           preferred_element_type=F32)


def _bf16_terms(x, n):
    terms = []
    for i in range(n):
        t = x.astype(BF16)
        terms.append(t)
        if i + 1 < n:
            x = x - t.astype(F32)
    return terms


def _mm(dot, a, b, split):
    na, nb, order = split
    at, bt = _bf16_terms(a, na), _bf16_terms(b, nb)
    acc = None
    for i in range(na):
        for j in range(nb):
            if i + j < order:
                t = dot(at[i], bt[j])
                acc = t if acc is None else acc + t
    return acc


def _params(*semantics):
    return pltpu.CompilerParams(dimension_semantics=semantics, vmem_limit_bytes=VMEM_LIMIT)


def _ffn_kernel(x_ref, g_ref, wi_ref, wo_ref, *rest, final_norm, sub, n_side):
    rest = list(rest)
    fg_ref = rest.pop(0) if final_norm else None
    side_in, o_ref, side_out = rest[:n_side], rest[n_side], rest[n_side + 1:]
    F = wo_ref.shape[0]
    for r0 in range(0, x_ref.shape[0], sub):
        x = x_ref[r0:r0 + sub, :]
        h = _rms(x, g_ref[...]).astype(BF16)
        gate = _dot(h, wi_ref[:, :F])
        up = _dot(h, wi_ref[:, F:])
        act = (gate * _sigmoid(gate) * up).astype(BF16)
        y = x + FFN_RES_WEIGHT * _dot(act, wo_ref[...])
        if final_norm:
            y = _rms(y, fg_ref[...])
        o_ref[r0:r0 + sub, :] = y
    for w_ref, w16_ref in zip(side_in, side_out):
        w16_ref[...] = w_ref[...].astype(BF16)


def _ffn(x, g, w_in, w_out, layer, final_g=None, side=(), *, tm=1024, sub=256):
    T, D = x.shape
    F = w_out.shape[1]
    tm = min(tm, T)
    steps = T // tm
    resident = lambda shape: pl.BlockSpec(shape, lambda i: (0, 0), pipeline_mode=pl.Buffered(1))
    of_layer = lambda shape: pl.BlockSpec((None,) + shape, lambda i: (layer, 0, 0),
                                          pipeline_mode=pl.Buffered(1))
    in_specs = [
        pl.BlockSpec((tm, D), lambda i: (i, 0)),
        resident((1, D)),
        of_layer((D, 2 * F)),
        of_layer((F, D)),
    ]
    args = [x, g.reshape(1, D), w_in, w_out]
    if final_g is not None:
        in_specs.append(resident((1, D)))
        args.append(final_g.reshape(1, D))
    out_specs = [pl.BlockSpec((tm, D), lambda i: (i, 0))]
    out_shape = [jax.ShapeDtypeStruct((T, D), F32)]
    for w, w_layer in side:
        n_layers, rows, cols = w.shape
        assert rows % steps == 0
        slab = (rows // steps, cols)
        in_specs.append(pl.BlockSpec((None, None) + slab, lambda i, w_layer=w_layer: (w_layer, i, 0, 0)))
        args.append(w.reshape((n_layers, steps) + slab))
        out_specs.append(pl.BlockSpec((None,) + slab, lambda i: (i, 0, 0)))
        out_shape.append(jax.ShapeDtypeStruct((steps,) + slab, BF16))
    y, *side16 = pl.pallas_call(
        functools.partial(_ffn_kernel, final_norm=final_g is not None, sub=min(sub, tm), n_side=len(side)),
        grid=(steps,),
        in_specs=in_specs,
        out_specs=out_specs,
        out_shape=out_shape,
        compiler_params=_params("parallel"),
        name="ffn",
    )(*args)
    return y, [w16.reshape((1,) + w.shape[1:]) for w16, (w, _) in zip(side16, side)]


def _norm_proj_kernel(x_ref, g_ref, w_ref, *rest, splits, tiles_per_seq, sub):
    if tiles_per_seq:
        mu_ref, *o_refs, prev_ref = rest

        @pl.when(pl.program_id(0) % tiles_per_seq == 0)
        def _():
            prev_ref[...] = jnp.zeros_like(prev_ref)

        before = prev_ref[...]
    else:
        o_refs = rest
    row = lax.broadcasted_iota(jnp.int32, (sub, 1), 0)
    for r0 in range(0, x_ref.shape[0], sub):
        h = _rms(x_ref[r0:r0 + sub, :], g_ref[...]).astype(BF16)
        off = 0
        for idx, (o_ref, n) in enumerate(zip(o_refs, splits)):
            y = _dot(h, w_ref[:, off:off + n])
            if tiles_per_seq and idx == 0:
                shifted = jnp.where(row == 0, before, pltpu.roll(y, 1, axis=0))
                before = y[sub - 1:sub, :]
                y = y + (shifted - y) * mu_ref[...]
            o_ref[r0:r0 + sub, :] = y
            off += n
    if tiles_per_seq:
        prev_ref[...] = before


def _norm_proj(x, g, w, layer, splits, *, shift_mu=None, seq_len=None, tm=1024, sub=128):
    T, D = x.shape
    N = w.shape[2]
    tm = min(tm, T)
    assert sum(splits) == N
    in_specs = [
        pl.BlockSpec((tm, D), lambda i: (i, 0)),
        pl.BlockSpec((1, D), lambda i: (0, 0)),
        pl.BlockSpec((None, D, N), lambda i: (layer, 0, 0)),
    ]
    args = [x, g.reshape(1, D), w]
    scratch, tiles_per_seq = [], 0
    if shift_mu is not None:
        assert seq_len % tm == 0
        tiles_per_seq = seq_len // tm
        in_specs.append(pl.BlockSpec((1, splits[0]), lambda i: (0, 0)))
        args.append(shift_mu.reshape(1, splits[0]))
        scratch = [pltpu.VMEM((1, splits[0]), F32)]
    return pl.pallas_call(
        functools.partial(_norm_proj_kernel, splits=splits, tiles_per_seq=tiles_per_seq,
                          sub=min(sub, tm)),
        grid=(T // tm,),
        in_specs=in_specs,
        out_specs=[pl.BlockSpec((tm, n), lambda i: (i, 0)) for n in splits],
        out_shape=[jax.ShapeDtypeStruct((T, n), F32) for n in splits],
        scratch_shapes=scratch,
        compiler_params=_params("arbitrary" if tiles_per_seq else "parallel"),
        name="norm_proj",
    )(*args)


def _rwkv_stages(p_ref, w0_ref, a0_ref, wwa_ref, gup_ref, kk_ref, ka_ref, rk_ref,
                 lng_ref, lnb_ref, o_ref, state_ref):
    C = RWKV_CHUNK
    W = RWKV_WIDTH
    NB, T, P = p_ref.shape
    n_pairs = W // LANES
    n_chunks = T // C

    @pl.when(pl.program_id(1) == 0)
    def _():
        state_ref[...] = jnp.zeros_like(state_ref)

    lane = lax.broadcasted_iota(jnp.int32, (1, LANES), 1)
    first_head = lane < HEAD_DIM
    ri = lax.broadcasted_iota(jnp.int32, (LANES, LANES), 0)
    ci = lax.broadcasted_iota(jnp.int32, (LANES, LANES), 1)
    same_head = (ri >> HEAD_SHIFT) == (ci >> HEAD_SHIFT)
    ri2 = lax.broadcasted_iota(jnp.int32, (MXU_DIM, MXU_DIM), 0)
    ci2 = lax.broadcasted_iota(jnp.int32, (MXU_DIM, MXU_DIM), 1)
    head_ones = ((ri2 >> HEAD_SHIFT) == (ci2 >> HEAD_SHIFT)).astype(F32)
    in_chunk_prefix = ((ci2 <= ri2) & ((ci2 >> CHUNK_SHIFT) == (ri2 >> CHUNK_SHIFT))).astype(F32)
    tok = lax.broadcasted_iota(jnp.int32, (C, LANES), 0)
    col_tok = lax.broadcasted_iota(jnp.int32, (C, LANES), 1) & (C - 1)
    strict = col_tok < tok
    incl = col_tok <= tok
    incl2 = jnp.concatenate([incl, incl], axis=1)
    eye = (col_tok == tok).astype(F32)
    lanes = [slice(i * LANES, (i + 1) * LANES) for i in range(n_pairs)]
    rows_of = lambda c: slice(c * C, (c + 1) * C)
    units = [(c, i) for c in range(n_chunks) for i in range(n_pairs)]
    c16 = lambda t: t.astype(BF16)

    def head_sum(t):
        return jnp.concatenate(
            [_mm(_dot, t[:, i * MXU_DIM:(i + 1) * MXU_DIM], head_ones, SPLIT_SUM)
             for i in range(W // MXU_DIM)],
            axis=1)

    def stack(t):
        zero = jnp.zeros_like(t)
        return jnp.concatenate([jnp.where(first_head, t, zero), jnp.where(first_head, zero, t)], axis=0)

    def tokenwise(nb, out):
        p = p_ref[nb]
        r, k, v = p[:, 0:W], p[:, W:2 * W], p[:, 2 * W:3 * W]
        lora = p[:, 3 * W:3 * W + LANES]
        g_lo = p[:, 3 * W + LANES:3 * W + 2 * LANES]
        z = jnp.where(first_head, jnp.tanh(lora), lora)
        wa = _mm(_dot, z, wwa_ref[...], SPLIT_LORA)
        g = _mm(_dot, _sigmoid(g_lo), gup_ref[...], SPLIT_LORA)
        yield
        logw = -math.exp(-0.5) * _sigmoid(w0_ref[...] + wa[:, :W])
        a = _sigmoid(a0_ref[...] + wa[:, W:])
        kk = k * kk_ref[...]
        k2 = k * (1.0 + (a - 1.0) * ka_ref[...])
        yield
        kkn = kk * lax.rsqrt(jnp.maximum(head_sum(kk * kk), 1e-24))
        b = kkn * a
        yield
        bonus = head_sum(r * k2 * rk_ref[...]) * v
        logp = jnp.concatenate(
            [_mm(_dot, in_chunk_prefix, logw[i * MXU_DIM:(i + 1) * MXU_DIM], SPLIT_CUMSUM)
             for i in range(T // MXU_DIM)], axis=0)
        yield
        inv = jnp.exp(-logp)
        out.update(a16=c16(-kkn * jnp.exp(logp - logw)), r16=c16(r * jnp.exp(logp)), v16=c16(v))
        yield
        out.update(b16=c16(b * inv), k16=c16(k2 * inv), b=b, k2=k2, logp=logp, bonus=bonus, g=g)

    def chunkwise(tw, out):
        a16, r16, b16, k16, v16 = tw["a16"], tw["r16"], tw["b16"], tw["k16"], tw["v16"]
        ar = {(c, i): jnp.concatenate([a16[rows_of(c), lanes[i]], r16[rows_of(c), lanes[i]]], axis=0)
              for c, i in units}
        bk = {(c, i): jnp.concatenate([stack(b16[rows_of(c), lanes[i]]), stack(k16[rows_of(c), lanes[i]])],
                                      axis=0) for c, i in units}
        v_s = {(c, i): stack(v16[rows_of(c), lanes[i]]) for c, i in units}
        yield
        gram = {u_: _dot_nt(ar[u_], bk[u_]) for u_ in units}
        yield
        l_ab = {u_: jnp.where(strict, gram[u_][:C, :LANES], 0.0) for u_ in units}
        a_ak = {u_: c16(jnp.where(strict, gram[u_][:C, LANES:], 0.0)) for u_ in units}
        a_r = {u_: c16(jnp.where(incl2, gram[u_][C:], 0.0)) for u_ in units}
        t_inv = {u_: eye + l_ab[u_] for u_ in units}
        lp16 = {u_: c16(l_ab[u_]) for u_ in units}
        yield
        l_pow = {u_: _dot(lp16[u_], stack(lp16[u_])) for u_ in units}
        av = {u_: _dot(a_ak[u_], v_s[u_]) for u_ in units}
        yield
        levels = int(math.log2(C)) - 1
        for level in range(levels - 1):
            lp16 = {u_: c16(l_pow[u_]) for u_ in units}
            both = {u_: _dot(jnp.concatenate([c16(t_inv[u_]), lp16[u_]], axis=0), stack(lp16[u_]))
                    for u_ in units}
            yield
            t_inv = {u_: t_inv[u_] + both[u_][:C] for u_ in units}
            l_pow = {u_: both[u_][C:] for u_ in units}
        last = {u_: _dot(c16(t_inv[u_]), stack(c16(l_pow[u_]))) for u_ in units}
        yield
        out.update(ar=ar, v_s=v_s, a_r=a_r, av=av, t16={u_: c16(t_inv[u_] + last[u_]) for u_ in units})

    def recurrent(nb, tw, cw, out):
        b, k2, logp, v16 = tw["b"], tw["k2"], tw["logp"], tw["v16"]
        ar, v_s, a_r, av, t16 = cw["ar"], cw["v_s"], cw["a_r"], cw["av"], cw["t16"]
        states = [state_ref[nb, i] for i in range(n_pairs)]
        y_chunks = []
        for c in range(n_chunks):
            logp_end = logp[(c + 1) * C - 1:(c + 1) * C, :]
            to_end = jnp.exp(logp_end - logp[rows_of(c)])
            bk_e = [c16(jnp.concatenate([b[rows_of(c), sl] * to_end[:, sl], k2[rows_of(c), sl] * to_end[:, sl]],
                                        axis=0)) for sl in lanes]
            from_state = [_dot_nt(ar[c, i], c16(states[i])) for i in range(n_pairs)]
            yield
            u16 = [c16(_dot(t16[c, i], stack(c16(from_state[i][:C] + av[c, i])))) for i in range(n_pairs)]
            yield
            outer = [_dot_tn(jnp.concatenate([u16[i], v16[rows_of(c), lanes[i]]], axis=0), bk_e[i])
                     for i in range(n_pairs)]
            y_chunks.append(jnp.concatenate(
                [from_state[i][C:] + _dot(a_r[c, i], jnp.concatenate([stack(u16[i]), v_s[c, i]], axis=0))
                 for i in range(n_pairs)], axis=1))
            yield
            decay_end = jnp.exp(logp_end)
            states = [states[i] * decay_end[:, lanes[i]] + jnp.where(same_head, outer[i], 0.0)
                      for i in range(n_pairs)]
        for i in range(n_pairs):
            state_ref[nb, i] = states[i]
        out.update(y=jnp.concatenate(y_chunks, axis=0))

    def finish(nb, tw, rc):
        y = rc["y"]
        mean = head_sum(y) * (1.0 / HEAD_DIM)
        yield
        yc = y - mean
        var = head_sum(yc * yc) * (1.0 / HEAD_DIM)
        yield
        y = yc * lax.rsqrt(var + LNX_EPS) * lng_ref[...] + lnb_ref[...]
        o_ref[nb] = (y + tw["bonus"]) * tw["g"]

    def in_turn(gens):
        while gens:
            gens = [gen for gen in gens if next(gen, gens) is not gens]
            yield

    tw = [dict() for _ in range(NB)]
    cw = [dict() for _ in range(NB)]
    rc = [dict() for _ in range(NB)]
    groups = [range(g0, min(g0 + RWKV_ROWS_IN_STEP, NB)) for g0 in range(0, NB, RWKV_ROWS_IN_STEP)]
    phases = [
        lambda rows: in_turn([tokenwise(nb, tw[nb]) for nb in rows]),
        lambda rows: in_turn([chunkwise(tw[nb], cw[nb]) for nb in rows]),
        lambda rows: in_turn([recurrent(nb, tw[nb], cw[nb], rc[nb]) for nb in rows]),
        lambda rows: in_turn([finish(nb, tw[nb], rc[nb]) for nb in rows]),
    ]
    for wave in range(len(groups) + len(phases) - 1):
        yield from in_turn([phases[wave - g](rows) for g, rows in enumerate(groups)
                            if 0 <= wave - g < len(phases)])


def _rwkv_kernel(*refs):
    for _ in _rwkv_stages(*refs):
        pass


def _rwkv_call_parts(p, w0, w_up, a0, a_up, g_up, k_k, k_a, r_k, ln_g, ln_b, ts, nb):
    B, S, P = p.shape
    W = RWKV_WIDTH
    wwa = jnp.zeros((LANES, 2 * W), F32)
    wwa = wwa.at[:DECAY_LORA, :W].set(w_up).at[DECAY_LORA:, W:].set(a_up)
    vec = lambda t: t.reshape(1, -1)
    row_spec = lambda n: pl.BlockSpec((1, n), lambda b, s: (0, 0))
    assert ts % MXU_DIM == 0 and S % ts == 0 and B % nb == 0
    args = (p, vec(w0), vec(a0), wwa, g_up, vec(k_k), vec(k_a), vec(r_k), vec(ln_g), vec(ln_b))
    in_specs = [
        pl.BlockSpec((nb, ts, P), lambda b, s: (b, s, 0)),
        row_spec(W), row_spec(W),
        pl.BlockSpec((LANES, 2 * W), lambda b, s: (0, 0)),
        pl.BlockSpec((GATE_LORA, W), lambda b, s: (0, 0)),
        row_spec(W), row_spec(W), row_spec(W), row_spec(W), row_spec(W),
    ]
    out_spec = pl.BlockSpec((nb, ts, W), lambda b, s: (b, s, 0))
    out_shape = jax.ShapeDtypeStruct((B, S, W), F32)
    scratch = [pltpu.VMEM((nb, W // LANES, LANES, LANES), F32)]
    return (B // nb, S // ts), args, in_specs, out_spec, out_shape, scratch


def _rwkv(p, *weights, ts=256, nb=2):
    B, S, _ = p.shape
    grid, args, in_specs, out_spec, out_shape, scratch = _rwkv_call_parts(p, *weights, min(ts, S), min(nb, B))
    return pl.pallas_call(
        _rwkv_kernel,
        grid=grid,
        in_specs=in_specs,
        out_specs=out_spec,
        out_shape=out_shape,
        scratch_shapes=scratch,
        compiler_params=_params("parallel", "arbitrary"),
        name="rwkv7",
    )(*args)


def _rel_bias_kernel(tab_ref, o_ref, *, n_blocks):
    blk = MOBA_BLOCK
    max_exact = REL_BUCKETS // 2

    def bucket_of(n):
        large = max_exact + math.floor(math.log(max(n, 1) / max_exact)
                                       / math.log(REL_MAX_DISTANCE / max_exact) * (REL_BUCKETS - max_exact))
        return n if n < max_exact else min(large, REL_BUCKETS - 1)

    def block(cb):
        kc = lax.broadcasted_iota(jnp.int32, (blk, blk), 0) + cb * blk
        qi = lax.broadcasted_iota(jnp.int32, (blk, blk), 1)
        dist = qi + (n_blocks - 1) * blk - kc
        n = jnp.maximum(dist, 0)
        nf = jnp.maximum(n, 1).astype(F32)
        large = max_exact + (jnp.log(nf / max_exact) / math.log(REL_MAX_DISTANCE / max_exact)
                             * (REL_BUCKETS - max_exact)).astype(jnp.int32)
        large = jnp.minimum(large, REL_BUCKETS - 1)
        bucket = jnp.where(n < max_exact, n, large)
        n_lo = max((n_blocks - 1 - cb) * blk - (blk - 1), 0)
        n_hi = max((n_blocks - 1 - cb) * blk + (blk - 1), 0)
        b_lo = max(bucket_of(n_lo) - 1, 0)
        b_hi = min(bucket_of(n_hi) + 1, REL_BUCKETS - 1)
        for h in range(MOBA_HEADS):
            tile = jnp.zeros((blk, blk), F32)
            for c in range(b_lo, b_hi + 1):
                tile = jnp.where(bucket == c, tab_ref[h, c], tile)
            o_ref[h] = jnp.where(dist < 0, -jnp.inf, tile * LOG2E)

    for cb in range(n_blocks):
        pl.when(pl.program_id(0) == cb)(functools.partial(block, cb))


def _rel_bias_strip(rel_bias, n_blocks):
    blk = MOBA_BLOCK
    return pl.pallas_call(
        functools.partial(_rel_bias_kernel, n_blocks=n_blocks),
        grid=(n_blocks,),
        in_specs=[pl.BlockSpec(memory_space=pltpu.SMEM)],
        out_specs=pl.BlockSpec((MOBA_HEADS, blk, blk), lambda c: (0, c, 0)),
        out_shape=jax.ShapeDtypeStruct((MOBA_HEADS, n_blocks * blk, blk), F32),
        compiler_params=_params("parallel"),
        name="rel_bias_strip",
    )(rel_bias)


def _moba_stages(q_ref, k_ref, v_ref, bias_ref, o_ref, *, n_blocks):
    blk = MOBA_BLOCK
    lane = lax.broadcasted_iota(jnp.int32, (1, LANES), 1)
    vrow = lax.broadcasted_iota(jnp.int32, (LANES, 1), 0)
    q = q_ref[0] * (HEAD_DIM ** -0.5 * LOG2E)
    k = k_ref[0]
    k16 = k.astype(BF16)
    vt = v_ref[0].T
    vt16 = [jnp.where((vrow >> HEAD_SHIFT) == e, vt, 1.0).astype(BF16) for e in range(HEADS_PER_TILE)]
    kmean = jnp.mean(k.reshape(n_blocks, blk, LANES), axis=1)

    q_hi, q_lo = _bf16_terms(q, 2)
    m_hi, m_lo = _bf16_terms(kmean, 2)
    gates, q16 = [], []
    for e in range(HEADS_PER_TILE):
        head_lanes = (lane >> HEAD_SHIFT) == e
        zero = jnp.zeros_like(q_hi)
        qe_hi, qe_lo = jnp.where(head_lanes, q_hi, zero), jnp.where(head_lanes, q_lo, zero)
        gates.append(_dot_nt(m_hi, qe_hi) + _dot_nt(m_hi, qe_lo) + _dot_nt(m_lo, qe_hi))
        q16.append(qe_hi)

    def scores(qb, e):
        return _dot_nt(k16[:(qb + 1) * blk], q16[e][qb * blk:(qb + 1) * blk])

    def weights(qb, e, s_t):
        rows = slice(qb * blk, (qb + 1) * blk)
        bias_lo = (n_blocks - 1 - qb) * blk
        g = [gates[e][j:j + 1, rows] for j in range(qb)]
        tiles = []
        for j in range(qb + 1):
            t = s_t[j * blk:(j + 1) * blk] + bias_ref[e, bias_lo + j * blk:bias_lo + (j + 1) * blk, :]
            if j < qb:
                rank = jnp.zeros((1, blk), jnp.int32)
                for jj in range(qb):
                    if jj != j:
                        ahead = (g[jj] >= g[j]) if jj < j else (g[jj] > g[j])
                        rank = rank + jnp.where(ahead, 1, 0)
                t = jnp.where(rank < MOBA_TOPK, t, -jnp.inf)
            tiles.append(t)
        m = tiles[0].max(axis=0, keepdims=True)
        for t in tiles[1:]:
            m = jnp.maximum(m, t.max(axis=0, keepdims=True))
        return jnp.concatenate([jnp.exp2(t - m).astype(BF16) for t in tiles], axis=0)

    def attend(qb, e, p16):
        pv = _dot(vt16[e][:, :(qb + 1) * blk], p16)
        den_row = (1 - e) * HEAD_DIM
        return pv / pv[den_row:den_row + 1]

    units = [(qb, e) for qb in range(n_blocks) for e in range(HEADS_PER_TILE)]
    s_t, p16, out_t = {}, {}, {}
    for step in range(len(units) + 2):
        if step < len(units):
            s_t[units[step]] = scores(*units[step])
        if 0 <= step - 1 < len(units):
            u = units[step - 1]
            p16[u] = weights(*u, s_t.pop(u))
        if 0 <= step - 2 < len(units):
            u = units[step - 2]
            out_t[u] = attend(*u, p16.pop(u))
            qb, e = u
            if e == HEADS_PER_TILE - 1:
                o_ref[0, qb * blk:(qb + 1) * blk, :] = jnp.where(
                    vrow < HEAD_DIM, out_t.pop((qb, 0)), out_t.pop((qb, 1))).T
        yield


def _moba_kernel(*refs, n_blocks):
    for _ in _moba_stages(*refs, n_blocks=n_blocks):
        pass


def _moba(qkv, bias_strip):
    B, S, _ = qkv.shape
    blk = MOBA_BLOCK
    n_blocks = S // blk
    n_pairs = MOBA_WIDTH // LANES
    return pl.pallas_call(
        functools.partial(_moba_kernel, n_blocks=n_blocks),
        grid=(n_pairs, B),
        in_specs=[
            pl.BlockSpec((1, S, LANES), lambda p, b: (b, 0, p)),
            pl.BlockSpec((1, S, LANES), lambda p, b: (b, 0, n_pairs + p)),
            pl.BlockSpec((1, S, LANES), lambda p, b: (b, 0, 2 * n_pairs + p)),
            pl.BlockSpec((HEADS_PER_TILE, S, blk), lambda p, b: (p, 0, 0)),
        ],
        out_specs=pl.BlockSpec((1, S, LANES), lambda p, b: (b, 0, p)),
        out_shape=jax.ShapeDtypeStruct((B, S, MOBA_WIDTH), F32),
        compiler_params=_params("parallel", "parallel"),
        name="moba",
    )(qkv, qkv, qkv, bias_strip)


N_RWKV_INPUTS = 10
N_MOBA_INPUTS = 4


def _mixers_kernel(*refs, n_blocks):
    rwkv_in, refs = refs[:N_RWKV_INPUTS], refs[N_RWKV_INPUTS:]
    moba_in, (y_rwkv_ref, y_moba_ref, state_ref) = refs[:N_MOBA_INPUTS], refs[N_MOBA_INPUTS:]
    jobs = [(_rwkv_stages(*rwkv_in, y_rwkv_ref, state_ref), MIXER_STAGE_RATIO[0]),
            (_moba_stages(*moba_in, y_moba_ref, n_blocks=n_blocks), MIXER_STAGE_RATIO[1])]
    while jobs:
        jobs = [(gen, n) for gen, n in jobs if all(next(gen, jobs) is not jobs for _ in range(n))]


def _mixers(p, qkv, bias_strip, *rwkv_weights, ts=256, nb=2):
    B, S, _ = qkv.shape
    blk = MOBA_BLOCK
    n_blocks = S // blk
    n_pairs = MOBA_WIDTH // LANES
    ts, nb = min(ts, S), min(nb, B)
    grid, args, in_specs, out_spec, out_shape, scratch = _rwkv_call_parts(p, *rwkv_weights, ts, nb)
    if grid[0] * grid[1] != n_pairs * B:
        return _rwkv(p, *rwkv_weights, ts=ts, nb=nb), _moba(qkv, bias_strip)
    item = lambda g, s: g * grid[1] + s
    row = lambda g, s: item(g, s) % B
    pair = lambda g, s: item(g, s) // B
    in_specs = in_specs + [
        pl.BlockSpec((1, S, LANES), lambda g, s: (row(g, s), 0, pair(g, s))),
        pl.BlockSpec((1, S, LANES), lambda g, s: (row(g, s), 0, n_pairs + pair(g, s))),
        pl.BlockSpec((1, S, LANES), lambda g, s: (row(g, s), 0, 2 * n_pairs + pair(g, s))),
        pl.BlockSpec((HEADS_PER_TILE, S, blk), lambda g, s: (pair(g, s), 0, 0)),
    ]
    assert len(args) == N_RWKV_INPUTS
    return pl.pallas_call(
        functools.partial(_mixers_kernel, n_blocks=n_blocks),
        grid=grid,
        in_specs=in_specs,
        out_specs=[out_spec, pl.BlockSpec((1, S, LANES), lambda g, s: (row(g, s), 0, pair(g, s)))],
        out_shape=[out_shape, jax.ShapeDtypeStruct((B, S, MOBA_WIDTH), F32)],
        scratch_shapes=scratch,
        compiler_params=_params("parallel", "arbitrary"),
        name="mixers",
    )(*args, qkv, qkv, qkv, bias_strip)


def _xattn_kernel(x_ref, ya_ref, yb_ref, wa_ref, wb_ref, g_ref, wq_ref, mem_ref, mg_ref, wkv_ref, wo_ref,
                  o_ref, kv_ref, *, sub):
    D = x_ref.shape[-1]
    dh = D // XATTN_HEADS

    @pl.when(pl.program_id(1) == 0)
    def _():
        kv_ref[...] = _dot(_rms(mem_ref[0], mg_ref[...]).astype(BF16), wkv_ref[...]).astype(BF16)

    k16 = [kv_ref[:, i * dh:(i + 1) * dh] for i in range(XATTN_HEADS)]
    v16 = [kv_ref[:, D + i * dh:D + (i + 1) * dh] for i in range(XATTN_HEADS)]

    def project(r0):
        rows = slice(r0, r0 + sub)
        x = (x_ref[0, rows, :] + _dot(ya_ref[0, rows, :].astype(BF16), wa_ref[...])
             + _dot(yb_ref[0, rows, :].astype(BF16), wb_ref[...]))
        h = _rms(x, g_ref[...]).astype(BF16)
        q = (_dot(h, wq_ref[...]) * (dh ** -0.5 * LOG2E)).astype(BF16)
        return x, q

    def attend(q):
        hs = range(XATTN_HEADS)
        s = [_dot_nt(q[:, i * dh:(i + 1) * dh], k16[i]) for i in hs]
        pr = [jnp.exp2(s[i] - jnp.max(s[i], axis=-1, keepdims=True)) for i in hs]
        pr = [(pr[i] / jnp.sum(pr[i], axis=-1, keepdims=True)).astype(BF16) for i in hs]
        return jnp.concatenate([_dot(pr[i], v16[i]) for i in hs], axis=1).astype(BF16)

    starts = list(range(0, x_ref.shape[1], sub))
    xq, att = {}, {}
    for step in range(len(starts) + 2):
        if step < len(starts):
            xq[starts[step]] = project(starts[step])
        if 0 <= step - 1 < len(starts):
            r0 = starts[step - 1]
            att[r0] = attend(xq[r0][1])
        if 0 <= step - 2 < len(starts):
            r0 = starts[step - 2]
            o_ref[0, r0:r0 + sub, :] = xq.pop(r0)[0] + _dot(att.pop(r0), wo_ref[...])


def _xattn(x, ya, yb, w_mix, g, w_q, mem, mem_g, w_kv, w_o, layer, *, tm=512, sub=512):
    B, S, D = x.shape
    M = mem.shape[1]
    na, nb = ya.shape[-1], yb.shape[-1]
    assert na == nb
    tm = min(tm, S)
    of_layer = lambda shape, blk=0: pl.BlockSpec((None,) + shape, lambda b, i: (layer, blk, 0))
    return pl.pallas_call(
        functools.partial(_xattn_kernel, sub=min(sub, tm)),
        grid=(B, S // tm),
        in_specs=[
            pl.BlockSpec((1, tm, D), lambda b, i: (b, i, 0)),
            pl.BlockSpec((1, tm, na), lambda b, i: (b, i, 0)),
            pl.BlockSpec((1, tm, nb), lambda b, i: (b, i, 0)),
            of_layer((na, D), 0),
            of_layer((nb, D), 1),
            pl.BlockSpec((1, D), lambda b, i: (0, 0)),
            of_layer((D, D)),
            pl.BlockSpec((1, M, D), lambda b, i: (b, 0, 0)),
            pl.BlockSpec((1, D), lambda b, i: (0, 0)),
            of_layer((D, 2 * D)),
            of_layer((D, D)),
        ],
        out_specs=pl.BlockSpec((1, tm, D), lambda b, i: (b, i, 0)),
        out_shape=jax.ShapeDtypeStruct((B, S, D), F32),
        scratch_shapes=[pltpu.VMEM((M, 2 * D), BF16)],
        compiler_params=_params("parallel", "arbitrary"),
        name="xattn",
    )(x, ya, yb, w_mix, w_mix, g.reshape(1, D), w_q, mem, mem_g.reshape(1, D), w_kv, w_o)


def kernel(x, mem, rel_bias, final_norm_g, ffn1_norm_g, ffn1_w_in, ffn1_w_out, mix_norm_g, w_mix_in, w_mix_out, rwkv_mu, rwkv_w0, rwkv_w_up, rwkv_a0, rwkv_a_up, rwkv_g_up, rwkv_k_k, rwkv_k_a, rwkv_r_k, rwkv_ln_g, rwkv_ln_b, xattn_norm_g, mem_norm_g, xattn_w_q, xattn_w_kv, xattn_w_o, ffn2_norm_g, ffn2_w_in, ffn2_w_out):
    B, S, D = x.shape
    M = mem.shape[1]
    depth = ffn1_w_in.shape[0]
    T = B * S
    ffn1_in16, ffn1_out16 = ffn1_w_in[:1].astype(BF16), ffn1_w_out[:1].astype(BF16)

    bias_strip = _rel_bias_strip(rel_bias, S // MOBA_BLOCK)
    x = x.reshape(T, D)
    for l in range(depth):
        later = (w_mix_in, xattn_w_kv, w_mix_out, xattn_w_q, xattn_w_o, ffn2_w_in, ffn2_w_out)
        x, (mix_in16, kv16, mix_out16, q16, o16, ffn2_in16, ffn2_out16) = _ffn(
            x, ffn1_norm_g[l], ffn1_in16, ffn1_out16, 0, side=[(w, l) for w in later])
        p_rwkv, qkv = _norm_proj(x, mix_norm_g[l], mix_in16, 0, (RWKV_PROJ, MOBA_PROJ),
                                 shift_mu=rwkv_mu[l], seq_len=S)
        y_rwkv, y_moba = _mixers(p_rwkv.reshape(B, S, RWKV_PROJ), qkv.reshape(B, S, MOBA_PROJ), bias_strip,
                                 rwkv_w0[l], rwkv_w_up[l], rwkv_a0[l], rwkv_a_up[l], rwkv_g_up[l],
                                 rwkv_k_k[l], rwkv_k_a[l], rwkv_r_k[l], rwkv_ln_g[l], rwkv_ln_b[l])
        x = _xattn(x.reshape(B, S, D), y_rwkv, y_moba, mix_out16, xattn_norm_g[l],
                   q16, mem, mem_norm_g[l], kv16, o16, 0).reshape(T, D)
        last = l == depth - 1
        x, next_ffn1 = _ffn(x, ffn2_norm_g[l], ffn2_in16, ffn2_out16, 0, final_norm_g if last else None,
                            side=[] if last else [(ffn1_w_in, l + 1), (ffn1_w_out, l + 1)])
        if not last:
            ffn1_in16, ffn1_out16 = next_ffn1
    return x.reshape(B, S, D)
```

```python
import functools
import math

import jax
import jax.numpy as jnp
from jax import lax
from jax.experimental import pallas as pl
from jax.experimental.pallas import tpu as pltpu

F32 = jnp.float32
BF16 = jnp.bfloat16

HEAD_DIM = 64
RWKV_WIDTH = 512
MOBA_WIDTH = 512
MOBA_HEADS = MOBA_WIDTH // HEAD_DIM
DECAY_LORA = 64
ICLR_LORA = 64
GATE_LORA = 128
RWKV_PROJ = 3 * RWKV_WIDTH + DECAY_LORA + ICLR_LORA + GATE_LORA
MOBA_PROJ = 3 * MOBA_WIDTH
LNX_EPS = 64e-5
MOBA_BLOCK = 256
MOBA_TOPK = 3
REL_BUCKETS = 32
REL_MAX_DISTANCE = 1024
XATTN_HEADS = 4
FFN_RES_WEIGHT = 0.5
NORM_EPS = 1e-6
LOG2E = math.log2(math.e)

LANES = 128
MXU_DIM = 256
HEADS_PER_TILE = LANES // HEAD_DIM
HEAD_SHIFT = HEAD_DIM.bit_length() - 1
RWKV_CHUNK = 64
CHUNK_SHIFT = RWKV_CHUNK.bit_length() - 1
RWKV_ROWS_IN_STEP = 2
MIXER_STAGE_RATIO = (3, 2)
SPLIT_LORA = (1, 1, 1)
SPLIT_SUM = (1, 1, 1)
SPLIT_CUMSUM = (1, 2, 2)
VMEM_LIMIT = 56 * 1024 * 1024


def _rms(x, g):
    ms = jnp.mean(x * x, axis=-1, keepdims=True)
    return x * lax.rsqrt(ms + NORM_EPS) * g


def _sigmoid(x):
    return 0.5 * jnp.tanh(0.5 * x) + 0.5


def _dot(a, b, precision=None):
    return jnp.dot(a, b, precision=precision, preferred_element_type=F32)


def _dot_nt(a, b, precision=None):
    return lax.dot_general(a, b, (((1,), (1,)), ((), ())), precision=precision,
                           preferred_element_type=F32)


def _dot_tn(a, b, precision=None):
    return lax.dot_general(a, b, (((0,), (0,)), ((), ())), precision=precision,
                           preferred_element_type=F32)


def _bf16_terms(x, n):
    terms = []
    for i in range(n):
        t = x.astype(BF16)
        terms.append(t)
        if i + 1 < n:
            x = x - t.astype(F32)
    return terms


def _mm(dot, a, b, split):
    na, nb, order = split
    at, bt = _bf16_terms(a, na), _bf16_terms(b, nb)
    acc = None
    for i in range(na):
        for j in range(nb):
            if i + j < order:
                t = dot(at[i], bt[j])
                acc = t if acc is None else acc + t
    return acc


def _params(*semantics):
    return pltpu.CompilerParams(dimension_semantics=semantics, vmem_limit_bytes=VMEM_LIMIT)


def _ffn_kernel(x_ref, g_ref, wi_ref, wo_ref, *rest, final_norm, sub, n_side):
    rest = list(rest)
    fg_ref = rest.pop(0) if final_norm else None
    side_in, o_ref, side_out = rest[:n_side], rest[n_side], rest[n_side + 1:]
    F = wo_ref.shape[0]
    for r0 in range(0, x_ref.shape[0], sub):
        x = x_ref[r0:r0 + sub, :]
        h = _rms(x, g_ref[...]).astype(BF16)
        gate = _dot(h, wi_ref[:, :F])
        up = _dot(h, wi_ref[:, F:])
        act = (gate * _sigmoid(gate) * up).astype(BF16)
        y = x + FFN_RES_WEIGHT * _dot(act, wo_ref[...])
        if final_norm:
            y = _rms(y, fg_ref[...])
        o_ref[r0:r0 + sub, :] = y
    for w_ref, w16_ref in zip(side_in, side_out):
        w16_ref[...] = w_ref[...].astype(BF16)


def _side_cast_parts(side, steps):
    args, in_specs, out_specs, out_shape = [], [], [], []
    for w, w_layer in side:
        n_layers, rows, cols = w.shape
        assert rows % steps == 0
        slab = (rows // steps, cols)
        in_specs.append(pl.BlockSpec((None, None) + slab, lambda i, w_layer=w_layer: (w_layer, i, 0, 0)))
        args.append(w.reshape((n_layers, steps) + slab))
        out_specs.append(pl.BlockSpec((None,) + slab, lambda i: (i, 0, 0)))
        out_shape.append(jax.ShapeDtypeStruct((steps,) + slab, BF16))
    return args, in_specs, out_specs, out_shape


def _ffn(x, g, w_in, w_out, layer, final_g=None, side=(), *, tm=1024, sub=256):
    T, D = x.shape
    F = w_out.shape[1]
    tm = min(tm, T)
    steps = T // tm
    resident = lambda shape: pl.BlockSpec(shape, lambda i: (0, 0), pipeline_mode=pl.Buffered(1))
    of_layer = lambda shape: pl.BlockSpec((None,) + shape, lambda i: (layer, 0, 0),
                                          pipeline_mode=pl.Buffered(1))
    in_specs = [
        pl.BlockSpec((tm, D), lambda i: (i, 0)),
        resident((1, D)),
        of_layer((D, 2 * F)),
        of_layer((F, D)),
    ]
    args = [x, g.reshape(1, D), w_in, w_out]
    if final_g is not None:
        in_specs.append(resident((1, D)))
        args.append(final_g.reshape(1, D))
    side_args, side_in, side_out, side_shape = _side_cast_parts(side, steps)
    y, *side16 = pl.pallas_call(
        functools.partial(_ffn_kernel, final_norm=final_g is not None, sub=min(sub, tm), n_side=len(side)),
        grid=(steps,),
        in_specs=in_specs + side_in,
        out_specs=[pl.BlockSpec((tm, D), lambda i: (i, 0))] + side_out,
        out_shape=[jax.ShapeDtypeStruct((T, D), F32)] + side_shape,
        compiler_params=_params("parallel"),
        name="ffn",
    )(*args, *side_args)
    return y, [w16.reshape((1,) + w.shape[1:]) for w16, (w, _) in zip(side16, side)]


def _norm_proj_kernel(x_ref, g_ref, w_ref, *rest, splits, tiles_per_seq, sub):
    if tiles_per_seq:
        mu_ref, *o_refs, prev_ref = rest

        @pl.when(pl.program_id(0) % tiles_per_seq == 0)
        def _():
            prev_ref[...] = jnp.zeros_like(prev_ref)

        before = prev_ref[...]
    else:
        o_refs = rest
    row = lax.broadcasted_iota(jnp.int32, (sub, 1), 0)
    for r0 in range(0, x_ref.shape[0], sub):
        h = _rms(x_ref[r0:r0 + sub, :], g_ref[...]).astype(BF16)
        off = 0
        for idx, (o_ref, n) in enumerate(zip(o_refs, splits)):
            y = _dot(h, w_ref[:, off:off + n])
            if tiles_per_seq and idx == 0:
                shifted = jnp.where(row == 0, before, pltpu.roll(y, 1, axis=0))
                before = y[sub - 1:sub, :]
                y = y + (shifted - y) * mu_ref[...]
            o_ref[r0:r0 + sub, :] = y
            off += n
    if tiles_per_seq:
        prev_ref[...] = before


def _norm_proj(x, g, w, layer, splits, *, shift_mu=None, seq_len=None, tm=1024, sub=128):
    T, D = x.shape
    N = w.shape[2]
    tm = min(tm, T)
    assert sum(splits) == N
    in_specs = [
        pl.BlockSpec((tm, D), lambda i: (i, 0)),
        pl.BlockSpec((1, D), lambda i: (0, 0)),
        pl.BlockSpec((None, D, N), lambda i: (layer, 0, 0)),
    ]
    args = [x, g.reshape(1, D), w]
    scratch, tiles_per_seq = [], 0
    if shift_mu is not None:
        assert seq_len % tm == 0
        tiles_per_seq = seq_len // tm
        in_specs.append(pl.BlockSpec((1, splits[0]), lambda i: (0, 0)))
        args.append(shift_mu.reshape(1, splits[0]))
        scratch = [pltpu.VMEM((1, splits[0]), F32)]
    return pl.pallas_call(
        functools.partial(_norm_proj_kernel, splits=splits, tiles_per_seq=tiles_per_seq,
                          sub=min(sub, tm)),
        grid=(T // tm,),
        in_specs=in_specs,
        out_specs=[pl.BlockSpec((tm, n), lambda i: (i, 0)) for n in splits],
        out_shape=[jax.ShapeDtypeStruct((T, n), F32) for n in splits],
        scratch_shapes=scratch,
        compiler_params=_params("arbitrary" if tiles_per_seq else "parallel"),
        name="norm_proj",
    )(*args)


def _rwkv_stages(p_ref, w0_ref, a0_ref, wwa_ref, gup_ref, kk_ref, ka_ref, rk_ref,
                 lng_ref, lnb_ref, o_ref, state_ref):
    C = RWKV_CHUNK
    W = RWKV_WIDTH
    NB, T, P = p_ref.shape
    n_pairs = W // LANES
    n_chunks = T // C

    @pl.when(pl.program_id(1) == 0)
    def _():
        state_ref[...] = jnp.zeros_like(state_ref)

    lane = lax.broadcasted_iota(jnp.int32, (1, LANES), 1)
    first_head = lane < HEAD_DIM
    ri = lax.broadcasted_iota(jnp.int32, (LANES, LANES), 0)
    ci = lax.broadcasted_iota(jnp.int32, (LANES, LANES), 1)
    same_head = (ri >> HEAD_SHIFT) == (ci >> HEAD_SHIFT)
    ri2 = lax.broadcasted_iota(jnp.int32, (MXU_DIM, MXU_DIM), 0)
    ci2 = lax.broadcasted_iota(jnp.int32, (MXU_DIM, MXU_DIM), 1)
    head_ones = ((ri2 >> HEAD_SHIFT) == (ci2 >> HEAD_SHIFT)).astype(F32)
    in_chunk_prefix = ((ci2 <= ri2) & ((ci2 >> CHUNK_SHIFT) == (ri2 >> CHUNK_SHIFT))).astype(F32)
    tok = lax.broadcasted_iota(jnp.int32, (C, LANES), 0)
    col_tok = lax.broadcasted_iota(jnp.int32, (C, LANES), 1) & (C - 1)
    strict = col_tok < tok
    incl = col_tok <= tok
    incl2 = jnp.concatenate([incl, incl], axis=1)
    eye = (col_tok == tok).astype(F32)
    lanes = [slice(i * LANES, (i + 1) * LANES) for i in range(n_pairs)]
    rows_of = lambda c: slice(c * C, (c + 1) * C)
    units = [(c, i) for c in range(n_chunks) for i in range(n_pairs)]
    c16 = lambda t: t.astype(BF16)

    def head_sum(t):
        return jnp.concatenate(
            [_mm(_dot, t[:, i * MXU_DIM:(i + 1) * MXU_DIM], head_ones, SPLIT_SUM)
             for i in range(W // MXU_DIM)],
            axis=1)

    def stack(t):
        zero = jnp.zeros_like(t)
        return jnp.concatenate([jnp.where(first_head, t, zero), jnp.where(first_head, zero, t)], axis=0)

    def tokenwise(nb, out):
        p = p_ref[nb]
        r, k, v = p[:, 0:W], p[:, W:2 * W], p[:, 2 * W:3 * W]
        lora = p[:, 3 * W:3 * W + LANES]
        g_lo = p[:, 3 * W + LANES:3 * W + 2 * LANES]
        z = jnp.where(first_head, jnp.tanh(lora), lora)
        wa = _mm(_dot, z, wwa_ref[...], SPLIT_LORA)
        g = _mm(_dot, _sigmoid(g_lo), gup_ref[...], SPLIT_LORA)
        yield
        logw = -math.exp(-0.5) * _sigmoid(w0_ref[...] + wa[:, :W])
        a = _sigmoid(a0_ref[...] + wa[:, W:])
        kk = k * kk_ref[...]
        k2 = k * (1.0 + (a - 1.0) * ka_ref[...])
        yield
        kkn = kk * lax.rsqrt(jnp.maximum(head_sum(kk * kk), 1e-24))
        b = kkn * a
        yield
        bonus = head_sum(r * k2 * rk_ref[...]) * v
        logp = jnp.concatenate(
            [_mm(_dot, in_chunk_prefix, logw[i * MXU_DIM:(i + 1) * MXU_DIM], SPLIT_CUMSUM)
             for i in range(T // MXU_DIM)], axis=0)
        yield
        inv = jnp.exp(-logp)
        out.update(a16=c16(-kkn * jnp.exp(logp - logw)), r16=c16(r * jnp.exp(logp)), v16=c16(v))
        yield
        out.update(b16=c16(b * inv), k16=c16(k2 * inv), b=b, k2=k2, logp=logp, bonus=bonus, g=g)

    def chunkwise(tw, out):
        a16, r16, b16, k16, v16 = tw["a16"], tw["r16"], tw["b16"], tw["k16"], tw["v16"]
        ar = {(c, i): jnp.concatenate([a16[rows_of(c), lanes[i]], r16[rows_of(c), lanes[i]]], axis=0)
              for c, i in units}
        bk = {(c, i): jnp.concatenate([stack(b16[rows_of(c), lanes[i]]), stack(k16[rows_of(c), lanes[i]])],
                                      axis=0) for c, i in units}
        v_s = {(c, i): stack(v16[rows_of(c), lanes[i]]) for c, i in units}
        yield
        gram = {u_: _dot_nt(ar[u_], bk[u_]) for u_ in units}
        yield
        l_ab = {u_: jnp.where(strict, gram[u_][:C, :LANES], 0.0) for u_ in units}
        a_ak = {u_: c16(jnp.where(strict, gram[u_][:C, LANES:], 0.0)) for u_ in units}
        a_r = {u_: c16(jnp.where(incl2, gram[u_][C:], 0.0)) for u_ in units}
        t_inv = {u_: eye + l_ab[u_] for u_ in units}
        lp16 = {u_: c16(l_ab[u_]) for u_ in units}
        yield
        l_pow = {u_: _dot(lp16[u_], stack(lp16[u_])) for u_ in units}
        av = {u_: _dot(a_ak[u_], v_s[u_]) for u_ in units}
        yield
        levels = int(math.log2(C)) - 1
        for level in range(levels - 1):
            lp16 = {u_: c16(l_pow[u_]) for u_ in units}
            both = {u_: _dot(jnp.concatenate([c16(t_inv[u_]), lp16[u_]], axis=0), stack(lp16[u_]))
                    for u_ in units}
            yield
            t_inv = {u_: t_inv[u_] + both[u_][:C] for u_ in units}
            l_pow = {u_: both[u_][C:] for u_ in units}
        last = {u_: _dot(c16(t_inv[u_]), stack(c16(l_pow[u_]))) for u_ in units}
        yield
        out.update(ar=ar, v_s=v_s, a_r=a_r, av=av, t16={u_: c16(t_inv[u_] + last[u_]) for u_ in units})

    def recurrent(nb, tw, cw, out):
        b, k2, logp, v16 = tw["b"], tw["k2"], tw["logp"], tw["v16"]
        ar, v_s, a_r, av, t16 = cw["ar"], cw["v_s"], cw["a_r"], cw["av"], cw["t16"]
        states = [state_ref[nb, i] for i in range(n_pairs)]
        y_chunks = []
        for c in range(n_chunks):
            logp_end = logp[(c + 1) * C - 1:(c + 1) * C, :]
            to_end = jnp.exp(logp_end - logp[rows_of(c)])
            bk_e = [c16(jnp.concatenate([b[rows_of(c), sl] * to_end[:, sl], k2[rows_of(c), sl] * to_end[:, sl]],
                                        axis=0)) for sl in lanes]
            from_state = [_dot_nt(ar[c, i], c16(states[i])) for i in range(n_pairs)]
            yield
            u16 = [c16(_dot(t16[c, i], stack(c16(from_state[i][:C] + av[c, i])))) for i in range(n_pairs)]
            yield
            outer = [_dot_tn(jnp.concatenate([u16[i], v16[rows_of(c), lanes[i]]], axis=0), bk_e[i])
                     for i in range(n_pairs)]
            y_chunks.append(jnp.concatenate(
                [from_state[i][C:] + _dot(a_r[c, i], jnp.concatenate([stack(u16[i]), v_s[c, i]], axis=0))
                 for i in range(n_pairs)], axis=1))
            yield
            decay_end = jnp.exp(logp_end)
            states = [states[i] * decay_end[:, lanes[i]] + jnp.where(same_head, outer[i], 0.0)
                      for i in range(n_pairs)]
        for i in range(n_pairs):
            state_ref[nb, i] = states[i]
        out.update(y=jnp.concatenate(y_chunks, axis=0))

    def finish(nb, tw, rc):
        y = rc["y"]
        mean = head_sum(y) * (1.0 / HEAD_DIM)
        yield
        yc = y - mean
        var = head_sum(yc * yc) * (1.0 / HEAD_DIM)
        yield
        y = yc * lax.rsqrt(var + LNX_EPS) * lng_ref[...] + lnb_ref[...]
        o_ref[nb] = (y + tw["bonus"]) * tw["g"]

    def in_turn(gens):
        while gens:
            gens = [gen for gen in gens if next(gen, gens) is not gens]
            yield

    tw = [dict() for _ in range(NB)]
    cw = [dict() for _ in range(NB)]
    rc = [dict() for _ in range(NB)]
    groups = [range(g0, min(g0 + RWKV_ROWS_IN_STEP, NB)) for g0 in range(0, NB, RWKV_ROWS_IN_STEP)]
    phases = [
        lambda rows: in_turn([tokenwise(nb, tw[nb]) for nb in rows]),
        lambda rows: in_turn([chunkwise(tw[nb], cw[nb]) for nb in rows]),
        lambda rows: in_turn([recurrent(nb, tw[nb], cw[nb], rc[nb]) for nb in rows]),
        lambda rows: in_turn([finish(nb, tw[nb], rc[nb]) for nb in rows]),
    ]
    for wave in range(len(groups) + len(phases) - 1):
        yield from in_turn([phases[wave - g](rows) for g, rows in enumerate(groups)
                            if 0 <= wave - g < len(phases)])


def _rwkv_kernel(*refs):
    for _ in _rwkv_stages(*refs):
        pass


def _rwkv_call_parts(p, w0, w_up, a0, a_up, g_up, k_k, k_a, r_k, ln_g, ln_b, ts, nb):
    B, S, P = p.shape
    W = RWKV_WIDTH
    wwa = jnp.zeros((LANES, 2 * W), F32)
    wwa = wwa.at[:DECAY_LORA, :W].set(w_up).at[DECAY_LORA:, W:].set(a_up)
    vec = lambda t: t.reshape(1, -1)
    row_spec = lambda n: pl.BlockSpec((1, n), lambda b, s: (0, 0))
    assert ts % MXU_DIM == 0 and S % ts == 0 and B % nb == 0
    args = (p, vec(w0), vec(a0), wwa, g_up, vec(k_k), vec(k_a), vec(r_k), vec(ln_g), vec(ln_b))
    in_specs = [
        pl.BlockSpec((nb, ts, P), lambda b, s: (b, s, 0)),
        row_spec(W), row_spec(W),
        pl.BlockSpec((LANES, 2 * W), lambda b, s: (0, 0)),
        pl.BlockSpec((GATE_LORA, W), lambda b, s: (0, 0)),
        row_spec(W), row_spec(W), row_spec(W), row_spec(W), row_spec(W),
    ]
    out_spec = pl.BlockSpec((nb, ts, W), lambda b, s: (b, s, 0))
    out_shape = jax.ShapeDtypeStruct((B, S, W), F32)
    scratch = [pltpu.VMEM((nb, W // LANES, LANES, LANES), F32)]
    return (B // nb, S // ts), args, in_specs, out_spec, out_shape, scratch


def _rwkv(p, *weights, ts=256, nb=2):
    B, S, _ = p.shape
    grid, args, in_specs, out_spec, out_shape, scratch = _rwkv_call_parts(p, *weights, min(ts, S), min(nb, B))
    return pl.pallas_call(
        _rwkv_kernel,
        grid=grid,
        in_specs=in_specs,
        out_specs=out_spec,
        out_shape=out_shape,
        scratch_shapes=scratch,
        compiler_params=_params("parallel", "arbitrary"),
        name="rwkv7",
    )(*args)


def _rel_bias_kernel(tab_ref, *refs, n_blocks, n_side):
    side_in, o_ref, side_out = refs[:n_side], refs[n_side], refs[n_side + 1:]
    blk = MOBA_BLOCK
    max_exact = REL_BUCKETS // 2

    def bucket_of(n):
        large = max_exact + math.floor(math.log(max(n, 1) / max_exact)
                                       / math.log(REL_MAX_DISTANCE / max_exact) * (REL_BUCKETS - max_exact))
        return n if n < max_exact else min(large, REL_BUCKETS - 1)

    def block(cb):
        kc = lax.broadcasted_iota(jnp.int32, (blk, blk), 0) + cb * blk
        qi = lax.broadcasted_iota(jnp.int32, (blk, blk), 1)
        dist = qi + (n_blocks - 1) * blk - kc
        n = jnp.maximum(dist, 0)
        nf = jnp.maximum(n, 1).astype(F32)
        large = max_exact + (jnp.log(nf / max_exact) / math.log(REL_MAX_DISTANCE / max_exact)
                             * (REL_BUCKETS - max_exact)).astype(jnp.int32)
        large = jnp.minimum(large, REL_BUCKETS - 1)
        bucket = jnp.where(n < max_exact, n, large)
        n_lo = max((n_blocks - 1 - cb) * blk - (blk - 1), 0)
        n_hi = max((n_blocks - 1 - cb) * blk + (blk - 1), 0)
        b_lo = max(bucket_of(n_lo) - 1, 0)
        b_hi = min(bucket_of(n_hi) + 1, REL_BUCKETS - 1)
        for h in range(MOBA_HEADS):
            tile = jnp.zeros((blk, blk), F32)
            for c in range(b_lo, b_hi + 1):
                tile = jnp.where(bucket == c, tab_ref[h, c], tile)
            o_ref[h] = jnp.where(dist < 0, -jnp.inf, tile * LOG2E)

    for cb in range(n_blocks):
        pl.when(pl.program_id(0) == cb)(functools.partial(block, cb))
    for w_ref, w16_ref in zip(side_in, side_out):
        w16_ref[...] = w_ref[...].astype(BF16)


def _rel_bias_strip(rel_bias, n_blocks, side=()):
    blk = MOBA_BLOCK
    side_args, side_in, side_out, side_shape = _side_cast_parts(side, n_blocks)
    strip, *side16 = pl.pallas_call(
        functools.partial(_rel_bias_kernel, n_blocks=n_blocks, n_side=len(side)),
        grid=(n_blocks,),
        in_specs=[pl.BlockSpec(memory_space=pltpu.SMEM)] + side_in,
        out_specs=[pl.BlockSpec((MOBA_HEADS, blk, blk), lambda c: (0, c, 0))] + side_out,
        out_shape=[jax.ShapeDtypeStruct((MOBA_HEADS, n_blocks * blk, blk), F32)] + side_shape,
        compiler_params=_params("parallel"),
        name="rel_bias_strip",
    )(rel_bias, *side_args)
    return strip, [w16.reshape((1,) + w.shape[1:]) for w16, (w, _) in zip(side16, side)]


def _moba_stages(q_ref, k_ref, v_ref, bias_ref, o_ref, *, n_blocks):
    blk = MOBA_BLOCK
    lane = lax.broadcasted_iota(jnp.int32, (1, LANES), 1)
    vrow = lax.broadcasted_iota(jnp.int32, (LANES, 1), 0)
    q = q_ref[0] * (HEAD_DIM ** -0.5 * LOG2E)
    k = k_ref[0]
    k16 = k.astype(BF16)
    vt = v_ref[0].T
    vt16 = [jnp.where((vrow >> HEAD_SHIFT) == e, vt, 1.0).astype(BF16) for e in range(HEADS_PER_TILE)]
    kmean = jnp.mean(k.reshape(n_blocks, blk, LANES), axis=1)

    q_hi, q_lo = _bf16_terms(q, 2)
    m_hi, m_lo = _bf16_terms(kmean, 2)
    gates, q16 = [], []
    for e in range(HEADS_PER_TILE):
        head_lanes = (lane >> HEAD_SHIFT) == e
        zero = jnp.zeros_like(q_hi)
        qe_hi, qe_lo = jnp.where(head_lanes, q_hi, zero), jnp.where(head_lanes, q_lo, zero)
        gates.append(_dot_nt(m_hi, qe_hi) + _dot_nt(m_hi, qe_lo) + _dot_nt(m_lo, qe_hi))
        q16.append(qe_hi)

    def scores(qb, e):
        return _dot_nt(k16[:(qb + 1) * blk], q16[e][qb * blk:(qb + 1) * blk])

    def weights(qb, e, s_t):
        rows = slice(qb * blk, (qb + 1) * blk)
        bias_lo = (n_blocks - 1 - qb) * blk
        g = [gates[e][j:j + 1, rows] for j in range(qb)]
        tiles = []
        for j in range(qb + 1):
            t = s_t[j * blk:(j + 1) * blk] + bias_ref[e, bias_lo + j * blk:bias_lo + (j + 1) * blk, :]
            if j < qb:
                rank = jnp.zeros((1, blk), jnp.int32)
                for jj in range(qb):
                    if jj != j:
                        ahead = (g[jj] >= g[j]) if jj < j else (g[jj] > g[j])
                        rank = rank + jnp.where(ahead, 1, 0)
                t = jnp.where(rank < MOBA_TOPK, t, -jnp.inf)
            tiles.append(t)
        m = tiles[0].max(axis=0, keepdims=True)
        for t in tiles[1:]:
            m = jnp.maximum(m, t.max(axis=0, keepdims=True))
        return jnp.concatenate([jnp.exp2(t - m).astype(BF16) for t in tiles], axis=0)

    def attend(qb, e, p16):
        pv = _dot(vt16[e][:, :(qb + 1) * blk], p16)
        den_row = (1 - e) * HEAD_DIM
        return pv / pv[den_row:den_row + 1]

    units = [(qb, e) for qb in range(n_blocks) for e in range(HEADS_PER_TILE)]
    s_t, p16, out_t = {}, {}, {}
    for step in range(len(units) + 2):
        if step < len(units):
            s_t[units[step]] = scores(*units[step])
        if 0 <= step - 1 < len(units):
            u = units[step - 1]
            p16[u] = weights(*u, s_t.pop(u))
        if 0 <= step - 2 < len(units):
            u = units[step - 2]
            out_t[u] = attend(*u, p16.pop(u))
            qb, e = u
            if e == HEADS_PER_TILE - 1:
                o_ref[0, qb * blk:(qb + 1) * blk, :] = jnp.where(
                    vrow < HEAD_DIM, out_t.pop((qb, 0)), out_t.pop((qb, 1))).T
        yield


def _moba_kernel(*refs, n_blocks):
    for _ in _moba_stages(*refs, n_blocks=n_blocks):
        pass


def _moba(qkv, bias_strip):
    B, S, _ = qkv.shape
    blk = MOBA_BLOCK
    n_blocks = S // blk
    n_pairs = MOBA_WIDTH // LANES
    return pl.pallas_call(
        functools.partial(_moba_kernel, n_blocks=n_blocks),
        grid=(n_pairs, B),
        in_specs=[
            pl.BlockSpec((1, S, LANES), lambda p, b: (b, 0, p)),
            pl.BlockSpec((1, S, LANES), lambda p, b: (b, 0, n_pairs + p)),
            pl.BlockSpec((1, S, LANES), lambda p, b: (b, 0, 2 * n_pairs + p)),
            pl.BlockSpec((HEADS_PER_TILE, S, blk), lambda p, b: (p, 0, 0)),
        ],
        out_specs=pl.BlockSpec((1, S, LANES), lambda p, b: (b, 0, p)),
        out_shape=jax.ShapeDtypeStruct((B, S, MOBA_WIDTH), F32),
        compiler_params=_params("parallel", "parallel"),
        name="moba",
    )(qkv, qkv, qkv, bias_strip)


N_RWKV_INPUTS = 10
N_MOBA_INPUTS = 4


def _mixers_kernel(*refs, n_blocks):
    rwkv_in, refs = refs[:N_RWKV_INPUTS], refs[N_RWKV_INPUTS:]
    moba_in, (y_rwkv_ref, y_moba_ref, state_ref) = refs[:N_MOBA_INPUTS], refs[N_MOBA_INPUTS:]
    jobs = [(_rwkv_stages(*rwkv_in, y_rwkv_ref, state_ref), MIXER_STAGE_RATIO[0]),
            (_moba_stages(*moba_in, y_moba_ref, n_blocks=n_blocks), MIXER_STAGE_RATIO[1])]
    while jobs:
        jobs = [(gen, n) for gen, n in jobs if all(next(gen, jobs) is not jobs for _ in range(n))]


def _mixers(p, qkv, bias_strip, *rwkv_weights, ts=256, nb=2):
    B, S, _ = qkv.shape
    blk = MOBA_BLOCK
    n_blocks = S // blk
    n_pairs = MOBA_WIDTH // LANES
    ts, nb = min(ts, S), min(nb, B)
    grid, args, in_specs, out_spec, out_shape, scratch = _rwkv_call_parts(p, *rwkv_weights, ts, nb)
    if grid[0] * grid[1] != n_pairs * B:
        return _rwkv(p, *rwkv_weights, ts=ts, nb=nb), _moba(qkv, bias_strip)
    item = lambda g, s: g * grid[1] + s
    row = lambda g, s: item(g, s) % B
    pair = lambda g, s: item(g, s) // B
    in_specs = in_specs + [
        pl.BlockSpec((1, S, LANES), lambda g, s: (row(g, s), 0, pair(g, s))),
        pl.BlockSpec((1, S, LANES), lambda g, s: (row(g, s), 0, n_pairs + pair(g, s))),
        pl.BlockSpec((1, S, LANES), lambda g, s: (row(g, s), 0, 2 * n_pairs + pair(g, s))),
        pl.BlockSpec((HEADS_PER_TILE, S, blk), lambda g, s: (pair(g, s), 0, 0)),
    ]
    assert len(args) == N_RWKV_INPUTS
    return pl.pallas_call(
        functools.partial(_mixers_kernel, n_blocks=n_blocks),
        grid=grid,
        in_specs=in_specs,
        out_specs=[out_spec, pl.BlockSpec((1, S, LANES), lambda g, s: (row(g, s), 0, pair(g, s)))],
        out_shape=[out_shape, jax.ShapeDtypeStruct((B, S, MOBA_WIDTH), F32)],
        scratch_shapes=scratch,
        compiler_params=_params("parallel", "arbitrary"),
        name="mixers",
    )(*args, qkv, qkv, qkv, bias_strip)


def _xattn_kernel(x_ref, ya_ref, yb_ref, wa_ref, wb_ref, g_ref, wq_ref, mem_ref, mg_ref, wkv_ref, wo_ref,
                  o_ref, kv_ref, *, sub):
    D = x_ref.shape[-1]
    dh = D // XATTN_HEADS

    @pl.when(pl.program_id(1) == 0)
    def _():
        kv_ref[...] = _dot(_rms(mem_ref[0], mg_ref[...]).astype(BF16), wkv_ref[...]).astype(BF16)

    k16 = [kv_ref[:, i * dh:(i + 1) * dh] for i in range(XATTN_HEADS)]
    v16 = [kv_ref[:, D + i * dh:D + (i + 1) * dh] for i in range(XATTN_HEADS)]

    def project(r0):
        rows = slice(r0, r0 + sub)
        x = (x_ref[0, rows, :] + _dot(ya_ref[0, rows, :].astype(BF16), wa_ref[...])
             + _dot(yb_ref[0, rows, :].astype(BF16), wb_ref[...]))
        h = _rms(x, g_ref[...]).astype(BF16)
        q = (_dot(h, wq_ref[...]) * (dh ** -0.5 * LOG2E)).astype(BF16)
        return x, q

    def attend(q):
        hs = range(XATTN_HEADS)
        s = [_dot_nt(q[:, i * dh:(i + 1) * dh], k16[i]) for i in hs]
        pr = [jnp.exp2(s[i] - jnp.max(s[i], axis=-1, keepdims=True)) for i in hs]
        pr = [(pr[i] / jnp.sum(pr[i], axis=-1, keepdims=True)).astype(BF16) for i in hs]
        return jnp.concatenate([_dot(pr[i], v16[i]) for i in hs], axis=1).astype(BF16)

    starts = list(range(0, x_ref.shape[1], sub))
    xq, att = {}, {}
    for step in range(len(starts) + 2):
        if step < len(starts):
            xq[starts[step]] = project(starts[step])
        if 0 <= step - 1 < len(starts):
            r0 = starts[step - 1]
            att[r0] = attend(xq[r0][1])
        if 0 <= step - 2 < len(starts):
            r0 = starts[step - 2]
            o_ref[0, r0:r0 + sub, :] = xq.pop(r0)[0] + _dot(att.pop(r0), wo_ref[...])


def _xattn(x, ya, yb, w_mix, g, w_q, mem, mem_g, w_kv, w_o, layer, *, tm=1024, sub=1024):
    B, S, D = x.shape
    M = mem.shape[1]
    na, nb = ya.shape[-1], yb.shape[-1]
    assert na == nb
    tm = min(tm, S)
    of_layer = lambda shape, blk=0: pl.BlockSpec((None,) + shape, lambda b, i: (layer, blk, 0),
                                                 pipeline_mode=pl.Buffered(1))
    return pl.pallas_call(
        functools.partial(_xattn_kernel, sub=min(sub, tm)),
        grid=(B, S // tm),
        in_specs=[
            pl.BlockSpec((1, tm, D), lambda b, i: (b, i, 0)),
            pl.BlockSpec((1, tm, na), lambda b, i: (b, i, 0)),
            pl.BlockSpec((1, tm, nb), lambda b, i: (b, i, 0)),
            of_layer((na, D), 0),
            of_layer((nb, D), 1),
            pl.BlockSpec((1, D), lambda b, i: (0, 0)),
            of_layer((D, D)),
            pl.BlockSpec((1, M, D), lambda b, i: (b, 0, 0)),
            pl.BlockSpec((1, D), lambda b, i: (0, 0)),
            of_layer((D, 2 * D)),
            of_layer((D, D)),
        ],
        out_specs=pl.BlockSpec((1, tm, D), lambda b, i: (b, i, 0)),
        out_shape=jax.ShapeDtypeStruct((B, S, D), F32),
        scratch_shapes=[pltpu.VMEM((M, 2 * D), BF16)],
        compiler_params=_params("parallel", "arbitrary"),
        name="xattn",
    )(x, ya, yb, w_mix, w_mix, g.reshape(1, D), w_q, mem, mem_g.reshape(1, D), w_kv, w_o)


def kernel(x, mem, rel_bias, final_norm_g, ffn1_norm_g, ffn1_w_in, ffn1_w_out, mix_norm_g, w_mix_in, w_mix_out, rwkv_mu, rwkv_w0, rwkv_w_up, rwkv_a0, rwkv_a_up, rwkv_g_up, rwkv_k_k, rwkv_k_a, rwkv_r_k, rwkv_ln_g, rwkv_ln_b, xattn_norm_g, mem_norm_g, xattn_w_q, xattn_w_kv, xattn_w_o, ffn2_norm_g, ffn2_w_in, ffn2_w_out):
    B, S, D = x.shape
    M = mem.shape[1]
    depth = ffn1_w_in.shape[0]
    T = B * S
    bias_strip, (ffn1_in16, ffn1_out16) = _rel_bias_strip(rel_bias, S // MOBA_BLOCK,
                                                         side=[(ffn1_w_in, 0), (ffn1_w_out, 0)])
    x = x.reshape(T, D)
    for l in range(depth):
        later = (w_mix_in, xattn_w_kv, w_mix_out, xattn_w_q, xattn_w_o, ffn2_w_in, ffn2_w_out)
        x, (mix_in16, kv16, mix_out16, q16, o16, ffn2_in16, ffn2_out16) = _ffn(
            x, ffn1_norm_g[l], ffn1_in16, ffn1_out16, 0, side=[(w, l) for w in later])
        p_rwkv, qkv = _norm_proj(x, mix_norm_g[l], mix_in16, 0, (RWKV_PROJ, MOBA_PROJ),
                                 shift_mu=rwkv_mu[l], seq_len=S)
        y_rwkv, y_moba = _mixers(p_rwkv.reshape(B, S, RWKV_PROJ), qkv.reshape(B, S, MOBA_PROJ), bias_strip,
                                 rwkv_w0[l], rwkv_w_up[l], rwkv_a0[l], rwkv_a_up[l], rwkv_g_up[l],
                                 rwkv_k_k[l], rwkv_k_a[l], rwkv_r_k[l], rwkv_ln_g[l], rwkv_ln_b[l])
        x = _xattn(x.reshape(B, S, D), y_rwkv, y_moba, mix_out16, xattn_norm_g[l],
                   q16, mem, mem_norm_g[l], kv16, o16, 0).reshape(T, D)
        last = l == depth - 1
        x, next_ffn1 = _ffn(x, ffn2_norm_g[l], ffn2_in16, ffn2_out16, 0, final_norm_g if last else None,
                            side=[] if last else [(ffn1_w_in, l + 1), (ffn1_w_out, l + 1)])
        if not last:
            ffn1_in16, ffn1_out16 = next_ffn1
    return x.reshape(B, S, D)
```

```python
import functools
import math

import jax
import jax.numpy as jnp
from jax import lax
from jax.experimental import pallas as pl
from jax.experimental.pallas import tpu as pltpu

F32 = jnp.float32
BF16 = jnp.bfloat16

HEAD_DIM = 64
RWKV_WIDTH = 512
MOBA_WIDTH = 512
MOBA_HEADS = MOBA_WIDTH // HEAD_DIM
DECAY_LORA = 64
ICLR_LORA = 64
GATE_LORA = 128
RWKV_PROJ = 3 * RWKV_WIDTH + DECAY_LORA + ICLR_LORA + GATE_LORA
MOBA_PROJ = 3 * MOBA_WIDTH
LNX_EPS = 64e-5
MOBA_BLOCK = 256
MOBA_TOPK = 3
REL_BUCKETS = 32
REL_MAX_DISTANCE = 1024
XATTN_HEADS = 4
FFN_RES_WEIGHT = 0.5
NORM_EPS = 1e-6
LOG2E = math.log2(math.e)

LANES = 128
MXU_DIM = 256
HEADS_PER_TILE = LANES // HEAD_DIM
HEAD_SHIFT = HEAD_DIM.bit_length() - 1
RWKV_CHUNK = 64
CHUNK_SHIFT = RWKV_CHUNK.bit_length() - 1
RWKV_ROWS_IN_STEP = 2
MIXER_STAGE_RATIO = (3, 2)
SPLIT_LORA = (1, 1, 1)
SPLIT_SUM = (1, 1, 1)
SPLIT_CUMSUM = (1, 2, 2)
VMEM_LIMIT = 56 * 1024 * 1024


def _rms(x, g):
    ms = jnp.mean(x * x, axis=-1, keepdims=True)
    return x * lax.rsqrt(ms + NORM_EPS) * g


def _sigmoid(x):
    return 0.5 * jnp.tanh(0.5 * x) + 0.5


def _dot(a, b, precision=None):
    return jnp.dot(a, b, precision=precision, preferred_element_type=F32)


def _dot_nt(a, b, precision=None):
    return lax.dot_general(a, b, (((1,), (1,)), ((), ())), precision=precision,
                           preferred_element_type=F32)


def _dot_tn(a, b, precision=None):
    return lax.dot_general(a, b, (((0,), (0,)), ((), ())), precision=precision,
                           preferred_element_type=F32)


def _bf16_terms(x, n):
    terms = []
    for i in range(n):
        t = x.astype(BF16)
        terms.append(t)
        if i + 1 < n:
            x = x - t.astype(F32)
    return terms


def _mm(dot, a, b, split):
    na, nb, order = split
    at, bt = _bf16_terms(a, na), _bf16_terms(b, nb)
    acc = None
    for i in range(na):
        for j in range(nb):
            if i + j < order:
                t = dot(at[i], bt[j])
                acc = t if acc is None else acc + t
    return acc


def _params(*semantics):
    return pltpu.CompilerParams(dimension_semantics=semantics, vmem_limit_bytes=VMEM_LIMIT)


def _ffn_kernel(x_ref, g_ref, wi_ref, wo_ref, *rest, final_norm, sub, n_side):
    rest = list(rest)
    fg_ref = rest.pop(0) if final_norm else None
    side_in, o_ref, side_out = rest[:n_side], rest[n_side], rest[n_side + 1:]
    F = wo_ref.shape[0]
    for r0 in range(0, x_ref.shape[0], sub):
        x = x_ref[r0:r0 + sub, :]
        h = _rms(x, g_ref[...]).astype(BF16)
        gate = _dot(h, wi_ref[:, :F])
        up = _dot(h, wi_ref[:, F:])
        act = (gate * _sigmoid(gate) * up).astype(BF16)
        y = x + FFN_RES_WEIGHT * _dot(act, wo_ref[...])
        if final_norm:
            y = _rms(y, fg_ref[...])
        o_ref[r0:r0 + sub, :] = y
    for w_ref, w16_ref in zip(side_in, side_out):
        w16_ref[...] = w_ref[...].astype(BF16)


def _side_cast_parts(side, steps):
    args, in_specs, out_specs, out_shape = [], [], [], []
    for w, w_layer in side:
        n_layers, rows, cols = w.shape
        assert rows % steps == 0
        slab = (rows // steps, cols)
        in_specs.append(pl.BlockSpec((None, None) + slab, lambda i, w_layer=w_layer: (w_layer, i, 0, 0)))
        args.append(w.reshape((n_layers, steps) + slab))
        out_specs.append(pl.BlockSpec((None,) + slab, lambda i: (i, 0, 0)))
        out_shape.append(jax.ShapeDtypeStruct((steps,) + slab, BF16))
    return args, in_specs, out_specs, out_shape


def _ffn(x, g, w_in, w_out, layer, final_g=None, side=(), *, tm=1024, sub=256):
    T, D = x.shape
    F = w_out.shape[1]
    tm = min(tm, T)
    steps = T // tm
    resident = lambda shape: pl.BlockSpec(shape, lambda i: (0, 0), pipeline_mode=pl.Buffered(1))
    of_layer = lambda shape: pl.BlockSpec((None,) + shape, lambda i: (layer, 0, 0),
                                          pipeline_mode=pl.Buffered(1))
    in_specs = [
        pl.BlockSpec((tm, D), lambda i: (i, 0)),
        resident((1, D)),
        of_layer((D, 2 * F)),
        of_layer((F, D)),
    ]
    args = [x, g.reshape(1, D), w_in, w_out]
    if final_g is not None:
        in_specs.append(resident((1, D)))
        args.append(final_g.reshape(1, D))
    side_args, side_in, side_out, side_shape = _side_cast_parts(side, steps)
    y, *side16 = pl.pallas_call(
        functools.partial(_ffn_kernel, final_norm=final_g is not None, sub=min(sub, tm), n_side=len(side)),
        grid=(steps,),
        in_specs=in_specs + side_in,
        out_specs=[pl.BlockSpec((tm, D), lambda i: (i, 0))] + side_out,
        out_shape=[jax.ShapeDtypeStruct((T, D), F32)] + side_shape,
        compiler_params=_params("parallel"),
        name="ffn",
    )(*args, *side_args)
    return y, [w16.reshape((1,) + w.shape[1:]) for w16, (w, _) in zip(side16, side)]


def _norm_proj_kernel(x_ref, g_ref, w_ref, *rest, splits, tiles_per_seq, sub):
    if tiles_per_seq:
        mu_ref, *o_refs, prev_ref = rest

        @pl.when(pl.program_id(0) % tiles_per_seq == 0)
        def _():
            prev_ref[...] = jnp.zeros_like(prev_ref)

        before = prev_ref[...]
    else:
        o_refs = rest
    row = lax.broadcasted_iota(jnp.int32, (sub, 1), 0)
    for r0 in range(0, x_ref.shape[0], sub):
        h = _rms(x_ref[r0:r0 + sub, :], g_ref[...]).astype(BF16)
        off = 0
        for idx, (o_ref, n) in enumerate(zip(o_refs, splits)):
            y = _dot(h, w_ref[:, off:off + n])
            if tiles_per_seq and idx == 0:
                shifted = jnp.where(row == 0, before, pltpu.roll(y, 1, axis=0))
                before = y[sub - 1:sub, :]
                y = y + (shifted - y) * mu_ref[...]
            o_ref[r0:r0 + sub, :] = y
            off += n
    if tiles_per_seq:
        prev_ref[...] = before


def _norm_proj(x, g, w, layer, splits, *, shift_mu=None, seq_len=None, tm=1024, sub=256):
    T, D = x.shape
    N = w.shape[2]
    tm = min(tm, T)
    assert sum(splits) == N
    in_specs = [
        pl.BlockSpec((tm, D), lambda i: (i, 0)),
        pl.BlockSpec((1, D), lambda i: (0, 0)),
        pl.BlockSpec((None, D, N), lambda i: (layer, 0, 0)),
    ]
    args = [x, g.reshape(1, D), w]
    scratch, tiles_per_seq = [], 0
    if shift_mu is not None:
        assert seq_len % tm == 0
        tiles_per_seq = seq_len // tm
        in_specs.append(pl.BlockSpec((1, splits[0]), lambda i: (0, 0)))
        args.append(shift_mu.reshape(1, splits[0]))
        scratch = [pltpu.VMEM((1, splits[0]), F32)]
    return pl.pallas_call(
        functools.partial(_norm_proj_kernel, splits=splits, tiles_per_seq=tiles_per_seq,
                          sub=min(sub, tm)),
        grid=(T // tm,),
        in_specs=in_specs,
        out_specs=[pl.BlockSpec((tm, n), lambda i: (i, 0)) for n in splits],
        out_shape=[jax.ShapeDtypeStruct((T, n), F32) for n in splits],
        scratch_shapes=scratch,
        compiler_params=_params("arbitrary" if tiles_per_seq else "parallel"),
        name="norm_proj",
    )(*args)


def _rwkv_stages(p_ref, w0_ref, a0_ref, wwa_ref, gup_ref, kk_ref, ka_ref, rk_ref,
                 lng_ref, lnb_ref, o_ref, state_ref):
    C = RWKV_CHUNK
    W = RWKV_WIDTH
    NB, T, P = p_ref.shape
    n_pairs = W // LANES
    n_chunks = T // C

    @pl.when(pl.program_id(1) == 0)
    def _():
        state_ref[...] = jnp.zeros_like(state_ref)

    lane = lax.broadcasted_iota(jnp.int32, (1, LANES), 1)
    first_head = lane < HEAD_DIM
    ri = lax.broadcasted_iota(jnp.int32, (LANES, LANES), 0)
    ci = lax.broadcasted_iota(jnp.int32, (LANES, LANES), 1)
    same_head = (ri >> HEAD_SHIFT) == (ci >> HEAD_SHIFT)
    ri2 = lax.broadcasted_iota(jnp.int32, (MXU_DIM, MXU_DIM), 0)
    ci2 = lax.broadcasted_iota(jnp.int32, (MXU_DIM, MXU_DIM), 1)
    head_ones = ((ri2 >> HEAD_SHIFT) == (ci2 >> HEAD_SHIFT)).astype(F32)
    in_chunk_prefix = ((ci2 <= ri2) & ((ci2 >> CHUNK_SHIFT) == (ri2 >> CHUNK_SHIFT))).astype(F32)
    tok = lax.broadcasted_iota(jnp.int32, (C, LANES), 0)
    col_tok = lax.broadcasted_iota(jnp.int32, (C, LANES), 1) & (C - 1)
    strict = col_tok < tok
    incl = col_tok <= tok
    incl2 = jnp.concatenate([incl, incl], axis=1)
    eye = (col_tok == tok).astype(F32)
    lanes = [slice(i * LANES, (i + 1) * LANES) for i in range(n_pairs)]
    rows_of = lambda c: slice(c * C, (c + 1) * C)
    units = [(c, i) for c in range(n_chunks) for i in range(n_pairs)]
    c16 = lambda t: t.astype(BF16)

    def head_sum(t):
        return jnp.concatenate(
            [_mm(_dot, t[:, i * MXU_DIM:(i + 1) * MXU_DIM], head_ones, SPLIT_SUM)
             for i in range(W // MXU_DIM)],
            axis=1)

    def stack(t):
        zero = jnp.zeros_like(t)
        return jnp.concatenate([jnp.where(first_head, t, zero), jnp.where(first_head, zero, t)], axis=0)

    def tokenwise(nb, out):
        p = p_ref[nb]
        r, k, v = p[:, 0:W], p[:, W:2 * W], p[:, 2 * W:3 * W]
        lora = p[:, 3 * W:3 * W + LANES]
        g_lo = p[:, 3 * W + LANES:3 * W + 2 * LANES]
        z = jnp.where(first_head, jnp.tanh(lora), lora)
        wa = _mm(_dot, z, wwa_ref[...], SPLIT_LORA)
        g = _mm(_dot, _sigmoid(g_lo), gup_ref[...], SPLIT_LORA)
        yield
        logw = -math.exp(-0.5) * _sigmoid(w0_ref[...] + wa[:, :W])
        a = _sigmoid(a0_ref[...] + wa[:, W:])
        kk = k * kk_ref[...]
        k2 = k * (1.0 + (a - 1.0) * ka_ref[...])
        yield
        kkn = kk * lax.rsqrt(jnp.maximum(head_sum(kk * kk), 1e-24))
        b = kkn * a
        yield
        bonus = head_sum(r * k2 * rk_ref[...]) * v
        logp = jnp.concatenate(
            [_mm(_dot, in_chunk_prefix, logw[i * MXU_DIM:(i + 1) * MXU_DIM], SPLIT_CUMSUM)
             for i in range(T // MXU_DIM)], axis=0)
        yield
        inv = jnp.exp(-logp)
        out.update(a16=c16(-kkn * jnp.exp(logp - logw)), r16=c16(r * jnp.exp(logp)), v16=c16(v))
        yield
        out.update(b16=c16(b * inv), k16=c16(k2 * inv), b=b, k2=k2, logp=logp, bonus=bonus, g=g)

    def chunkwise(tw, out):
        a16, r16, b16, k16, v16 = tw["a16"], tw["r16"], tw["b16"], tw["k16"], tw["v16"]
        ar = {(c, i): jnp.concatenate([a16[rows_of(c), lanes[i]], r16[rows_of(c), lanes[i]]], axis=0)
              for c, i in units}
        bk = {(c, i): jnp.concatenate([stack(b16[rows_of(c), lanes[i]]), stack(k16[rows_of(c), lanes[i]])],
                                      axis=0) for c, i in units}
        v_s = {(c, i): stack(v16[rows_of(c), lanes[i]]) for c, i in units}
        yield
        gram = {u_: _dot_nt(ar[u_], bk[u_]) for u_ in units}
        yield
        l_ab = {u_: jnp.where(strict, gram[u_][:C, :LANES], 0.0) for u_ in units}
        a_ak = {u_: c16(jnp.where(strict, gram[u_][:C, LANES:], 0.0)) for u_ in units}
        a_r = {u_: c16(jnp.where(incl2, gram[u_][C:], 0.0)) for u_ in units}
        t_inv = {u_: eye + l_ab[u_] for u_ in units}
        lp16 = {u_: c16(l_ab[u_]) for u_ in units}
        yield
        l_pow = {u_: _dot(lp16[u_], stack(lp16[u_])) for u_ in units}
        av = {u_: _dot(a_ak[u_], v_s[u_]) for u_ in units}
        yield
        levels = int(math.log2(C)) - 1
        for level in range(levels - 1):
            lp16 = {u_: c16(l_pow[u_]) for u_ in units}
            both = {u_: _dot(jnp.concatenate([c16(t_inv[u_]), lp16[u_]], axis=0), stack(lp16[u_]))
                    for u_ in units}
            yield
            t_inv = {u_: t_inv[u_] + both[u_][:C] for u_ in units}
            l_pow = {u_: both[u_][C:] for u_ in units}
        last = {u_: _dot(c16(t_inv[u_]), stack(c16(l_pow[u_]))) for u_ in units}
        yield
        out.update(ar=ar, v_s=v_s, a_r=a_r, av=av, t16={u_: c16(t_inv[u_] + last[u_]) for u_ in units})

    def recurrent(nb, tw, cw, out):
        b, k2, logp, v16 = tw["b"], tw["k2"], tw["logp"], tw["v16"]
        ar, v_s, a_r, av, t16 = cw["ar"], cw["v_s"], cw["a_r"], cw["av"], cw["t16"]
        states = [state_ref[nb, i] for i in range(n_pairs)]
        y_chunks = []
        for c in range(n_chunks):
            logp_end = logp[(c + 1) * C - 1:(c + 1) * C, :]
            to_end = jnp.exp(logp_end - logp[rows_of(c)])
            bk_e = [c16(jnp.concatenate([b[rows_of(c), sl] * to_end[:, sl], k2[rows_of(c), sl] * to_end[:, sl]],
                                        axis=0)) for sl in lanes]
            from_state = [_dot_nt(ar[c, i], c16(states[i])) for i in range(n_pairs)]
            yield
            u16 = [c16(_dot(t16[c, i], stack(c16(from_state[i][:C] + av[c, i])))) for i in range(n_pairs)]
            yield
            outer = [_dot_tn(jnp.concatenate([u16[i], v16[rows_of(c), lanes[i]]], axis=0), bk_e[i])
                     for i in range(n_pairs)]
            y_chunks.append(jnp.concatenate(
                [from_state[i][C:] + _dot(a_r[c, i], jnp.concatenate([stack(u16[i]), v_s[c, i]], axis=0))
                 for i in range(n_pairs)], axis=1))
            yield
            decay_end = jnp.exp(logp_end)
            states = [states[i] * decay_end[:, lanes[i]] + jnp.where(same_head, outer[i], 0.0)
                      for i in range(n_pairs)]
        for i in range(n_pairs):
            state_ref[nb, i] = states[i]
        out.update(y=jnp.concatenate(y_chunks, axis=0))

    def finish(nb, tw, rc):
        y = rc["y"]
        mean = head_sum(y) * (1.0 / HEAD_DIM)
        yield
        yc = y - mean
        var = head_sum(yc * yc) * (1.0 / HEAD_DIM)
        yield
        y = yc * lax.rsqrt(var + LNX_EPS) * lng_ref[...] + lnb_ref[...]
        o_ref[nb] = (y + tw["bonus"]) * tw["g"]

    def in_turn(gens):
        while gens:
            gens = [gen for gen in gens if next(gen, gens) is not gens]
            yield

    tw = [dict() for _ in range(NB)]
    cw = [dict() for _ in range(NB)]
    rc = [dict() for _ in range(NB)]
    groups = [range(g0, min(g0 + RWKV_ROWS_IN_STEP, NB)) for g0 in range(0, NB, RWKV_ROWS_IN_STEP)]
    phases = [
        lambda rows: in_turn([tokenwise(nb, tw[nb]) for nb in rows]),
        lambda rows: in_turn([chunkwise(tw[nb], cw[nb]) for nb in rows]),
        lambda rows: in_turn([recurrent(nb, tw[nb], cw[nb], rc[nb]) for nb in rows]),
        lambda rows: in_turn([finish(nb, tw[nb], rc[nb]) for nb in rows]),
    ]
    for wave in range(len(groups) + len(phases) - 1):
        yield from in_turn([phases[wave - g](rows) for g, rows in enumerate(groups)
                            if 0 <= wave - g < len(phases)])


def _rwkv_kernel(*refs):
    for _ in _rwkv_stages(*refs):
        pass


def _rwkv_call_parts(p, w0, w_up, a0, a_up, g_up, k_k, k_a, r_k, ln_g, ln_b, ts, nb):
    B, S, P = p.shape
    W = RWKV_WIDTH
    wwa = jnp.zeros((LANES, 2 * W), F32)
    wwa = wwa.at[:DECAY_LORA, :W].set(w_up).at[DECAY_LORA:, W:].set(a_up)
    vec = lambda t: t.reshape(1, -1)
    row_spec = lambda n: pl.BlockSpec((1, n), lambda b, s: (0, 0))
    assert ts % MXU_DIM == 0 and S % ts == 0 and B % nb == 0
    args = (p, vec(w0), vec(a0), wwa, g_up, vec(k_k), vec(k_a), vec(r_k), vec(ln_g), vec(ln_b))
    in_specs = [
        pl.BlockSpec((nb, ts, P), lambda b, s: (b, s, 0)),
        row_spec(W), row_spec(W),
        pl.BlockSpec((LANES, 2 * W), lambda b, s: (0, 0)),
        pl.BlockSpec((GATE_LORA, W), lambda b, s: (0, 0)),
        row_spec(W), row_spec(W), row_spec(W), row_spec(W), row_spec(W),
    ]
    out_spec = pl.BlockSpec((nb, ts, W), lambda b, s: (b, s, 0))
    out_shape = jax.ShapeDtypeStruct((B, S, W), F32)
    scratch = [pltpu.VMEM((nb, W // LANES, LANES, LANES), F32)]
    return (B // nb, S // ts), args, in_specs, out_spec, out_shape, scratch


def _rwkv(p, *weights, ts=256, nb=2):
    B, S, _ = p.shape
    grid, args, in_specs, out_spec, out_shape, scratch = _rwkv_call_parts(p, *weights, min(ts, S), min(nb, B))
    return pl.pallas_call(
        _rwkv_kernel,
        grid=grid,
        in_specs=in_specs,
        out_specs=out_spec,
        out_shape=out_shape,
        scratch_shapes=scratch,
        compiler_params=_params("parallel", "arbitrary"),
        name="rwkv7",
    )(*args)


def _rel_bias_kernel(tab_ref, *refs, n_blocks, n_side):
    side_in, o_ref, side_out = refs[:n_side], refs[n_side], refs[n_side + 1:]
    blk = MOBA_BLOCK
    max_exact = REL_BUCKETS // 2

    def bucket_of(n):
        large = max_exact + math.floor(math.log(max(n, 1) / max_exact)
                                       / math.log(REL_MAX_DISTANCE / max_exact) * (REL_BUCKETS - max_exact))
        return n if n < max_exact else min(large, REL_BUCKETS - 1)

    def block(cb):
        kc = lax.broadcasted_iota(jnp.int32, (blk, blk), 0) + cb * blk
        qi = lax.broadcasted_iota(jnp.int32, (blk, blk), 1)
        dist = qi + (n_blocks - 1) * blk - kc
        n = jnp.maximum(dist, 0)
        nf = jnp.maximum(n, 1).astype(F32)
        large = max_exact + (jnp.log(nf / max_exact) / math.log(REL_MAX_DISTANCE / max_exact)
                             * (REL_BUCKETS - max_exact)).astype(jnp.int32)
        large = jnp.minimum(large, REL_BUCKETS - 1)
        bucket = jnp.where(n < max_exact, n, large)
        n_lo = max((n_blocks - 1 - cb) * blk - (blk - 1), 0)
        n_hi = max((n_blocks - 1 - cb) * blk + (blk - 1), 0)
        b_lo = max(bucket_of(n_lo) - 1, 0)
        b_hi = min(bucket_of(n_hi) + 1, REL_BUCKETS - 1)
        for h in range(MOBA_HEADS):
            tile = jnp.zeros((blk, blk), F32)
            for c in range(b_lo, b_hi + 1):
                tile = jnp.where(bucket == c, tab_ref[h, c], tile)
            o_ref[h] = jnp.where(dist < 0, -jnp.inf, tile * LOG2E)

    for cb in range(n_blocks):
        pl.when(pl.program_id(0) == cb)(functools.partial(block, cb))
    for w_ref, w16_ref in zip(side_in, side_out):
        w16_ref[...] = w_ref[...].astype(BF16)


def _rel_bias_strip(rel_bias, n_blocks, side=()):
    blk = MOBA_BLOCK
    side_args, side_in, side_out, side_shape = _side_cast_parts(side, n_blocks)
    strip, *side16 = pl.pallas_call(
        functools.partial(_rel_bias_kernel, n_blocks=n_blocks, n_side=len(side)),
        grid=(n_blocks,),
        in_specs=[pl.BlockSpec(memory_space=pltpu.SMEM)] + side_in,
        out_specs=[pl.BlockSpec((MOBA_HEADS, blk, blk), lambda c: (0, c, 0))] + side_out,
        out_shape=[jax.ShapeDtypeStruct((MOBA_HEADS, n_blocks * blk, blk), F32)] + side_shape,
        compiler_params=_params("parallel"),
        name="rel_bias_strip",
    )(rel_bias, *side_args)
    return strip, [w16.reshape((1,) + w.shape[1:]) for w16, (w, _) in zip(side16, side)]


def _moba_stages(q_ref, k_ref, v_ref, bias_ref, o_ref, *, n_blocks):
    blk = MOBA_BLOCK
    lane = lax.broadcasted_iota(jnp.int32, (1, LANES), 1)
    vrow = lax.broadcasted_iota(jnp.int32, (LANES, 1), 0)
    q = q_ref[0] * (HEAD_DIM ** -0.5 * LOG2E)
    k = k_ref[0]
    k16 = k.astype(BF16)
    vt = v_ref[0].T
    vt16 = [jnp.where((vrow >> HEAD_SHIFT) == e, vt, 1.0).astype(BF16) for e in range(HEADS_PER_TILE)]
    kmean = jnp.mean(k.reshape(n_blocks, blk, LANES), axis=1)

    q_hi, q_lo = _bf16_terms(q, 2)
    m_hi, m_lo = _bf16_terms(kmean, 2)
    gates, q16 = [], []
    for e in range(HEADS_PER_TILE):
        head_lanes = (lane >> HEAD_SHIFT) == e
        zero = jnp.zeros_like(q_hi)
        qe_hi, qe_lo = jnp.where(head_lanes, q_hi, zero), jnp.where(head_lanes, q_lo, zero)
        gates.append(_dot_nt(m_hi, qe_hi) + _dot_nt(m_hi, qe_lo) + _dot_nt(m_lo, qe_hi))
        q16.append(qe_hi)

    def scores(qb, e):
        return _dot_nt(k16[:(qb + 1) * blk], q16[e][qb * blk:(qb + 1) * blk])

    def weights(qb, e, s_t):
        rows = slice(qb * blk, (qb + 1) * blk)
        bias_lo = (n_blocks - 1 - qb) * blk
        g = [gates[e][j:j + 1, rows] for j in range(qb)]
        tiles = []
        for j in range(qb + 1):
            t = s_t[j * blk:(j + 1) * blk] + bias_ref[e, bias_lo + j * blk:bias_lo + (j + 1) * blk, :]
            if j < qb:
                rank = jnp.zeros((1, blk), jnp.int32)
                for jj in range(qb):
                    if jj != j:
                        ahead = (g[jj] >= g[j]) if jj < j else (g[jj] > g[j])
                        rank = rank + jnp.where(ahead, 1, 0)
                t = jnp.where(rank < MOBA_TOPK, t, -jnp.inf)
            tiles.append(t)
        m = tiles[0].max(axis=0, keepdims=True)
        for t in tiles[1:]:
            m = jnp.maximum(m, t.max(axis=0, keepdims=True))
        return jnp.concatenate([jnp.exp2(t - m).astype(BF16) for t in tiles], axis=0)

    def attend(qb, e, p16):
        pv = _dot(vt16[e][:, :(qb + 1) * blk], p16)
        den_row = (1 - e) * HEAD_DIM
        return pv / pv[den_row:den_row + 1]

    units = [(qb, e) for qb in range(n_blocks) for e in range(HEADS_PER_TILE)]
    s_t, p16, out_t = {}, {}, {}
    for step in range(len(units) + 2):
        if step < len(units):
            s_t[units[step]] = scores(*units[step])
        if 0 <= step - 1 < len(units):
            u = units[step - 1]
            p16[u] = weights(*u, s_t.pop(u))
        if 0 <= step - 2 < len(units):
            u = units[step - 2]
            out_t[u] = attend(*u, p16.pop(u))
            qb, e = u
            if e == HEADS_PER_TILE - 1:
                o_ref[0, qb * blk:(qb + 1) * blk, :] = jnp.where(
                    vrow < HEAD_DIM, out_t.pop((qb, 0)), out_t.pop((qb, 1))).T
        yield


def _moba_kernel(*refs, n_blocks):
    for _ in _moba_stages(*refs, n_blocks=n_blocks):
        pass


def _moba(qkv, bias_strip):
    B, S, _ = qkv.shape
    blk = MOBA_BLOCK
    n_blocks = S // blk
    n_pairs = MOBA_WIDTH // LANES
    return pl.pallas_call(
        functools.partial(_moba_kernel, n_blocks=n_blocks),
        grid=(n_pairs, B),
        in_specs=[
            pl.BlockSpec((1, S, LANES), lambda p, b: (b, 0, p)),
            pl.BlockSpec((1, S, LANES), lambda p, b: (b, 0, n_pairs + p)),
            pl.BlockSpec((1, S, LANES), lambda p, b: (b, 0, 2 * n_pairs + p)),
            pl.BlockSpec((HEADS_PER_TILE, S, blk), lambda p, b: (p, 0, 0)),
        ],
        out_specs=pl.BlockSpec((1, S, LANES), lambda p, b: (b, 0, p)),
        out_shape=jax.ShapeDtypeStruct((B, S, MOBA_WIDTH), F32),
        compiler_params=_params("parallel", "parallel"),
        name="moba",
    )(qkv, qkv, qkv, bias_strip)


N_RWKV_INPUTS = 10
N_MOBA_INPUTS = 4


def _mixers_kernel(*refs, n_blocks):
    rwkv_in, refs = refs[:N_RWKV_INPUTS], refs[N_RWKV_INPUTS:]
    moba_in, (y_rwkv_ref, y_moba_ref, state_ref) = refs[:N_MOBA_INPUTS], refs[N_MOBA_INPUTS:]
    jobs = [(_rwkv_stages(*rwkv_in, y_rwkv_ref, state_ref), MIXER_STAGE_RATIO[0]),
            (_moba_stages(*moba_in, y_moba_ref, n_blocks=n_blocks), MIXER_STAGE_RATIO[1])]
    while jobs:
        jobs = [(gen, n) for gen, n in jobs if all(next(gen, jobs) is not jobs for _ in range(n))]


def _mixers(p, qkv, bias_strip, *rwkv_weights, ts=256, nb=2):
    B, S, _ = qkv.shape
    blk = MOBA_BLOCK
    n_blocks = S // blk
    n_pairs = MOBA_WIDTH // LANES
    ts, nb = min(ts, S), min(nb, B)
    grid, args, in_specs, out_spec, out_shape, scratch = _rwkv_call_parts(p, *rwkv_weights, ts, nb)
    if grid[0] * grid[1] != n_pairs * B:
        return _rwkv(p, *rwkv_weights, ts=ts, nb=nb), _moba(qkv, bias_strip)
    item = lambda g, s: g * grid[1] + s
    row = lambda g, s: item(g, s) % B
    pair = lambda g, s: item(g, s) // B
    in_specs = in_specs + [
        pl.BlockSpec((1, S, LANES), lambda g, s: (row(g, s), 0, pair(g, s))),
        pl.BlockSpec((1, S, LANES), lambda g, s: (row(g, s), 0, n_pairs + pair(g, s))),
        pl.BlockSpec((1, S, LANES), lambda g, s: (row(g, s), 0, 2 * n_pairs + pair(g, s))),
        pl.BlockSpec((HEADS_PER_TILE, S, blk), lambda g, s: (pair(g, s), 0, 0)),
    ]
    assert len(args) == N_RWKV_INPUTS
    return pl.pallas_call(
        functools.partial(_mixers_kernel, n_blocks=n_blocks),
        grid=grid,
        in_specs=in_specs,
        out_specs=[out_spec, pl.BlockSpec((1, S, LANES), lambda g, s: (row(g, s), 0, pair(g, s)))],
        out_shape=[out_shape, jax.ShapeDtypeStruct((B, S, MOBA_WIDTH), F32)],
        scratch_shapes=scratch,
        compiler_params=_params("parallel", "arbitrary"),
        name="mixers",
    )(*args, qkv, qkv, qkv, bias_strip)


def _xattn_kernel(x_ref, ya_ref, yb_ref, wa_ref, wb_ref, g_ref, wq_ref, mem_ref, mg_ref, wkv_ref, wo_ref,
                  o_ref, kv_ref, *, sub):
    D = x_ref.shape[-1]
    dh = D // XATTN_HEADS

    @pl.when(pl.program_id(1) == 0)
    def _():
        kv_ref[...] = _dot(_rms(mem_ref[0], mg_ref[...]).astype(BF16), wkv_ref[...]).astype(BF16)

    k16 = [kv_ref[:, i * dh:(i + 1) * dh] for i in range(XATTN_HEADS)]
    v16 = [kv_ref[:, D + i * dh:D + (i + 1) * dh] for i in range(XATTN_HEADS)]

    def project(r0):
        rows = slice(r0, r0 + sub)
        x = (x_ref[0, rows, :] + _dot(ya_ref[0, rows, :].astype(BF16), wa_ref[...])
             + _dot(yb_ref[0, rows, :].astype(BF16), wb_ref[...]))
        h = _rms(x, g_ref[...]).astype(BF16)
        q = (_dot(h, wq_ref[...]) * (dh ** -0.5 * LOG2E)).astype(BF16)
        return x, q

    def attend(q):
        hs = range(XATTN_HEADS)
        s = [_dot_nt(q[:, i * dh:(i + 1) * dh], k16[i]) for i in hs]
        pr = [jnp.exp2(s[i] - jnp.max(s[i], axis=-1, keepdims=True)) for i in hs]
        pr = [(pr[i] / jnp.sum(pr[i], axis=-1, keepdims=True)).astype(BF16) for i in hs]
        return jnp.concatenate([_dot(pr[i], v16[i]) for i in hs], axis=1).astype(BF16)

    starts = list(range(0, x_ref.shape[1], sub))
    xq, att = {}, {}
    for step in range(len(starts) + 2):
        if step < len(starts):
            xq[starts[step]] = project(starts[step])
        if 0 <= step - 1 < len(starts):
            r0 = starts[step - 1]
            att[r0] = attend(xq[r0][1])
        if 0 <= step - 2 < len(starts):
            r0 = starts[step - 2]
            o_ref[0, r0:r0 + sub, :] = xq.pop(r0)[0] + _dot(att.pop(r0), wo_ref[...])


def _xattn(x, ya, yb, w_mix, g, w_q, mem, mem_g, w_kv, w_o, layer, *, tm=1024, sub=1024):
    B, S, D = x.shape
    M = mem.shape[1]
    na, nb = ya.shape[-1], yb.shape[-1]
    assert na == nb
    tm = min(tm, S)
    of_layer = lambda shape, blk=0: pl.BlockSpec((None,) + shape, lambda b, i: (layer, blk, 0),
                                                 pipeline_mode=pl.Buffered(1))
    return pl.pallas_call(
        functools.partial(_xattn_kernel, sub=min(sub, tm)),
        grid=(B, S // tm),
        in_specs=[
            pl.BlockSpec((1, tm, D), lambda b, i: (b, i, 0)),
            pl.BlockSpec((1, tm, na), lambda b, i: (b, i, 0)),
            pl.BlockSpec((1, tm, nb), lambda b, i: (b, i, 0)),
            of_layer((na, D), 0),
            of_layer((nb, D), 1),
            pl.BlockSpec((1, D), lambda b, i: (0, 0)),
            of_layer((D, D)),
            pl.BlockSpec((1, M, D), lambda b, i: (b, 0, 0)),
            pl.BlockSpec((1, D), lambda b, i: (0, 0)),
            of_layer((D, 2 * D)),
            of_layer((D, D)),
        ],
        out_specs=pl.BlockSpec((1, tm, D), lambda b, i: (b, i, 0)),
        out_shape=jax.ShapeDtypeStruct((B, S, D), F32),
        scratch_shapes=[pltpu.VMEM((M, 2 * D), BF16)],
        compiler_params=_params("parallel", "arbitrary"),
        name="xattn",
    )(x, ya, yb, w_mix, w_mix, g.reshape(1, D), w_q, mem, mem_g.reshape(1, D), w_kv, w_o)


def kernel(x, mem, rel_bias, final_norm_g, ffn1_norm_g, ffn1_w_in, ffn1_w_out, mix_norm_g, w_mix_in, w_mix_out, rwkv_mu, rwkv_w0, rwkv_w_up, rwkv_a0, rwkv_a_up, rwkv_g_up, rwkv_k_k, rwkv_k_a, rwkv_r_k, rwkv_ln_g, rwkv_ln_b, xattn_norm_g, mem_norm_g, xattn_w_q, xattn_w_kv, xattn_w_o, ffn2_norm_g, ffn2_w_in, ffn2_w_out):
    B, S, D = x.shape
    M = mem.shape[1]
    depth = ffn1_w_in.shape[0]
    T = B * S
    bias_strip, (ffn1_in16, ffn1_out16) = _rel_bias_strip(rel_bias, S // MOBA_BLOCK,
                                                         side=[(ffn1_w_in, 0), (ffn1_w_out, 0)])
    x = x.reshape(T, D)
    for l in range(depth):
        later = (w_mix_in, xattn_w_kv, w_mix_out, xattn_w_q, xattn_w_o, ffn2_w_in, ffn2_w_out)
        x, (mix_in16, kv16, mix_out16, q16, o16, ffn2_in16, ffn2_out16) = _ffn(
            x, ffn1_norm_g[l], ffn1_in16, ffn1_out16, 0, side=[(w, l) for w in later])
        p_rwkv, qkv = _norm_proj(x, mix_norm_g[l], mix_in16, 0, (RWKV_PROJ, MOBA_PROJ),
                                 shift_mu=rwkv_mu[l], seq_len=S)
        y_rwkv, y_moba = _mixers(p_rwkv.reshape(B, S, RWKV_PROJ), qkv.reshape(B, S, MOBA_PROJ), bias_strip,
                                 rwkv_w0[l], rwkv_w_up[l], rwkv_a0[l], rwkv_a_up[l], rwkv_g_up[l],
                                 rwkv_k_k[l], rwkv_k_a[l], rwkv_r_k[l], rwkv_ln_g[l], rwkv_ln_b[l])
        x = _xattn(x.reshape(B, S, D), y_rwkv, y_moba, mix_out16, xattn_norm_g[l],
                   q16, mem, mem_norm_g[l], kv16, o16, 0).reshape(T, D)
        last = l == depth - 1
        x, next_ffn1 = _ffn(x, ffn2_norm_g[l], ffn2_in16, ffn2_out16, 0, final_norm_g if last else None,
                            side=[] if last else [(ffn1_w_in, l + 1), (ffn1_w_out, l + 1)])
        if not last:
            ffn1_in16, ffn1_out16 = next_ffn1
    return x.reshape(B, S, D)
```

```python
import functools
import math

import jax
import jax.numpy as jnp
from jax import lax
from jax.experimental import pallas as pl
from jax.experimental.pallas import tpu as pltpu

F32 = jnp.float32
BF16 = jnp.bfloat16

HEAD_DIM = 64
RWKV_WIDTH = 512
MOBA_WIDTH = 512
MOBA_HEADS = MOBA_WIDTH // HEAD_DIM
DECAY_LORA = 64
ICLR_LORA = 64
GATE_LORA = 128
RWKV_PROJ = 3 * RWKV_WIDTH + DECAY_LORA + ICLR_LORA + GATE_LORA
MOBA_PROJ = 3 * MOBA_WIDTH
LNX_EPS = 64e-5
MOBA_BLOCK = 256
MOBA_TOPK = 3
REL_BUCKETS = 32
REL_MAX_DISTANCE = 1024
XATTN_HEADS = 4
FFN_RES_WEIGHT = 0.5
NORM_EPS = 1e-6
LOG2E = math.log2(math.e)

LANES = 128
MXU_DIM = 256
HEADS_PER_TILE = LANES // HEAD_DIM
HEAD_SHIFT = HEAD_DIM.bit_length() - 1
RWKV_CHUNK = 64
CHUNK_SHIFT = RWKV_CHUNK.bit_length() - 1
RWKV_ROWS_IN_STEP = 2
MIXER_STAGE_RATIO = (3, 2)
SPLIT_LORA = (1, 1, 1)
SPLIT_SUM = (1, 1, 1)
SPLIT_CUMSUM = (1, 2, 2)
VMEM_LIMIT = 56 * 1024 * 1024


def _rms(x, g):
    ms = jnp.mean(x * x, axis=-1, keepdims=True)
    return x * lax.rsqrt(ms + NORM_EPS) * g


def _sigmoid(x):
    return 0.5 * jnp.tanh(0.5 * x) + 0.5


def _dot(a, b, precision=None):
    return jnp.dot(a, b, precision=precision, preferred_element_type=F32)


def _dot_nt(a, b, precision=None):
    return lax.dot_general(a, b, (((1,), (1,)), ((), ())), precision=precision,
                           preferred_element_type=F32)


def _dot_tn(a, b, precision=None):
    return lax.dot_general(a, b, (((0,), (0,)), ((), ())), precision=precision,
                           preferred_element_type=F32)


def _bf16_terms(x, n):
    terms = []
    for i in range(n):
        t = x.astype(BF16)
        terms.append(t)
        if i + 1 < n:
            x = x - t.astype(F32)
    return terms


def _mm(dot, a, b, split):
    na, nb, order = split
    at, bt = _bf16_terms(a, na), _bf16_terms(b, nb)
    acc = None
    for i in range(na):
        for j in range(nb):
            if i + j < order:
                t = dot(at[i], bt[j])
                acc = t if acc is None else acc + t
    return acc


def _params(*semantics):
    return pltpu.CompilerParams(dimension_semantics=semantics, vmem_limit_bytes=VMEM_LIMIT)


def _ffn_kernel(x_ref, g_ref, wi_ref, wo_ref, *rest, final_norm, sub, n_side):
    rest = list(rest)
    fg_ref = rest.pop(0) if final_norm else None
    side_in, o_ref, side_out = rest[:n_side], rest[n_side], rest[n_side + 1:]
    F = wo_ref.shape[0]
    for r0 in range(0, x_ref.shape[0], sub):
        x = x_ref[r0:r0 + sub, :]
        h = _rms(x, g_ref[...]).astype(BF16)
        gate = _dot(h, wi_ref[:, :F])
        up = _dot(h, wi_ref[:, F:])
        act = (gate * _sigmoid(gate) * up).astype(BF16)
        y = x + FFN_RES_WEIGHT * _dot(act, wo_ref[...])
        if final_norm:
            y = _rms(y, fg_ref[...])
        o_ref[r0:r0 + sub, :] = y
    for w_ref, w16_ref in zip(side_in, side_out):
        w16_ref[...] = w_ref[...].astype(BF16)


def _side_cast_parts(side, steps):
    args, in_specs, out_specs, out_shape = [], [], [], []
    for w, w_layer in side:
        n_layers, rows, cols = w.shape
        assert rows % steps == 0
        slab = (rows // steps, cols)
        in_specs.append(pl.BlockSpec((None, None) + slab, lambda i, w_layer=w_layer: (w_layer, i, 0, 0)))
        args.append(w.reshape((n_layers, steps) + slab))
        out_specs.append(pl.BlockSpec((None,) + slab, lambda i: (i, 0, 0)))
        out_shape.append(jax.ShapeDtypeStruct((steps,) + slab, BF16))
    return args, in_specs, out_specs, out_shape


def _ffn(x, g, w_in, w_out, layer, final_g=None, side=(), *, tm=1024, sub=256):
    T, D = x.shape
    F = w_out.shape[1]
    tm = min(tm, T)
    steps = T // tm
    resident = lambda shape: pl.BlockSpec(shape, lambda i: (0, 0), pipeline_mode=pl.Buffered(1))
    of_layer = lambda shape: pl.BlockSpec((None,) + shape, lambda i: (layer, 0, 0),
                                          pipeline_mode=pl.Buffered(1))
    in_specs = [
        pl.BlockSpec((tm, D), lambda i: (i, 0)),
        resident((1, D)),
        of_layer((D, 2 * F)),
        of_layer((F, D)),
    ]
    args = [x, g.reshape(1, D), w_in, w_out]
    if final_g is not None:
        in_specs.append(resident((1, D)))
        args.append(final_g.reshape(1, D))
    side_args, side_in, side_out, side_shape = _side_cast_parts(side, steps)
    y, *side16 = pl.pallas_call(
        functools.partial(_ffn_kernel, final_norm=final_g is not None, sub=min(sub, tm), n_side=len(side)),
        grid=(steps,),
        in_specs=in_specs + side_in,
        out_specs=[pl.BlockSpec((tm, D), lambda i: (i, 0))] + side_out,
        out_shape=[jax.ShapeDtypeStruct((T, D), F32)] + side_shape,
        compiler_params=_params("parallel"),
        name="ffn",
    )(*args, *side_args)
    return y, [w16.reshape((1,) + w.shape[1:]) for w16, (w, _) in zip(side16, side)]


def _norm_proj_kernel(x_ref, g_ref, w_ref, *rest, splits, tiles_per_seq, sub):
    if tiles_per_seq:
        mu_ref, *o_refs, prev_ref = rest

        @pl.when(pl.program_id(0) % tiles_per_seq == 0)
        def _():
            prev_ref[...] = jnp.zeros_like(prev_ref)

        before = prev_ref[...]
    else:
        o_refs = rest
    row = lax.broadcasted_iota(jnp.int32, (sub, 1), 0)
    for r0 in range(0, x_ref.shape[0], sub):
        h = _rms(x_ref[r0:r0 + sub, :], g_ref[...]).astype(BF16)
        off = 0
        for idx, (o_ref, n) in enumerate(zip(o_refs, splits)):
            y = _dot(h, w_ref[:, off:off + n])
            if tiles_per_seq and idx == 0:
                shifted = jnp.where(row == 0, before, pltpu.roll(y, 1, axis=0))
                before = y[sub - 1:sub, :]
                y = y + (shifted - y) * mu_ref[...]
            o_ref[r0:r0 + sub, :] = y
            off += n
    if tiles_per_seq:
        prev_ref[...] = before


def _norm_proj(x, g, w, layer, splits, *, shift_mu=None, seq_len=None, tm=1024, sub=256):
    T, D = x.shape
    N = w.shape[2]
    tm = min(tm, T)
    assert sum(splits) == N
    in_specs = [
        pl.BlockSpec((tm, D), lambda i: (i, 0)),
        pl.BlockSpec((1, D), lambda i: (0, 0)),
        pl.BlockSpec((None, D, N), lambda i: (layer, 0, 0)),
    ]
    args = [x, g.reshape(1, D), w]
    scratch, tiles_per_seq = [], 0
    if shift_mu is not None:
        assert seq_len % tm == 0
        tiles_per_seq = seq_len // tm
        in_specs.append(pl.BlockSpec((1, splits[0]), lambda i: (0, 0)))
        args.append(shift_mu.reshape(1, splits[0]))
        scratch = [pltpu.VMEM((1, splits[0]), F32)]
    return pl.pallas_call(
        functools.partial(_norm_proj_kernel, splits=splits, tiles_per_seq=tiles_per_seq,
                          sub=min(sub, tm)),
        grid=(T // tm,),
        in_specs=in_specs,
        out_specs=[pl.BlockSpec((tm, n), lambda i: (i, 0)) for n in splits],
        out_shape=[jax.ShapeDtypeStruct((T, n), F32) for n in splits],
        scratch_shapes=scratch,
        compiler_params=_params("arbitrary" if tiles_per_seq else "parallel"),
        name="norm_proj",
    )(*args)


def _rwkv_stages(p_ref, w0_ref, a0_ref, wwa_ref, gup_ref, kk_ref, ka_ref, rk_ref,
                 lng_ref, lnb_ref, o_ref, state_ref):
    C = RWKV_CHUNK
    W = RWKV_WIDTH
    NB, T, P = p_ref.shape
    n_pairs = W // LANES
    n_chunks = T // C

    @pl.when(pl.program_id(1) == 0)
    def _():
        state_ref[...] = jnp.zeros_like(state_ref)

    lane = lax.broadcasted_iota(jnp.int32, (1, LANES), 1)
    first_head = lane < HEAD_DIM
    ri = lax.broadcasted_iota(jnp.int32, (LANES, LANES), 0)
    ci = lax.broadcasted_iota(jnp.int32, (LANES, LANES), 1)
    same_head = (ri >> HEAD_SHIFT) == (ci >> HEAD_SHIFT)
    ri2 = lax.broadcasted_iota(jnp.int32, (MXU_DIM, MXU_DIM), 0)
    ci2 = lax.broadcasted_iota(jnp.int32, (MXU_DIM, MXU_DIM), 1)
    head_ones = ((ri2 >> HEAD_SHIFT) == (ci2 >> HEAD_SHIFT)).astype(F32)
    in_chunk_prefix = ((ci2 <= ri2) & ((ci2 >> CHUNK_SHIFT) == (ri2 >> CHUNK_SHIFT))).astype(F32)
    tok = lax.broadcasted_iota(jnp.int32, (C, LANES), 0)
    col_tok = lax.broadcasted_iota(jnp.int32, (C, LANES), 1) & (C - 1)
    strict = col_tok < tok
    incl = col_tok <= tok
    incl2 = jnp.concatenate([incl, incl], axis=1)
    eye = (col_tok == tok).astype(F32)
    lanes = [slice(i * LANES, (i + 1) * LANES) for i in range(n_pairs)]
    rows_of = lambda c: slice(c * C, (c + 1) * C)
    units = [(c, i) for c in range(n_chunks) for i in range(n_pairs)]
    c16 = lambda t: t.astype(BF16)

    def head_sum(t):
        return jnp.concatenate(
            [_mm(_dot, t[:, i * MXU_DIM:(i + 1) * MXU_DIM], head_ones, SPLIT_SUM)
             for i in range(W // MXU_DIM)],
            axis=1)

    def stack(t):
        zero = jnp.zeros_like(t)
        return jnp.concatenate([jnp.where(first_head, t, zero), jnp.where(first_head, zero, t)], axis=0)

    def tokenwise(nb, out):
        p = p_ref[nb]
        r, k, v = p[:, 0:W], p[:, W:2 * W], p[:, 2 * W:3 * W]
        lora = p[:, 3 * W:3 * W + LANES]
        g_lo = p[:, 3 * W + LANES:3 * W + 2 * LANES]
        z = jnp.where(first_head, jnp.tanh(lora), lora)
        wa = _mm(_dot, z, wwa_ref[...], SPLIT_LORA)
        g = _mm(_dot, _sigmoid(g_lo), gup_ref[...], SPLIT_LORA)
        yield
        logw = -math.exp(-0.5) * _sigmoid(w0_ref[...] + wa[:, :W])
        a = _sigmoid(a0_ref[...] + wa[:, W:])
        kk = k * kk_ref[...]
        k2 = k * (1.0 + (a - 1.0) * ka_ref[...])
        yield
        kkn = kk * lax.rsqrt(jnp.maximum(head_sum(kk * kk), 1e-24))
        b = kkn * a
        yield
        bonus = head_sum(r * k2 * rk_ref[...]) * v
        logp = jnp.concatenate(
            [_mm(_dot, in_chunk_prefix, logw[i * MXU_DIM:(i + 1) * MXU_DIM], SPLIT_CUMSUM)
             for i in range(T // MXU_DIM)], axis=0)
        yield
        inv = jnp.exp(-logp)
        out.update(a16=c16(-kkn * jnp.exp(logp - logw)), r16=c16(r * jnp.exp(logp)), v16=c16(v))
        yield
        out.update(b16=c16(b * inv), k16=c16(k2 * inv), b=b, k2=k2, logp=logp, bonus=bonus, g=g)

    def chunkwise(tw, out):
        a16, r16, b16, k16, v16 = tw["a16"], tw["r16"], tw["b16"], tw["k16"], tw["v16"]
        ar = {(c, i): jnp.concatenate([a16[rows_of(c), lanes[i]], r16[rows_of(c), lanes[i]]], axis=0)
              for c, i in units}
        bk = {(c, i): jnp.concatenate([stack(b16[rows_of(c), lanes[i]]), stack(k16[rows_of(c), lanes[i]])],
                                      axis=0) for c, i in units}
        v_s = {(c, i): stack(v16[rows_of(c), lanes[i]]) for c, i in units}
        yield
        gram = {u_: _dot_nt(ar[u_], bk[u_]) for u_ in units}
        yield
        l_ab = {u_: jnp.where(strict, gram[u_][:C, :LANES], 0.0) for u_ in units}
        a_ak = {u_: c16(jnp.where(strict, gram[u_][:C, LANES:], 0.0)) for u_ in units}
        a_r = {u_: c16(jnp.where(incl2, gram[u_][C:], 0.0)) for u_ in units}
        t_inv = {u_: eye + l_ab[u_] for u_ in units}
        lp16 = {u_: c16(l_ab[u_]) for u_ in units}
        yield
        l_pow = {u_: _dot(lp16[u_], stack(lp16[u_])) for u_ in units}
        av = {u_: _dot(a_ak[u_], v_s[u_]) for u_ in units}
        yield
        levels = int(math.log2(C)) - 1
        for level in range(levels - 1):
            lp16 = {u_: c16(l_pow[u_]) for u_ in units}
            both = {u_: _dot(jnp.concatenate([c16(t_inv[u_]), lp16[u_]], axis=0), stack(lp16[u_]))
                    for u_ in units}
            yield
            t_inv = {u_: t_inv[u_] + both[u_][:C] for u_ in units}
            l_pow = {u_: both[u_][C:] for u_ in units}
        last = {u_: _dot(c16(t_inv[u_]), stack(c16(l_pow[u_]))) for u_ in units}
        yield
        out.update(ar=ar, v_s=v_s, a_r=a_r, av=av, t16={u_: c16(t_inv[u_] + last[u_]) for u_ in units})

    def recurrent(nb, tw, cw, out):
        b, k2, logp, v16 = tw["b"], tw["k2"], tw["logp"], tw["v16"]
        ar, v_s, a_r, av, t16 = cw["ar"], cw["v_s"], cw["a_r"], cw["av"], cw["t16"]
        states = [state_ref[nb, i] for i in range(n_pairs)]
        y_chunks = []
        for c in range(n_chunks):
            logp_end = logp[(c + 1) * C - 1:(c + 1) * C, :]
            to_end = jnp.exp(logp_end - logp[rows_of(c)])
            bk_e = [c16(jnp.concatenate([b[rows_of(c), sl] * to_end[:, sl], k2[rows_of(c), sl] * to_end[:, sl]],
                                        axis=0)) for sl in lanes]
            from_state = [_dot_nt(ar[c, i], c16(states[i])) for i in range(n_pairs)]
            yield
            u16 = [c16(_dot(t16[c, i], stack(c16(from_state[i][:C] + av[c, i])))) for i in range(n_pairs)]
            yield
            outer = [_dot_tn(jnp.concatenate([u16[i], v16[rows_of(c), lanes[i]]], axis=0), bk_e[i])
                     for i in range(n_pairs)]
            y_chunks.append(jnp.concatenate(
                [from_state[i][C:] + _dot(a_r[c, i], jnp.concatenate([stack(u16[i]), v_s[c, i]], axis=0))
                 for i in range(n_pairs)], axis=1))
            yield
            decay_end = jnp.exp(logp_end)
            states = [states[i] * decay_end[:, lanes[i]] + jnp.where(same_head, outer[i], 0.0)
                      for i in range(n_pairs)]
        for i in range(n_pairs):
            state_ref[nb, i] = states[i]
        out.update(y=jnp.concatenate(y_chunks, axis=0))

    def finish(nb, tw, rc):
        y = rc["y"]
        mean = head_sum(y) * (1.0 / HEAD_DIM)
        yield
        yc = y - mean
        var = head_sum(yc * yc) * (1.0 / HEAD_DIM)
        yield
        y = yc * lax.rsqrt(var + LNX_EPS) * lng_ref[...] + lnb_ref[...]
        o_ref[nb] = (y + tw["bonus"]) * tw["g"]

    def in_turn(gens):
        while gens:
            gens = [gen for gen in gens if next(gen, gens) is not gens]
            yield

    tw = [dict() for _ in range(NB)]
    cw = [dict() for _ in range(NB)]
    rc = [dict() for _ in range(NB)]
    groups = [range(g0, min(g0 + RWKV_ROWS_IN_STEP, NB)) for g0 in range(0, NB, RWKV_ROWS_IN_STEP)]
    phases = [
        lambda rows: in_turn([tokenwise(nb, tw[nb]) for nb in rows]),
        lambda rows: in_turn([chunkwise(tw[nb], cw[nb]) for nb in rows]),
        lambda rows: in_turn([recurrent(nb, tw[nb], cw[nb], rc[nb]) for nb in rows]),
        lambda rows: in_turn([finish(nb, tw[nb], rc[nb]) for nb in rows]),
    ]
    for wave in range(len(groups) + len(phases) - 1):
        yield from in_turn([phases[wave - g](rows) for g, rows in enumerate(groups)
                            if 0 <= wave - g < len(phases)])


def _rwkv_kernel(*refs):
    for _ in _rwkv_stages(*refs):
        pass


def _rwkv_call_parts(p, w0, w_up, a0, a_up, g_up, k_k, k_a, r_k, ln_g, ln_b, ts, nb):
    B, S, P = p.shape
    W = RWKV_WIDTH
    wwa = jnp.zeros((LANES, 2 * W), F32)
    wwa = wwa.at[:DECAY_LORA, :W].set(w_up).at[DECAY_LORA:, W:].set(a_up)
    vec = lambda t: t.reshape(1, -1)
    row_spec = lambda n: pl.BlockSpec((1, n), lambda b, s: (0, 0))
    assert ts % MXU_DIM == 0 and S % ts == 0 and B % nb == 0
    args = (p, vec(w0), vec(a0), wwa, g_up, vec(k_k), vec(k_a), vec(r_k), vec(ln_g), vec(ln_b))
    in_specs = [
        pl.BlockSpec((nb, ts, P), lambda b, s: (b, s, 0)),
        row_spec(W), row_spec(W),
        pl.BlockSpec((LANES, 2 * W), lambda b, s: (0, 0)),
        pl.BlockSpec((GATE_LORA, W), lambda b, s: (0, 0)),
        row_spec(W), row_spec(W), row_spec(W), row_spec(W), row_spec(W),
    ]
    out_spec = pl.BlockSpec((nb, ts, W), lambda b, s: (b, s, 0))
    out_shape = jax.ShapeDtypeStruct((B, S, W), F32)
    scratch = [pltpu.VMEM((nb, W // LANES, LANES, LANES), F32)]
    return (B // nb, S // ts), args, in_specs, out_spec, out_shape, scratch


def _rwkv(p, *weights, ts=256, nb=2):
    B, S, _ = p.shape
    grid, args, in_specs, out_spec, out_shape, scratch = _rwkv_call_parts(p, *weights, min(ts, S), min(nb, B))
    return pl.pallas_call(
        _rwkv_kernel,
        grid=grid,
        in_specs=in_specs,
        out_specs=out_spec,
        out_shape=out_shape,
        scratch_shapes=scratch,
        compiler_params=_params("parallel", "arbitrary"),
        name="rwkv7",
    )(*args)


def _rel_bias_kernel(tab_ref, *refs, n_blocks, n_side):
    side_in, o_ref, side_out = refs[:n_side], refs[n_side], refs[n_side + 1:]
    blk = MOBA_BLOCK
    max_exact = REL_BUCKETS // 2

    def bucket_of(n):
        large = max_exact + math.floor(math.log(max(n, 1) / max_exact)
                                       / math.log(REL_MAX_DISTANCE / max_exact) * (REL_BUCKETS - max_exact))
        return n if n < max_exact else min(large, REL_BUCKETS - 1)

    def block(cb):
        kc = lax.broadcasted_iota(jnp.int32, (blk, blk), 0) + cb * blk
        qi = lax.broadcasted_iota(jnp.int32, (blk, blk), 1)
        dist = qi + (n_blocks - 1) * blk - kc
        n = jnp.maximum(dist, 0)
        nf = jnp.maximum(n, 1).astype(F32)
        large = max_exact + (jnp.log(nf / max_exact) / math.log(REL_MAX_DISTANCE / max_exact)
                             * (REL_BUCKETS - max_exact)).astype(jnp.int32)
        large = jnp.minimum(large, REL_BUCKETS - 1)
        bucket = jnp.where(n < max_exact, n, large)
        n_lo = max((n_blocks - 1 - cb) * blk - (blk - 1), 0)
        n_hi = max((n_blocks - 1 - cb) * blk + (blk - 1), 0)
        b_lo = max(bucket_of(n_lo) - 1, 0)
        b_hi = min(bucket_of(n_hi) + 1, REL_BUCKETS - 1)
        for h in range(MOBA_HEADS):
            tile = jnp.zeros((blk, blk), F32)
            for c in range(b_lo, b_hi + 1):
                tile = jnp.where(bucket == c, tab_ref[h, c], tile)
            o_ref[h] = jnp.where(dist < 0, -jnp.inf, tile * LOG2E)

    for cb in range(n_blocks):
        pl.when(pl.program_id(0) == cb)(functools.partial(block, cb))
    for w_ref, w16_ref in zip(side_in, side_out):
        w16_ref[...] = w_ref[...].astype(BF16)


def _rel_bias_strip(rel_bias, n_blocks, side=()):
    blk = MOBA_BLOCK
    side_args, side_in, side_out, side_shape = _side_cast_parts(side, n_blocks)
    strip, *side16 = pl.pallas_call(
        functools.partial(_rel_bias_kernel, n_blocks=n_blocks, n_side=len(side)),
        grid=(n_blocks,),
        in_specs=[pl.BlockSpec(memory_space=pltpu.SMEM)] + side_in,
        out_specs=[pl.BlockSpec((MOBA_HEADS, blk, blk), lambda c: (0, c, 0))] + side_out,
        out_shape=[jax.ShapeDtypeStruct((MOBA_HEADS, n_blocks * blk, blk), F32)] + side_shape,
        compiler_params=_params("parallel"),
        name="rel_bias_strip",
    )(rel_bias, *side_args)
    return strip, [w16.reshape((1,) + w.shape[1:]) for w16, (w, _) in zip(side16, side)]


def _moba_stages(q_ref, k_ref, v_ref, bias_ref, o_ref, *, n_blocks):
    blk = MOBA_BLOCK
    lane = lax.broadcasted_iota(jnp.int32, (1, LANES), 1)
    vrow = lax.broadcasted_iota(jnp.int32, (LANES, 1), 0)
    q = q_ref[0] * (HEAD_DIM ** -0.5 * LOG2E)
    k = k_ref[0]
    k16 = k.astype(BF16)
    vt = v_ref[0].T
    vt16 = [jnp.where((vrow >> HEAD_SHIFT) == e, vt, 1.0).astype(BF16) for e in range(HEADS_PER_TILE)]
    kmean = jnp.mean(k.reshape(n_blocks, blk, LANES), axis=1)

    q_hi, q_lo = _bf16_terms(q, 2)
    m_hi, m_lo = _bf16_terms(kmean, 2)
    gates, q16 = [], []
    for e in range(HEADS_PER_TILE):
        head_lanes = (lane >> HEAD_SHIFT) == e
        zero = jnp.zeros_like(q_hi)
        qe_hi, qe_lo = jnp.where(head_lanes, q_hi, zero), jnp.where(head_lanes, q_lo, zero)
        gates.append(_dot_nt(m_hi, qe_hi) + _dot_nt(m_hi, qe_lo) + _dot_nt(m_lo, qe_hi))
        q16.append(qe_hi)

    def scores(qb, e):
        return _dot_nt(k16[:(qb + 1) * blk], q16[e][qb * blk:(qb + 1) * blk])

    def weights(qb, e, s_t):
        rows = slice(qb * blk, (qb + 1) * blk)
        bias_lo = (n_blocks - 1 - qb) * blk
        g = [gates[e][j:j + 1, rows] for j in range(qb)]
        tiles = []
        for j in range(qb + 1):
            t = s_t[j * blk:(j + 1) * blk] + bias_ref[e, bias_lo + j * blk:bias_lo + (j + 1) * blk, :]
            if j < qb:
                rank = jnp.zeros((1, blk), jnp.int32)
                for jj in range(qb):
                    if jj != j:
                        ahead = (g[jj] >= g[j]) if jj < j else (g[jj] > g[j])
                        rank = rank + jnp.where(ahead, 1, 0)
                t = jnp.where(rank < MOBA_TOPK, t, -jnp.inf)
            tiles.append(t)
        m = tiles[0].max(axis=0, keepdims=True)
        for t in tiles[1:]:
            m = jnp.maximum(m, t.max(axis=0, keepdims=True))
        return jnp.concatenate([jnp.exp2(t - m).astype(BF16) for t in tiles], axis=0)

    def attend(qb, e, p16):
        pv = _dot(vt16[e][:, :(qb + 1) * blk], p16)
        den_row = (1 - e) * HEAD_DIM
        return pv / pv[den_row:den_row + 1]

    units = [(qb, e) for qb in range(n_blocks) for e in range(HEADS_PER_TILE)]
    s_t, p16, out_t = {}, {}, {}
    for step in range(len(units) + 2):
        if 0 <= step - 2 < len(units):
            u = units[step - 2]
            out_t[u] = attend(*u, p16.pop(u))
            qb, e = u
            if e == HEADS_PER_TILE - 1:
                o_ref[0, qb * blk:(qb + 1) * blk, :] = jnp.where(
                    vrow < HEAD_DIM, out_t.pop((qb, 0)), out_t.pop((qb, 1))).T
        if step < len(units):
            s_t[units[step]] = scores(*units[step])
        if 0 <= step - 1 < len(units):
            u = units[step - 1]
            p16[u] = weights(*u, s_t.pop(u))
        yield


def _moba_kernel(*refs, n_blocks):
    for _ in _moba_stages(*refs, n_blocks=n_blocks):
        pass


def _moba(qkv, bias_strip):
    B, S, _ = qkv.shape
    blk = MOBA_BLOCK
    n_blocks = S // blk
    n_pairs = MOBA_WIDTH // LANES
    return pl.pallas_call(
        functools.partial(_moba_kernel, n_blocks=n_blocks),
        grid=(n_pairs, B),
        in_specs=[
            pl.BlockSpec((1, S, LANES), lambda p, b: (b, 0, p)),
            pl.BlockSpec((1, S, LANES), lambda p, b: (b, 0, n_pairs + p)),
            pl.BlockSpec((1, S, LANES), lambda p, b: (b, 0, 2 * n_pairs + p)),
            pl.BlockSpec((HEADS_PER_TILE, S, blk), lambda p, b: (p, 0, 0)),
        ],
        out_specs=pl.BlockSpec((1, S, LANES), lambda p, b: (b, 0, p)),
        out_shape=jax.ShapeDtypeStruct((B, S, MOBA_WIDTH), F32),
        compiler_params=_params("parallel", "parallel"),
        name="moba",
    )(qkv, qkv, qkv, bias_strip)


N_RWKV_INPUTS = 10
N_MOBA_INPUTS = 4


def _mixers_kernel(*refs, n_blocks):
    rwkv_in, refs = refs[:N_RWKV_INPUTS], refs[N_RWKV_INPUTS:]
    moba_in, (y_rwkv_ref, y_moba_ref, state_ref) = refs[:N_MOBA_INPUTS], refs[N_MOBA_INPUTS:]
    jobs = [(_rwkv_stages(*rwkv_in, y_rwkv_ref, state_ref), MIXER_STAGE_RATIO[0]),
            (_moba_stages(*moba_in, y_moba_ref, n_blocks=n_blocks), MIXER_STAGE_RATIO[1])]
    while jobs:
        jobs = [(gen, n) for gen, n in jobs if all(next(gen, jobs) is not jobs for _ in range(n))]


def _mixers(p, qkv, bias_strip, *rwkv_weights, ts=256, nb=2):
    B, S, _ = qkv.shape
    blk = MOBA_BLOCK
    n_blocks = S // blk
    n_pairs = MOBA_WIDTH // LANES
    ts, nb = min(ts, S), min(nb, B)
    grid, args, in_specs, out_spec, out_shape, scratch = _rwkv_call_parts(p, *rwkv_weights, ts, nb)
    if grid[0] * grid[1] != n_pairs * B:
        return _rwkv(p, *rwkv_weights, ts=ts, nb=nb), _moba(qkv, bias_strip)
    item = lambda g, s: g * grid[1] + s
    row = lambda g, s: item(g, s) % B
    pair = lambda g, s: item(g, s) // B
    in_specs = in_specs + [
        pl.BlockSpec((1, S, LANES), lambda g, s: (row(g, s), 0, pair(g, s))),
        pl.BlockSpec((1, S, LANES), lambda g, s: (row(g, s), 0, n_pairs + pair(g, s))),
        pl.BlockSpec((1, S, LANES), lambda g, s: (row(g, s), 0, 2 * n_pairs + pair(g, s))),
        pl.BlockSpec((HEADS_PER_TILE, S, blk), lambda g, s: (pair(g, s), 0, 0)),
    ]
    assert len(args) == N_RWKV_INPUTS
    return pl.pallas_call(
        functools.partial(_mixers_kernel, n_blocks=n_blocks),
        grid=grid,
        in_specs=in_specs,
        out_specs=[out_spec, pl.BlockSpec((1, S, LANES), lambda g, s: (row(g, s), 0, pair(g, s)))],
        out_shape=[out_shape, jax.ShapeDtypeStruct((B, S, MOBA_WIDTH), F32)],
        scratch_shapes=scratch,
        compiler_params=_params("parallel", "arbitrary"),
        name="mixers",
    )(*args, qkv, qkv, qkv, bias_strip)


def _xattn_kernel(x_ref, ya_ref, yb_ref, wa_ref, wb_ref, g_ref, wq_ref, mem_ref, mg_ref, wkv_ref, wo_ref,
                  o_ref, kv_ref, *, sub):
    D = x_ref.shape[-1]
    dh = D // XATTN_HEADS

    @pl.when(pl.program_id(1) == 0)
    def _():
        kv_ref[...] = _dot(_rms(mem_ref[0], mg_ref[...]).astype(BF16), wkv_ref[...]).astype(BF16)

    k16 = [kv_ref[:, i * dh:(i + 1) * dh] for i in range(XATTN_HEADS)]
    v16 = [kv_ref[:, D + i * dh:D + (i + 1) * dh] for i in range(XATTN_HEADS)]

    def project(r0):
        rows = slice(r0, r0 + sub)
        x = (x_ref[0, rows, :] + _dot(ya_ref[0, rows, :].astype(BF16), wa_ref[...])
             + _dot(yb_ref[0, rows, :].astype(BF16), wb_ref[...]))
        h = _rms(x, g_ref[...]).astype(BF16)
        q = (_dot(h, wq_ref[...]) * (dh ** -0.5 * LOG2E)).astype(BF16)
        return x, q

    def attend(q):
        hs = range(XATTN_HEADS)
        s = [_dot_nt(q[:, i * dh:(i + 1) * dh], k16[i]) for i in hs]
        pr = [jnp.exp2(s[i] - jnp.max(s[i], axis=-1, keepdims=True)) for i in hs]
        pr = [(pr[i] / jnp.sum(pr[i], axis=-1, keepdims=True)).astype(BF16) for i in hs]
        return jnp.concatenate([_dot(pr[i], v16[i]) for i in hs], axis=1).astype(BF16)

    starts = list(range(0, x_ref.shape[1], sub))
    xq, att = {}, {}
    for step in range(len(starts) + 2):
        if step < len(starts):
            xq[starts[step]] = project(starts[step])
        if 0 <= step - 1 < len(starts):
            r0 = starts[step - 1]
            att[r0] = attend(xq[r0][1])
        if 0 <= step - 2 < len(starts):
            r0 = starts[step - 2]
            o_ref[0, r0:r0 + sub, :] = xq.pop(r0)[0] + _dot(att.pop(r0), wo_ref[...])


def _xattn(x, ya, yb, w_mix, g, w_q, mem, mem_g, w_kv, w_o, layer, *, tm=1024, sub=1024):
    B, S, D = x.shape
    M = mem.shape[1]
    na, nb = ya.shape[-1], yb.shape[-1]
    assert na == nb
    tm = min(tm, S)
    of_layer = lambda shape, blk=0: pl.BlockSpec((None,) + shape, lambda b, i: (layer, blk, 0),
                                                 pipeline_mode=pl.Buffered(1))
    return pl.pallas_call(
        functools.partial(_xattn_kernel, sub=min(sub, tm)),
        grid=(B, S // tm),
        in_specs=[
            pl.BlockSpec((1, tm, D), lambda b, i: (b, i, 0)),
            pl.BlockSpec((1, tm, na), lambda b, i: (b, i, 0)),
            pl.BlockSpec((1, tm, nb), lambda b, i: (b, i, 0)),
            of_layer((na, D), 0),
            of_layer((nb, D), 1),
            pl.BlockSpec((1, D), lambda b, i: (0, 0)),
            of_layer((D, D)),
            pl.BlockSpec((1, M, D), lambda b, i: (b, 0, 0)),
            pl.BlockSpec((1, D), lambda b, i: (0, 0)),
            of_layer((D, 2 * D)),
            of_layer((D, D)),
        ],
        out_specs=pl.BlockSpec((1, tm, D), lambda b, i: (b, i, 0)),
        out_shape=jax.ShapeDtypeStruct((B, S, D), F32),
        scratch_shapes=[pltpu.VMEM((M, 2 * D), BF16)],
        compiler_params=_params("parallel", "arbitrary"),
        name="xattn",
    )(x, ya, yb, w_mix, w_mix, g.reshape(1, D), w_q, mem, mem_g.reshape(1, D), w_kv, w_o)


def kernel(x, mem, rel_bias, final_norm_g, ffn1_norm_g, ffn1_w_in, ffn1_w_out, mix_norm_g, w_mix_in, w_mix_out, rwkv_mu, rwkv_w0, rwkv_w_up, rwkv_a0, rwkv_a_up, rwkv_g_up, rwkv_k_k, rwkv_k_a, rwkv_r_k, rwkv_ln_g, rwkv_ln_b, xattn_norm_g, mem_norm_g, xattn_w_q, xattn_w_kv, xattn_w_o, ffn2_norm_g, ffn2_w_in, ffn2_w_out):
    B, S, D = x.shape
    M = mem.shape[1]
    depth = ffn1_w_in.shape[0]
    T = B * S
    bias_strip, (ffn1_in16, ffn1_out16) = _rel_bias_strip(rel_bias, S // MOBA_BLOCK,
                                                         side=[(ffn1_w_in, 0), (ffn1_w_out, 0)])
    x = x.reshape(T, D)
    for l in range(depth):
        later = (w_mix_in, xattn_w_kv, w_mix_out, xattn_w_q, xattn_w_o, ffn2_w_in, ffn2_w_out)
        x, (mix_in16, kv16, mix_out16, q16, o16, ffn2_in16, ffn2_out16) = _ffn(
            x, ffn1_norm_g[l], ffn1_in16, ffn1_out16, 0, side=[(w, l) for w in later])
        p_rwkv, qkv = _norm_proj(x, mix_norm_g[l], mix_in16, 0, (RWKV_PROJ, MOBA_PROJ),
                                 shift_mu=rwkv_mu[l], seq_len=S)
        y_rwkv, y_moba = _mixers(p_rwkv.reshape(B, S, RWKV_PROJ), qkv.reshape(B, S, MOBA_PROJ), bias_strip,
                                 rwkv_w0[l], rwkv_w_up[l], rwkv_a0[l], rwkv_a_up[l], rwkv_g_up[l],
                                 rwkv_k_k[l], rwkv_k_a[l], rwkv_r_k[l], rwkv_ln_g[l], rwkv_ln_b[l])
        x = _xattn(x.reshape(B, S, D), y_rwkv, y_moba, mix_out16, xattn_norm_g[l],
                   q16, mem, mem_norm_g[l], kv16, o16, 0).reshape(T, D)
        last = l == depth - 1
        x, next_ffn1 = _ffn(x, ffn2_norm_g[l], ffn2_in16, ffn2_out16, 0, final_norm_g if last else None,
                            side=[] if last else [(ffn1_w_in, l + 1), (ffn1_w_out, l + 1)])
        if not last:
            ffn1_in16, ffn1_out16 = next_ffn1
    return x.reshape(B, S, D)
```

```python
import functools
import math

import jax
import jax.numpy as jnp
from jax import lax
from jax.experimental import pallas as pl
from jax.experimental.pallas import tpu as pltpu

F32 = jnp.float32
BF16 = jnp.bfloat16

HEAD_DIM = 64
RWKV_WIDTH = 512
MOBA_WIDTH = 512
MOBA_HEADS = MOBA_WIDTH // HEAD_DIM
DECAY_LORA = 64
ICLR_LORA = 64
GATE_LORA = 128
RWKV_PROJ = 3 * RWKV_WIDTH + DECAY_LORA + ICLR_LORA + GATE_LORA
MOBA_PROJ = 3 * MOBA_WIDTH
LNX_EPS = 64e-5
MOBA_BLOCK = 256
MOBA_TOPK = 3
REL_BUCKETS = 32
REL_MAX_DISTANCE = 1024
XATTN_HEADS = 4
FFN_RES_WEIGHT = 0.5
NORM_EPS = 1e-6
LOG2E = math.log2(math.e)

LANES = 128
MXU_DIM = 256
HEADS_PER_TILE = LANES // HEAD_DIM
HEAD_SHIFT = HEAD_DIM.bit_length() - 1
RWKV_CHUNK = 64
CHUNK_SHIFT = RWKV_CHUNK.bit_length() - 1
RWKV_ROWS_IN_STEP = 2
MIXER_STAGE_RATIO = (3, 2)
SPLIT_LORA = (1, 1, 1)
SPLIT_SUM = (1, 1, 1)
SPLIT_CUMSUM = (1, 2, 2)
VMEM_LIMIT = 56 * 1024 * 1024


def _rms(x, g):
    ms = jnp.mean(x * x, axis=-1, keepdims=True)
    return x * lax.rsqrt(ms + NORM_EPS) * g


def _sigmoid(x):
    return 0.5 * jnp.tanh(0.5 * x) + 0.5


def _dot(a, b, precision=None):
    return jnp.dot(a, b, precision=precision, preferred_element_type=F32)


def _dot_nt(a, b, precision=None):
    return lax.dot_general(a, b, (((1,), (1,)), ((), ())), precision=precision,
                           preferred_element_type=F32)


def _dot_tn(a, b, precision=None):
    return lax.dot_general(a, b, (((0,), (0,)), ((), ())), precision=precision,
                           preferred_element_type=F32)


def _bf16_terms(x, n):
    terms = []
    for i in range(n):
        t = x.astype(BF16)
        terms.append(t)
        if i + 1 < n:
            x = x - t.astype(F32)
    return terms


def _mm(dot, a, b, split):
    na, nb, order = split
    at, bt = _bf16_terms(a, na), _bf16_terms(b, nb)
    acc = None
    for i in range(na):
        for j in range(nb):
            if i + j < order:
                t = dot(at[i], bt[j])
                acc = t if acc is None else acc + t
    return acc


def _params(*semantics):
    return pltpu.CompilerParams(dimension_semantics=semantics, vmem_limit_bytes=VMEM_LIMIT)


def _ffn_kernel(x_ref, g_ref, wi_ref, wo_ref, *rest, final_norm, sub, n_side):
    rest = list(rest)
    fg_ref = rest.pop(0) if final_norm else None
    side_in, o_ref, side_out = rest[:n_side], rest[n_side], rest[n_side + 1:]
    F = wo_ref.shape[0]
    for r0 in range(0, x_ref.shape[0], sub):
        x = x_ref[r0:r0 + sub, :]
        h = _rms(x, g_ref[...]).astype(BF16)
        gate = _dot(h, wi_ref[:, :F])
        up = _dot(h, wi_ref[:, F:])
        act = (gate * _sigmoid(gate) * up).astype(BF16)
        y = x + FFN_RES_WEIGHT * _dot(act, wo_ref[...])
        if final_norm:
            y = _rms(y, fg_ref[...])
        o_ref[r0:r0 + sub, :] = y
    for w_ref, w16_ref in zip(side_in, side_out):
        w16_ref[...] = w_ref[...].astype(BF16)


def _side_cast_parts(side, steps):
    args, in_specs, out_specs, out_shape = [], [], [], []
    for w, w_layer in side:
        n_layers, rows, cols = w.shape
        assert rows % steps == 0
        slab = (rows // steps, cols)
        in_specs.append(pl.BlockSpec((None, None) + slab, lambda i, w_layer=w_layer: (w_layer, i, 0, 0)))
        args.append(w.reshape((n_layers, steps) + slab))
        out_specs.append(pl.BlockSpec((None,) + slab, lambda i: (i, 0, 0)))
        out_shape.append(jax.ShapeDtypeStruct((steps,) + slab, BF16))
    return args, in_specs, out_specs, out_shape


def _ffn(x, g, w_in, w_out, layer, final_g=None, side=(), *, tm=1024, sub=256):
    T, D = x.shape
    F = w_out.shape[1]
    tm = min(tm, T)
    steps = T // tm
    resident = lambda shape: pl.BlockSpec(shape, lambda i: (0, 0), pipeline_mode=pl.Buffered(1))
    of_layer = lambda shape: pl.BlockSpec((None,) + shape, lambda i: (layer, 0, 0),
                                          pipeline_mode=pl.Buffered(1))
    in_specs = [
        pl.BlockSpec((tm, D), lambda i: (i, 0)),
        resident((1, D)),
        of_layer((D, 2 * F)),
        of_layer((F, D)),
    ]
    args = [x, g.reshape(1, D), w_in, w_out]
    if final_g is not None:
        in_specs.append(resident((1, D)))
        args.append(final_g.reshape(1, D))
    side_args, side_in, side_out, side_shape = _side_cast_parts(side, steps)
    y, *side16 = pl.pallas_call(
        functools.partial(_ffn_kernel, final_norm=final_g is not None, sub=min(sub, tm), n_side=len(side)),
        grid=(steps,),
        in_specs=in_specs + side_in,
        out_specs=[pl.BlockSpec((tm, D), lambda i: (i, 0))] + side_out,
        out_shape=[jax.ShapeDtypeStruct((T, D), F32)] + side_shape,
        compiler_params=_params("parallel"),
        name="ffn",
    )(*args, *side_args)
    return y, [w16.reshape((1,) + w.shape[1:]) for w16, (w, _) in zip(side16, side)]


def _norm_proj_kernel(x_ref, g_ref, w_ref, *rest, splits, tiles_per_seq, sub):
    if tiles_per_seq:
        mu_ref, *o_refs, prev_ref = rest

        @pl.when(pl.program_id(0) % tiles_per_seq == 0)
        def _():
            prev_ref[...] = jnp.zeros_like(prev_ref)

        before = prev_ref[...]
    else:
        o_refs = rest
    row = lax.broadcasted_iota(jnp.int32, (sub, 1), 0)
    for r0 in range(0, x_ref.shape[0], sub):
        h = _rms(x_ref[r0:r0 + sub, :], g_ref[...]).astype(BF16)
        off = 0
        for idx, (o_ref, n) in enumerate(zip(o_refs, splits)):
            y = _dot(h, w_ref[:, off:off + n])
            if tiles_per_seq and idx == 0:
                shifted = jnp.where(row == 0, before, pltpu.roll(y, 1, axis=0))
                before = y[sub - 1:sub, :]
                y = y + (shifted - y) * mu_ref[...]
            o_ref[r0:r0 + sub, :] = y.astype(o_ref.dtype)
            off += n
    if tiles_per_seq:
        prev_ref[...] = before


def _norm_proj(x, g, w, layer, splits, *, out_dtypes=None, shift_mu=None, seq_len=None, tm=1024, sub=256):
    T, D = x.shape
    N = w.shape[2]
    tm = min(tm, T)
    assert sum(splits) == N
    in_specs = [
        pl.BlockSpec((tm, D), lambda i: (i, 0)),
        pl.BlockSpec((1, D), lambda i: (0, 0)),
        pl.BlockSpec((None, D, N), lambda i: (layer, 0, 0)),
    ]
    args = [x, g.reshape(1, D), w]
    scratch, tiles_per_seq = [], 0
    if shift_mu is not None:
        assert seq_len % tm == 0
        tiles_per_seq = seq_len // tm
        in_specs.append(pl.BlockSpec((1, splits[0]), lambda i: (0, 0)))
        args.append(shift_mu.reshape(1, splits[0]))
        scratch = [pltpu.VMEM((1, splits[0]), F32)]
    return pl.pallas_call(
        functools.partial(_norm_proj_kernel, splits=splits, tiles_per_seq=tiles_per_seq,
                          sub=min(sub, tm)),
        grid=(T // tm,),
        in_specs=in_specs,
        out_specs=[pl.BlockSpec((tm, n), lambda i: (i, 0)) for n in splits],
        out_shape=[jax.ShapeDtypeStruct((T, n), dt) for n, dt in zip(splits, out_dtypes or (F32,) * len(splits))],
        scratch_shapes=scratch,
        compiler_params=_params("arbitrary" if tiles_per_seq else "parallel"),
        name="norm_proj",
    )(*args)


def _rwkv_stages(p_ref, w0_ref, a0_ref, wwa_ref, gup_ref, kk_ref, ka_ref, rk_ref,
                 lng_ref, lnb_ref, o_ref, state_ref):
    C = RWKV_CHUNK
    W = RWKV_WIDTH
    NB, T, P = p_ref.shape
    n_pairs = W // LANES
    n_chunks = T // C

    @pl.when(pl.program_id(1) == 0)
    def _():
        state_ref[...] = jnp.zeros_like(state_ref)

    lane = lax.broadcasted_iota(jnp.int32, (1, LANES), 1)
    first_head = lane < HEAD_DIM
    ri = lax.broadcasted_iota(jnp.int32, (LANES, LANES), 0)
    ci = lax.broadcasted_iota(jnp.int32, (LANES, LANES), 1)
    same_head = (ri >> HEAD_SHIFT) == (ci >> HEAD_SHIFT)
    ri2 = lax.broadcasted_iota(jnp.int32, (MXU_DIM, MXU_DIM), 0)
    ci2 = lax.broadcasted_iota(jnp.int32, (MXU_DIM, MXU_DIM), 1)
    head_ones = ((ri2 >> HEAD_SHIFT) == (ci2 >> HEAD_SHIFT)).astype(F32)
    in_chunk_prefix = ((ci2 <= ri2) & ((ci2 >> CHUNK_SHIFT) == (ri2 >> CHUNK_SHIFT))).astype(F32)
    tok = lax.broadcasted_iota(jnp.int32, (C, LANES), 0)
    col_tok = lax.broadcasted_iota(jnp.int32, (C, LANES), 1) & (C - 1)
    strict = col_tok < tok
    incl = col_tok <= tok
    incl2 = jnp.concatenate([incl, incl], axis=1)
    eye = (col_tok == tok).astype(F32)
    lanes = [slice(i * LANES, (i + 1) * LANES) for i in range(n_pairs)]
    rows_of = lambda c: slice(c * C, (c + 1) * C)
    units = [(c, i) for c in range(n_chunks) for i in range(n_pairs)]
    c16 = lambda t: t.astype(BF16)

    def head_sum(t):
        return jnp.concatenate(
            [_mm(_dot, t[:, i * MXU_DIM:(i + 1) * MXU_DIM], head_ones, SPLIT_SUM)
             for i in range(W // MXU_DIM)],
            axis=1)

    def stack(t):
        zero = jnp.zeros_like(t)
        return jnp.concatenate([jnp.where(first_head, t, zero), jnp.where(first_head, zero, t)], axis=0)

    def tokenwise(nb, out):
        p = p_ref[nb]
        r, k, v = p[:, 0:W], p[:, W:2 * W], p[:, 2 * W:3 * W]
        lora = p[:, 3 * W:3 * W + LANES]
        g_lo = p[:, 3 * W + LANES:3 * W + 2 * LANES]
        z = jnp.where(first_head, jnp.tanh(lora), lora)
        wa = _mm(_dot, z, wwa_ref[...], SPLIT_LORA)
        g = _mm(_dot, _sigmoid(g_lo), gup_ref[...], SPLIT_LORA)
        yield
        logw = -math.exp(-0.5) * _sigmoid(w0_ref[...] + wa[:, :W])
        a = _sigmoid(a0_ref[...] + wa[:, W:])
        kk = k * kk_ref[...]
        k2 = k * (1.0 + (a - 1.0) * ka_ref[...])
        yield
        kkn = kk * lax.rsqrt(jnp.maximum(head_sum(kk * kk), 1e-24))
        b = kkn * a
        yield
        bonus = head_sum(r * k2 * rk_ref[...]) * v
        logp = jnp.concatenate(
            [_mm(_dot, in_chunk_prefix, logw[i * MXU_DIM:(i + 1) * MXU_DIM], SPLIT_CUMSUM)
             for i in range(T // MXU_DIM)], axis=0)
        yield
        inv = jnp.exp(-logp)
        out.update(a16=c16(-kkn * jnp.exp(logp - logw)), r16=c16(r * jnp.exp(logp)), v16=c16(v))
        yield
        out.update(b16=c16(b * inv), k16=c16(k2 * inv), b=b, k2=k2, logp=logp, bonus=bonus, g=g)

    def chunkwise(tw, out):
        a16, r16, b16, k16, v16 = tw["a16"], tw["r16"], tw["b16"], tw["k16"], tw["v16"]
        ar = {(c, i): jnp.concatenate([a16[rows_of(c), lanes[i]], r16[rows_of(c), lanes[i]]], axis=0)
              for c, i in units}
        bk = {(c, i): jnp.concatenate([stack(b16[rows_of(c), lanes[i]]), stack(k16[rows_of(c), lanes[i]])],
                                      axis=0) for c, i in units}
        v_s = {(c, i): stack(v16[rows_of(c), lanes[i]]) for c, i in units}
        yield
        gram = {u_: _dot_nt(ar[u_], bk[u_]) for u_ in units}
        yield
        l_ab = {u_: jnp.where(strict, gram[u_][:C, :LANES], 0.0) for u_ in units}
        a_ak = {u_: c16(jnp.where(strict, gram[u_][:C, LANES:], 0.0)) for u_ in units}
        a_r = {u_: c16(jnp.where(incl2, gram[u_][C:], 0.0)) for u_ in units}
        t_inv = {u_: eye + l_ab[u_] for u_ in units}
        lp16 = {u_: c16(l_ab[u_]) for u_ in units}
        yield
        l_pow = {u_: _dot(lp16[u_], stack(lp16[u_])) for u_ in units}
        av = {u_: _dot(a_ak[u_], v_s[u_]) for u_ in units}
        yield
        levels = int(math.log2(C)) - 1
        for level in range(levels - 1):
            lp16 = {u_: c16(l_pow[u_]) for u_ in units}
            both = {u_: _dot(jnp.concatenate([c16(t_inv[u_]), lp16[u_]], axis=0), stack(lp16[u_]))
                    for u_ in units}
            yield
            t_inv = {u_: t_inv[u_] + both[u_][:C] for u_ in units}
            l_pow = {u_: both[u_][C:] for u_ in units}
        last = {u_: _dot(c16(t_inv[u_]), stack(c16(l_pow[u_]))) for u_ in units}
        yield
        out.update(ar=ar, v_s=v_s, a_r=a_r, av=av, t16={u_: c16(t_inv[u_] + last[u_]) for u_ in units})

    def recurrent(nb, tw, cw, out):
        b, k2, logp, v16 = tw["b"], tw["k2"], tw["logp"], tw["v16"]
        ar, v_s, a_r, av, t16 = cw["ar"], cw["v_s"], cw["a_r"], cw["av"], cw["t16"]
        states = [state_ref[nb, i] for i in range(n_pairs)]
        y_chunks = []
        for c in range(n_chunks):
            logp_end = logp[(c + 1) * C - 1:(c + 1) * C, :]
            to_end = jnp.exp(logp_end - logp[rows_of(c)])
            bk_e = [c16(jnp.concatenate([b[rows_of(c), sl] * to_end[:, sl], k2[rows_of(c), sl] * to_end[:, sl]],
                                        axis=0)) for sl in lanes]
            from_state = [_dot_nt(ar[c, i], c16(states[i])) for i in range(n_pairs)]
            yield
            u16 = [c16(_dot(t16[c, i], stack(c16(from_state[i][:C] + av[c, i])))) for i in range(n_pairs)]
            yield
            outer = [_dot_tn(jnp.concatenate([u16[i], v16[rows_of(c), lanes[i]]], axis=0), bk_e[i])
                     for i in range(n_pairs)]
            y_chunks.append(jnp.concatenate(
                [from_state[i][C:] + _dot(a_r[c, i], jnp.concatenate([stack(u16[i]), v_s[c, i]], axis=0))
                 for i in range(n_pairs)], axis=1))
            yield
            decay_end = jnp.exp(logp_end)
            states = [states[i] * decay_end[:, lanes[i]] + jnp.where(same_head, outer[i], 0.0)
                      for i in range(n_pairs)]
        for i in range(n_pairs):
            state_ref[nb, i] = states[i]
        out.update(y=jnp.concatenate(y_chunks, axis=0))

    def finish(nb, tw, rc):
        y = rc["y"]
        mean = head_sum(y) * (1.0 / HEAD_DIM)
        yield
        yc = y - mean
        var = head_sum(yc * yc) * (1.0 / HEAD_DIM)
        yield
        y = yc * lax.rsqrt(var + LNX_EPS) * lng_ref[...] + lnb_ref[...]
        o_ref[nb] = (y + tw["bonus"]) * tw["g"]

    def in_turn(gens):
        while gens:
            gens = [gen for gen in gens if next(gen, gens) is not gens]
            yield

    tw = [dict() for _ in range(NB)]
    cw = [dict() for _ in range(NB)]
    rc = [dict() for _ in range(NB)]
    groups = [range(g0, min(g0 + RWKV_ROWS_IN_STEP, NB)) for g0 in range(0, NB, RWKV_ROWS_IN_STEP)]
    phases = [
        lambda rows: in_turn([tokenwise(nb, tw[nb]) for nb in rows]),
        lambda rows: in_turn([chunkwise(tw[nb], cw[nb]) for nb in rows]),
        lambda rows: in_turn([recurrent(nb, tw[nb], cw[nb], rc[nb]) for nb in rows]),
        lambda rows: in_turn([finish(nb, tw[nb], rc[nb]) for nb in rows]),
    ]
    for wave in range(len(groups) + len(phases) - 1):
        yield from in_turn([phases[wave - g](rows) for g, rows in enumerate(groups)
                            if 0 <= wave - g < len(phases)])


def _rwkv_kernel(*refs):
    for _ in _rwkv_stages(*refs):
        pass


def _rwkv_call_parts(p, w0, w_up, a0, a_up, g_up, k_k, k_a, r_k, ln_g, ln_b, ts, nb):
    B, S, P = p.shape
    W = RWKV_WIDTH
    wwa = jnp.zeros((LANES, 2 * W), F32)
    wwa = wwa.at[:DECAY_LORA, :W].set(w_up).at[DECAY_LORA:, W:].set(a_up)
    vec = lambda t: t.reshape(1, -1)
    row_spec = lambda n: pl.BlockSpec((1, n), lambda b, s: (0, 0))
    assert ts % MXU_DIM == 0 and S % ts == 0 and B % nb == 0
    args = (p, vec(w0), vec(a0), wwa, g_up, vec(k_k), vec(k_a), vec(r_k), vec(ln_g), vec(ln_b))
    in_specs = [
        pl.BlockSpec((nb, ts, P), lambda b, s: (b, s, 0)),
        row_spec(W), row_spec(W),
        pl.BlockSpec((LANES, 2 * W), lambda b, s: (0, 0)),
        pl.BlockSpec((GATE_LORA, W), lambda b, s: (0, 0)),
        row_spec(W), row_spec(W), row_spec(W), row_spec(W), row_spec(W),
    ]
    out_spec = pl.BlockSpec((nb, ts, W), lambda b, s: (b, s, 0))
    out_shape = jax.ShapeDtypeStruct((B, S, W), F32)
    scratch = [pltpu.VMEM((nb, W // LANES, LANES, LANES), F32)]
    return (B // nb, S // ts), args, in_specs, out_spec, out_shape, scratch


def _rwkv(p, *weights, ts=256, nb=2):
    B, S, _ = p.shape
    grid, args, in_specs, out_spec, out_shape, scratch = _rwkv_call_parts(p, *weights, min(ts, S), min(nb, B))
    return pl.pallas_call(
        _rwkv_kernel,
        grid=grid,
        in_specs=in_specs,
        out_specs=out_spec,
        out_shape=out_shape,
        scratch_shapes=scratch,
        compiler_params=_params("parallel", "arbitrary"),
        name="rwkv7",
    )(*args)


def _rel_bias_kernel(tab_ref, *refs, n_blocks, n_side):
    side_in, o_ref, side_out = refs[:n_side], refs[n_side], refs[n_side + 1:]
    blk = MOBA_BLOCK
    max_exact = REL_BUCKETS // 2

    def bucket_of(n):
        large = max_exact + math.floor(math.log(max(n, 1) / max_exact)
                                       / math.log(REL_MAX_DISTANCE / max_exact) * (REL_BUCKETS - max_exact))
        return n if n < max_exact else min(large, REL_BUCKETS - 1)

    def block(cb):
        kc = lax.broadcasted_iota(jnp.int32, (blk, blk), 0) + cb * blk
        qi = lax.broadcasted_iota(jnp.int32, (blk, blk), 1)
        dist = qi + (n_blocks - 1) * blk - kc
        n = jnp.maximum(dist, 0)
        nf = jnp.maximum(n, 1).astype(F32)
        large = max_exact + (jnp.log(nf / max_exact) / math.log(REL_MAX_DISTANCE / max_exact)
                             * (REL_BUCKETS - max_exact)).astype(jnp.int32)
        large = jnp.minimum(large, REL_BUCKETS - 1)
        bucket = jnp.where(n < max_exact, n, large)
        n_lo = max((n_blocks - 1 - cb) * blk - (blk - 1), 0)
        n_hi = max((n_blocks - 1 - cb) * blk + (blk - 1), 0)
        b_lo = max(bucket_of(n_lo) - 1, 0)
        b_hi = min(bucket_of(n_hi) + 1, REL_BUCKETS - 1)
        for h in range(MOBA_HEADS):
            tile = jnp.zeros((blk, blk), F32)
            for c in range(b_lo, b_hi + 1):
                tile = jnp.where(bucket == c, tab_ref[h, c], tile)
            o_ref[h] = jnp.where(dist < 0, -jnp.inf, tile * LOG2E)

    for cb in range(n_blocks):
        pl.when(pl.program_id(0) == cb)(functools.partial(block, cb))
    for w_ref, w16_ref in zip(side_in, side_out):
        w16_ref[...] = w_ref[...].astype(BF16)


def _rel_bias_strip(rel_bias, n_blocks, side=()):
    blk = MOBA_BLOCK
    side_args, side_in, side_out, side_shape = _side_cast_parts(side, n_blocks)
    strip, *side16 = pl.pallas_call(
        functools.partial(_rel_bias_kernel, n_blocks=n_blocks, n_side=len(side)),
        grid=(n_blocks,),
        in_specs=[pl.BlockSpec(memory_space=pltpu.SMEM)] + side_in,
        out_specs=[pl.BlockSpec((MOBA_HEADS, blk, blk), lambda c: (0, c, 0))] + side_out,
        out_shape=[jax.ShapeDtypeStruct((MOBA_HEADS, n_blocks * blk, blk), F32)] + side_shape,
        compiler_params=_params("parallel"),
        name="rel_bias_strip",
    )(rel_bias, *side_args)
    return strip, [w16.reshape((1,) + w.shape[1:]) for w16, (w, _) in zip(side16, side)]


def _moba_stages(q_ref, k_ref, v_ref, bias_ref, o_ref, *, n_blocks):
    blk = MOBA_BLOCK
    lane = lax.broadcasted_iota(jnp.int32, (1, LANES), 1)
    vrow = lax.broadcasted_iota(jnp.int32, (LANES, 1), 0)
    q = q_ref[0].astype(F32) * (HEAD_DIM ** -0.5 * LOG2E)
    k16 = k_ref[0].astype(BF16)
    k = k16.astype(F32)
    vt = v_ref[0].astype(F32).T
    vt16 = [jnp.where((vrow >> HEAD_SHIFT) == e, vt, 1.0).astype(BF16) for e in range(HEADS_PER_TILE)]
    kmean = jnp.mean(k.reshape(n_blocks, blk, LANES), axis=1)

    q_hi, q_lo = _bf16_terms(q, 2)
    m_hi, m_lo = _bf16_terms(kmean, 2)
    gates, q16 = [], []
    for e in range(HEADS_PER_TILE):
        head_lanes = (lane >> HEAD_SHIFT) == e
        zero = jnp.zeros_like(q_hi)
        qe_hi, qe_lo = jnp.where(head_lanes, q_hi, zero), jnp.where(head_lanes, q_lo, zero)
        gates.append(_dot_nt(m_hi, qe_hi) + _dot_nt(m_hi, qe_lo) + _dot_nt(m_lo, qe_hi))
        q16.append(qe_hi)

    def scores(qb, e):
        return _dot_nt(k16[:(qb + 1) * blk], q16[e][qb * blk:(qb + 1) * blk])

    def weights(qb, e, s_t):
        rows = slice(qb * blk, (qb + 1) * blk)
        bias_lo = (n_blocks - 1 - qb) * blk
        g = [gates[e][j:j + 1, rows] for j in range(qb)]
        tiles = []
        for j in range(qb + 1):
            t = s_t[j * blk:(j + 1) * blk] + bias_ref[e, bias_lo + j * blk:bias_lo + (j + 1) * blk, :]
            if j < qb:
                rank = jnp.zeros((1, blk), jnp.int32)
                for jj in range(qb):
                    if jj != j:
                        ahead = (g[jj] >= g[j]) if jj < j else (g[jj] > g[j])
                        rank = rank + jnp.where(ahead, 1, 0)
                t = jnp.where(rank < MOBA_TOPK, t, -jnp.inf)
            tiles.append(t)
        m = tiles[0].max(axis=0, keepdims=True)
        for t in tiles[1:]:
            m = jnp.maximum(m, t.max(axis=0, keepdims=True))
        return jnp.concatenate([jnp.exp2(t - m).astype(BF16) for t in tiles], axis=0)

    def attend(qb, e, p16):
        pv = _dot(vt16[e][:, :(qb + 1) * blk], p16)
        den_row = (1 - e) * HEAD_DIM
        return pv / pv[den_row:den_row + 1]

    units = [(qb, e) for qb in range(n_blocks) for e in range(HEADS_PER_TILE)]
    s_t, p16, out_t = {}, {}, {}
    for step in range(len(units) + 2):
        if 0 <= step - 2 < len(units):
            u = units[step - 2]
            out_t[u] = attend(*u, p16.pop(u))
            qb, e = u
            if e == HEADS_PER_TILE - 1:
                o_ref[0, qb * blk:(qb + 1) * blk, :] = jnp.where(
                    vrow < HEAD_DIM, out_t.pop((qb, 0)), out_t.pop((qb, 1))).T
        if step < len(units):
            s_t[units[step]] = scores(*units[step])
        if 0 <= step - 1 < len(units):
            u = units[step - 1]
            p16[u] = weights(*u, s_t.pop(u))
        yield


def _moba_kernel(*refs, n_blocks):
    for _ in _moba_stages(*refs, n_blocks=n_blocks):
        pass


def _moba(qkv, bias_strip):
    B, S, _ = qkv.shape
    blk = MOBA_BLOCK
    n_blocks = S // blk
    n_pairs = MOBA_WIDTH // LANES
    return pl.pallas_call(
        functools.partial(_moba_kernel, n_blocks=n_blocks),
        grid=(n_pairs, B),
        in_specs=[
            pl.BlockSpec((1, S, LANES), lambda p, b: (b, 0, p)),
            pl.BlockSpec((1, S, LANES), lambda p, b: (b, 0, n_pairs + p)),
            pl.BlockSpec((1, S, LANES), lambda p, b: (b, 0, 2 * n_pairs + p)),
            pl.BlockSpec((HEADS_PER_TILE, S, blk), lambda p, b: (p, 0, 0)),
        ],
        out_specs=pl.BlockSpec((1, S, LANES), lambda p, b: (b, 0, p)),
        out_shape=jax.ShapeDtypeStruct((B, S, MOBA_WIDTH), F32),
        compiler_params=_params("parallel", "parallel"),
        name="moba",
    )(qkv, qkv, qkv, bias_strip)


N_RWKV_INPUTS = 10
N_MOBA_INPUTS = 4


def _mixers_kernel(*refs, n_blocks):
    rwkv_in, refs = refs[:N_RWKV_INPUTS], refs[N_RWKV_INPUTS:]
    moba_in, (y_rwkv_ref, y_moba_ref, state_ref) = refs[:N_MOBA_INPUTS], refs[N_MOBA_INPUTS:]
    jobs = [(_rwkv_stages(*rwkv_in, y_rwkv_ref, state_ref), MIXER_STAGE_RATIO[0]),
            (_moba_stages(*moba_in, y_moba_ref, n_blocks=n_blocks), MIXER_STAGE_RATIO[1])]
    while jobs:
        jobs = [(gen, n) for gen, n in jobs if all(next(gen, jobs) is not jobs for _ in range(n))]


def _mixers(p, qkv, bias_strip, *rwkv_weights, ts=256, nb=2):
    B, S, _ = qkv.shape
    blk = MOBA_BLOCK
    n_blocks = S // blk
    n_pairs = MOBA_WIDTH // LANES
    ts, nb = min(ts, S), min(nb, B)
    grid, args, in_specs, out_spec, out_shape, scratch = _rwkv_call_parts(p, *rwkv_weights, ts, nb)
    if grid[0] * grid[1] != n_pairs * B:
        return _rwkv(p, *rwkv_weights, ts=ts, nb=nb), _moba(qkv, bias_strip)
    item = lambda g, s: g * grid[1] + s
    row = lambda g, s: item(g, s) % B
    pair = lambda g, s: item(g, s) // B
    in_specs = in_specs + [
        pl.BlockSpec((1, S, LANES), lambda g, s: (row(g, s), 0, pair(g, s))),
        pl.BlockSpec((1, S, LANES), lambda g, s: (row(g, s), 0, n_pairs + pair(g, s))),
        pl.BlockSpec((1, S, LANES), lambda g, s: (row(g, s), 0, 2 * n_pairs + pair(g, s))),
        pl.BlockSpec((HEADS_PER_TILE, S, blk), lambda g, s: (pair(g, s), 0, 0)),
    ]
    assert len(args) == N_RWKV_INPUTS
    return pl.pallas_call(
        functools.partial(_mixers_kernel, n_blocks=n_blocks),
        grid=grid,
        in_specs=in_specs,
        out_specs=[out_spec, pl.BlockSpec((1, S, LANES), lambda g, s: (row(g, s), 0, pair(g, s)))],
        out_shape=[out_shape, jax.ShapeDtypeStruct((B, S, MOBA_WIDTH), F32)],
        scratch_shapes=scratch,
        compiler_params=_params("parallel", "arbitrary"),
        name="mixers",
    )(*args, qkv, qkv, qkv, bias_strip)


def _xattn_kernel(x_ref, ya_ref, yb_ref, wa_ref, wb_ref, g_ref, wq_ref, mem_ref, mg_ref, wkv_ref, wo_ref,
                  o_ref, kv_ref, *, sub):
    D = x_ref.shape[-1]
    dh = D // XATTN_HEADS

    @pl.when(pl.program_id(1) == 0)
    def _():
        kv_ref[...] = _dot(_rms(mem_ref[0], mg_ref[...]).astype(BF16), wkv_ref[...]).astype(BF16)

    k16 = [kv_ref[:, i * dh:(i + 1) * dh] for i in range(XATTN_HEADS)]
    v16 = [kv_ref[:, D + i * dh:D + (i + 1) * dh] for i in range(XATTN_HEADS)]

    def project(r0):
        rows = slice(r0, r0 + sub)
        x = (x_ref[0, rows, :] + _dot(ya_ref[0, rows, :].astype(BF16), wa_ref[...])
             + _dot(yb_ref[0, rows, :].astype(BF16), wb_ref[...]))
        h = _rms(x, g_ref[...]).astype(BF16)
        q = (_dot(h, wq_ref[...]) * (dh ** -0.5 * LOG2E)).astype(BF16)
        return x, q

    def attend(q):
        hs = range(XATTN_HEADS)
        s = [_dot_nt(q[:, i * dh:(i + 1) * dh], k16[i]) for i in hs]
        pr = [jnp.exp2(s[i] - jnp.max(s[i], axis=-1, keepdims=True)) for i in hs]
        pr = [(pr[i] / jnp.sum(pr[i], axis=-1, keepdims=True)).astype(BF16) for i in hs]
        return jnp.concatenate([_dot(pr[i], v16[i]) for i in hs], axis=1).astype(BF16)

    starts = list(range(0, x_ref.shape[1], sub))
    xq, att = {}, {}
    for step in range(len(starts) + 2):
        if step < len(starts):
            xq[starts[step]] = project(starts[step])
        if 0 <= step - 1 < len(starts):
            r0 = starts[step - 1]
            att[r0] = attend(xq[r0][1])
        if 0 <= step - 2 < len(starts):
            r0 = starts[step - 2]
            o_ref[0, r0:r0 + sub, :] = xq.pop(r0)[0] + _dot(att.pop(r0), wo_ref[...])


def _xattn(x, ya, yb, w_mix, g, w_q, mem, mem_g, w_kv, w_o, layer, *, tm=1024, sub=1024):
    B, S, D = x.shape
    M = mem.shape[1]
    na, nb = ya.shape[-1], yb.shape[-1]
    assert na == nb
    tm = min(tm, S)
    of_layer = lambda shape, blk=0: pl.BlockSpec((None,) + shape, lambda b, i: (layer, blk, 0),
                                                 pipeline_mode=pl.Buffered(1))
    return pl.pallas_call(
        functools.partial(_xattn_kernel, sub=min(sub, tm)),
        grid=(B, S // tm),
        in_specs=[
            pl.BlockSpec((1, tm, D), lambda b, i: (b, i, 0)),
            pl.BlockSpec((1, tm, na), lambda b, i: (b, i, 0)),
            pl.BlockSpec((1, tm, nb), lambda b, i: (b, i, 0)),
            of_layer((na, D), 0),
            of_layer((nb, D), 1),
            pl.BlockSpec((1, D), lambda b, i: (0, 0)),
            of_layer((D, D)),
            pl.BlockSpec((1, M, D), lambda b, i: (b, 0, 0)),
            pl.BlockSpec((1, D), lambda b, i: (0, 0)),
            of_layer((D, 2 * D)),
            of_layer((D, D)),
        ],
        out_specs=pl.BlockSpec((1, tm, D), lambda b, i: (b, i, 0)),
        out_shape=jax.ShapeDtypeStruct((B, S, D), F32),
        scratch_shapes=[pltpu.VMEM((M, 2 * D), BF16)],
        compiler_params=_params("parallel", "arbitrary"),
        name="xattn",
    )(x, ya, yb, w_mix, w_mix, g.reshape(1, D), w_q, mem, mem_g.reshape(1, D), w_kv, w_o)


def kernel(x, mem, rel_bias, final_norm_g, ffn1_norm_g, ffn1_w_in, ffn1_w_out, mix_norm_g, w_mix_in, w_mix_out, rwkv_mu, rwkv_w0, rwkv_w_up, rwkv_a0, rwkv_a_up, rwkv_g_up, rwkv_k_k, rwkv_k_a, rwkv_r_k, rwkv_ln_g, rwkv_ln_b, xattn_norm_g, mem_norm_g, xattn_w_q, xattn_w_kv, xattn_w_o, ffn2_norm_g, ffn2_w_in, ffn2_w_out):
    B, S, D = x.shape
    M = mem.shape[1]
    depth = ffn1_w_in.shape[0]
    T = B * S
    bias_strip, (ffn1_in16, ffn1_out16) = _rel_bias_strip(rel_bias, S // MOBA_BLOCK,
                                                         side=[(ffn1_w_in, 0), (ffn1_w_out, 0)])
    x = x.reshape(T, D)
    for l in range(depth):
        later = (w_mix_in, xattn_w_kv, w_mix_out, xattn_w_q, xattn_w_o, ffn2_w_in, ffn2_w_out)
        x, (mix_in16, kv16, mix_out16, q16, o16, ffn2_in16, ffn2_out16) = _ffn(
            x, ffn1_norm_g[l], ffn1_in16, ffn1_out16, 0, side=[(w, l) for w in later])
        p_rwkv, qkv = _norm_proj(x, mix_norm_g[l], mix_in16, 0, (RWKV_PROJ, MOBA_PROJ),
                                 out_dtypes=(F32, BF16), shift_mu=rwkv_mu[l], seq_len=S)
        y_rwkv, y_moba = _mixers(p_rwkv.reshape(B, S, RWKV_PROJ), qkv.reshape(B, S, MOBA_PROJ), bias_strip,
                                 rwkv_w0[l], rwkv_w_up[l], rwkv_a0[l], rwkv_a_up[l], rwkv_g_up[l],
                                 rwkv_k_k[l], rwkv_k_a[l], rwkv_r_k[l], rwkv_ln_g[l], rwkv_ln_b[l])
        x = _xattn(x.reshape(B, S, D), y_rwkv, y_moba, mix_out16, xattn_norm_g[l],
                   q16, mem, mem_norm_g[l], kv16, o16, 0).reshape(T, D)
        last = l == depth - 1
        x, next_ffn1 = _ffn(x, ffn2_norm_g[l], ffn2_in16, ffn2_out16, 0, final_norm_g if last else None,
                            side=[] if last else [(ffn1_w_in, l + 1), (ffn1_w_out, l + 1)])
        if not last:
            ffn1_in16, ffn1_out16 = next_ffn1
    return x.reshape(B, S, D)
```
